```python
import math
import jax, jax.numpy as jnp
from jax import lax
import numpy as np

D_MODEL = 2048
BATCH = 2
SEQ = 4096
DEPTH = 1

ATT_HEADS = 8
ATT_QK_DIM = 128
ATT_V_DIM = 2 * ATT_QK_DIM
ATT_WIDTH = ATT_HEADS * ATT_V_DIM
ATT_Q_WIDTH = ATT_HEADS * 2 * ATT_QK_DIM
Q_BLOCK = 128
SSM_HEADS = 32
SSM_HEAD_DIM = 64
SSM_WIDTH = SSM_HEADS * SSM_HEAD_DIM
SSM_GROUPS = 4
SSM_STATE = 128
SSM_CONV = 4
SSM_CHUNK = 128
SSM_CONV_DIM = SSM_WIDTH + 2 * SSM_GROUPS * SSM_STATE
MIX_WIDTH = ATT_WIDTH + SSM_WIDTH
IN_WIDTH = 2 * ATT_Q_WIDTH + ATT_WIDTH + SSM_WIDTH + SSM_CONV_DIM + SSM_HEADS
D_FF = -(-8 * D_MODEL // (3 * 256)) * 256
RMS_EPS = 1e-6
SUB_EPS = 1e-5

kernel_name = "hymba_diffattn_ssd_hybrid_block"


def rmsnorm(x, w, eps=RMS_EPS):
    xf = x.astype(jnp.float32)
    y = xf * lax.rsqrt(jnp.mean(xf * xf, axis=-1, keepdims=True) + eps)
    return (y * w.astype(jnp.float32)).astype(x.dtype)


def alibi_slopes(n):
    start = 2.0 ** (-8.0 / n)
    return jnp.asarray([start ** (i + 1) for i in range(n)], dtype=jnp.float32)


def diff_attention(q, k, v, lam, subln_w, lambda_init):
    b, s = q.shape[0], q.shape[1]
    nb = s // Q_BLOCK
    scale = ATT_QK_DIM ** -0.5
    slopes = alibi_slopes(ATT_HEADS)
    kpos = jnp.arange(s)
    qb = q.reshape(b, nb, Q_BLOCK, ATT_HEADS, 2, ATT_QK_DIM).transpose(1, 0, 2, 3, 4, 5)

    def one_block(args):
        q_blk, start = args
        qpos = start + jnp.arange(Q_BLOCK)
        dist = qpos[:, None] - kpos[None, :]
        bias = -slopes[:, None, None] * dist.astype(jnp.float32)
        sc = jnp.einsum('bqhcd,bkhcd->bhcqk', q_blk, k).astype(jnp.float32) * scale
        sc = sc + bias[None, :, None]
        sc = jnp.where(dist >= 0, sc, -jnp.inf)
        p = jax.nn.softmax(sc, axis=-1)
        p = p[:, :, 0] - lam * p[:, :, 1]
        return jnp.einsum('bhqk,bkhe->bqhe', p.astype(v.dtype), v)

    o = lax.map(one_block, (qb, jnp.arange(nb) * Q_BLOCK))
    o = o.transpose(1, 0, 2, 3, 4).reshape(b, s, ATT_HEADS, ATT_V_DIM)
    o = rmsnorm(o, subln_w, SUB_EPS) * (1.0 - lambda_init)
    return o.reshape(b, s, ATT_WIDTH)


def causal_dwconv(u, w, bias):
    c = u.shape[-1]
    out = lax.conv_general_dilated(u, w[:, None, :], window_strides=(1,), padding=[(SSM_CONV - 1, 0)],
                                   dimension_numbers=('NWC', 'WIO', 'NWC'), feature_group_count=c)
    return out + bias


def ssd_scan(xdt, adt, bm, cm):
    b, s, h, p = xdt.shape
    c = s // SSM_CHUNK
    g, r, n, l = SSM_GROUPS, h // SSM_GROUPS, SSM_STATE, SSM_CHUNK
    X = xdt.reshape(b, c, l, g, r, p)
    A = adt.reshape(b, c, l, g, r).transpose(0, 3, 4, 1, 2)
    Bm = bm.reshape(b, c, l, g, n)
    Cm = cm.reshape(b, c, l, g, n)
    A_cs = jnp.cumsum(A, axis=-1)
    seg = A_cs[..., :, None] - A_cs[..., None, :]
    causal = jnp.tril(jnp.ones((l, l), dtype=bool))
    Lmat = jnp.exp(jnp.where(causal, seg, -jnp.inf))
    CB = jnp.einsum('bclgn,bcsgn->bcgls', Cm, Bm)
    y_diag = jnp.einsum('bcgls,bgrcls,bcsgrp->bclgrp', CB, Lmat, X)
    decay_states = jnp.exp(A_cs[..., -1:] - A_cs)
    states = jnp.einsum('bcsgn,bgrcs,bcsgrp->bcgrpn', Bm, decay_states, X)
    chunk_decay = jnp.exp(A_cs[..., -1])

    def step(hstate, inp):
        st, dec = inp
        new = hstate * dec[..., None, None] + st
        return new, hstate

    h0 = jnp.zeros((b, g, r, p, n), dtype=jnp.float32)
    _, prev = lax.scan(step, h0, (states.transpose(1, 0, 2, 3, 4, 5), chunk_decay.transpose(3, 0, 1, 2)))
    prev = prev.transpose(1, 0, 2, 3, 4, 5)
    y_off = jnp.einsum('bclgn,bcgrpn,bgrcl->bclgrp', Cm, prev, jnp.exp(A_cs))
    return (y_diag + y_off).reshape(b, s, h, p)


def ssd_mixer(z, xbc, dt_raw, conv_w, conv_b, dt_bias, a_log, d_skip, norm_w):
    b, s = z.shape[0], z.shape[1]
    xbc = jax.nn.silu(causal_dwconv(xbc, conv_w, conv_b))
    gn = SSM_GROUPS * SSM_STATE
    xs, bm, cm = jnp.split(xbc, [SSM_WIDTH, SSM_WIDTH + gn], axis=-1)
    xs = xs.reshape(b, s, SSM_HEADS, SSM_HEAD_DIM)
    dt = jax.nn.softplus(dt_raw.astype(jnp.float32) + dt_bias.astype(jnp.float32))
    A = -jnp.exp(a_log.astype(jnp.float32))
    xf = xs.astype(jnp.float32)
    y = ssd_scan(xf * dt[..., None], dt * A,
                 bm.reshape(b, s, SSM_GROUPS, SSM_STATE).astype(jnp.float32),
                 cm.reshape(b, s, SSM_GROUPS, SSM_STATE).astype(jnp.float32))
    y = y + d_skip.astype(jnp.float32)[:, None] * xf
    y = y.reshape(b, s, SSM_WIDTH) * jax.nn.silu(z.astype(jnp.float32))
    yg = y.reshape(b, s, SSM_GROUPS, SSM_WIDTH // SSM_GROUPS)
    yg = yg * lax.rsqrt(jnp.mean(yg * yg, axis=-1, keepdims=True) + SUB_EPS)
    y = yg.reshape(b, s, SSM_WIDTH) * norm_w.astype(jnp.float32)
    return y.astype(z.dtype)


def setup_inputs(seed: int = 0) -> dict:
    key = jax.random.key(seed)
    ks = jax.random.split(key, 20)
    f32 = jnp.float32
    nrm = lambda k, shape, sc: jax.random.normal(k, shape, f32) * sc
    dt_init = jnp.exp(jax.random.uniform(ks[8], (DEPTH, SSM_HEADS), f32, math.log(1e-3), math.log(1e-1)))
    return {
        "x": jax.random.normal(ks[0], (BATCH, SEQ, D_MODEL), f32),
        "norm_mix_w": 1.0 + nrm(ks[1], (DEPTH, D_MODEL), 0.02),
        "w_in": nrm(ks[2], (DEPTH, D_MODEL, IN_WIDTH), D_MODEL ** -0.5),
        "lambda_q1": nrm(ks[3], (DEPTH, ATT_QK_DIM), 0.1),
        "lambda_k1": nrm(ks[4], (DEPTH, ATT_QK_DIM), 0.1),
        "lambda_q2": nrm(ks[5], (DEPTH, ATT_QK_DIM), 0.1),
        "lambda_k2": nrm(ks[6], (DEPTH, ATT_QK_DIM), 0.1),
        "subln_w": 1.0 + nrm(ks[7], (DEPTH, ATT_V_DIM), 0.02),
        "conv_w": jax.random.uniform(ks[9], (DEPTH, SSM_CONV, SSM_CONV_DIM), f32, -0.5, 0.5),
        "conv_b": nrm(ks[10], (DEPTH, SSM_CONV_DIM), 0.01),
        "dt_bias": dt_init + jnp.log(-jnp.expm1(-dt_init)),
        "a_log": jnp.log(jax.random.uniform(ks[11], (DEPTH, SSM_HEADS), f32, 1.0, 16.0)),
        "d_skip": 1.0 + nrm(ks[12], (DEPTH, SSM_HEADS), 0.01),
        "ssm_norm_w": 1.0 + nrm(ks[13], (DEPTH, SSM_WIDTH), 0.02),
        "w_out": nrm(ks[14], (DEPTH, MIX_WIDTH, D_MODEL), MIX_WIDTH ** -0.5),
        "norm_ffn_w": 1.0 + nrm(ks[15], (DEPTH, D_MODEL), 0.02),
        "w_gate": nrm(ks[16], (DEPTH, D_MODEL, D_FF), D_MODEL ** -0.5),
        "w_up": nrm(ks[17], (DEPTH, D_MODEL, D_FF), D_MODEL ** -0.5),
        "w_down": nrm(ks[18], (DEPTH, D_FF, D_MODEL), D_FF ** -0.5),
        "norm_final_w": 1.0 + nrm(ks[19], (D_MODEL,), 0.02),
    }


def reference(x, norm_mix_w, w_in, lambda_q1, lambda_k1, lambda_q2, lambda_k2, subln_w,
              conv_w, conv_b, dt_bias, a_log, d_skip, ssm_norm_w, w_out,
              norm_ffn_w, w_gate, w_up, w_down, norm_final_w):
    b, s = x.shape[0], x.shape[1]
    offs = np.cumsum([ATT_Q_WIDTH, ATT_Q_WIDTH, ATT_WIDTH, SSM_WIDTH, SSM_CONV_DIM]).tolist()
    for layer in range(DEPTH):
        lambda_init = 0.8 - 0.6 * math.exp(-0.3 * layer)
        h = rmsnorm(x, norm_mix_w[layer])
        proj = h @ w_in[layer]
        q, k, v, z, xbc, dt_raw = jnp.split(proj, offs, axis=-1)
        q = q.reshape(b, s, ATT_HEADS, 2, ATT_QK_DIM)
        k = k.reshape(b, s, ATT_HEADS, 2, ATT_QK_DIM)
        v = v.reshape(b, s, ATT_HEADS, ATT_V_DIM)
        lam = (jnp.exp(jnp.sum(lambda_q1[layer].astype(jnp.float32) * lambda_k1[layer].astype(jnp.float32)))
               - jnp.exp(jnp.sum(lambda_q2[layer].astype(jnp.float32) * lambda_k2[layer].astype(jnp.float32)))
               + lambda_init)
        att = diff_attention(q, k, v, lam, subln_w[layer], lambda_init)
        ssm = ssd_mixer(z, xbc, dt_raw, conv_w[layer], conv_b[layer], dt_bias[layer],
                        a_log[layer], d_skip[layer], ssm_norm_w[layer])
        x = x + jnp.concatenate([att, ssm], axis=-1) @ w_out[layer]
        h = rmsnorm(x, norm_ffn_w[layer])
        x = x + (jax.nn.silu(h @ w_gate[layer]) * (h @ w_up[layer])) @ w_down[layer]
    return rmsnorm(x, norm_final_w)
```

```python
import functools
import math

import jax
import jax.numpy as jnp
from jax import lax
from jax.experimental import pallas as pl
from jax.experimental.pallas import tpu as pltpu

F32 = jnp.float32
BF16 = jnp.bfloat16

D_MODEL = 2048
ATT_HEADS = 8
ATT_QK_DIM = 128
ATT_V_DIM = 2 * ATT_QK_DIM
ATT_WIDTH = ATT_HEADS * ATT_V_DIM
SSM_HEADS = 32
SSM_HEAD_DIM = 64
SSM_WIDTH = SSM_HEADS * SSM_HEAD_DIM
SSM_GROUPS = 4
SSM_STATE = 128
SSM_CONV = 4
SSM_CHUNK = 128
SSM_BC_WIDTH = 2 * SSM_GROUPS * SSM_STATE
MIX_WIDTH = ATT_WIDTH + SSM_WIDTH
PROJ_MAIN = 2 * ATT_WIDTH + ATT_WIDTH + SSM_WIDTH + SSM_WIDTH + SSM_BC_WIDTH
D_FF = 5632
RMS_EPS = 1e-6
SUB_EPS = 1e-5
LOG2E = 1.4426950408889634
LANES = 128
HEADS_PER_GROUP = SSM_HEADS // SSM_GROUPS
PAIR_WIDTH = 2 * SSM_HEAD_DIM
NEG_INF = float("-inf")

COL_Q, COL_K, COL_V = 0, ATT_WIDTH, 2 * ATT_WIDTH
COL_Z = 3 * ATT_WIDTH
COL_XS = COL_Z + SSM_WIDTH
COL_BC = COL_XS + SSM_WIDTH

VMEM_LIMIT = 56 * 1024 * 1024


def _params(semantics):
    return pltpu.CompilerParams(dimension_semantics=semantics, vmem_limit_bytes=VMEM_LIMIT)


def _sigmoid(x):
    return 1.0 / (1.0 + jnp.exp(-x))


def _in_proj_kernel(x_ref, nw_ref, w_ref, wdt_ref, cs_ref, proj_ref, dt_ref, h_ref):
    @pl.when(pl.program_id(1) == 0)
    def _():
        x = x_ref[...]
        ms = jnp.mean(x * x, axis=-1, keepdims=True)
        hb = ((x * lax.rsqrt(ms + RMS_EPS)) * nw_ref[...]).astype(BF16)
        h_ref[...] = hb
        dt_ref[...] = jnp.dot(hb, wdt_ref[...], preferred_element_type=F32)

    acc = jnp.dot(h_ref[...], w_ref[...], preferred_element_type=F32)
    proj_ref[...] = (acc * cs_ref[...]).astype(BF16)


def _in_proj(x2d, norm_w, w_main, w_dt, col_scale, tm=1024, tn=1024):
    m = x2d.shape[0]
    n = w_main.shape[1]
    return pl.pallas_call(
        _in_proj_kernel,
        grid=(m // tm, n // tn),
        in_specs=[
            pl.BlockSpec((tm, D_MODEL), lambda i, j: (i, 0)),
            pl.BlockSpec((1, D_MODEL), lambda i, j: (0, 0)),
            pl.BlockSpec((D_MODEL, tn), lambda i, j: (0, j)),
            pl.BlockSpec((D_MODEL, LANES), lambda i, j: (0, 0)),
            pl.BlockSpec((1, tn), lambda i, j: (0, j)),
        ],
        out_specs=[
            pl.BlockSpec((tm, tn), lambda i, j: (i, j)),
            pl.BlockSpec((tm, LANES), lambda i, j: (i, 0)),
        ],
        out_shape=[
            jax.ShapeDtypeStruct((m, n), BF16),
            jax.ShapeDtypeStruct((m, LANES), F32),
        ],
        scratch_shapes=[pltpu.VMEM((tm, D_MODEL), BF16)],
        compiler_params=_params(("parallel", "arbitrary")),
        name="in_proj",
    )(x2d, norm_w, w_main, w_dt, col_scale)


def _attn_kernel(slope_ref, lamv_ref, subw_ref, q_ref, k_ref, v_ref, o_ref,
                 m_scr, l_scr, acc_scr, *, tile, lambda_init):
    h = pl.program_id(1)
    qi = pl.program_id(2)
    slope2 = slope_ref[h]
    kbias = slope2 * lax.broadcasted_iota(jnp.int32, (1, tile), 1).astype(F32)
    row = lax.broadcasted_iota(jnp.int32, (tile, tile), 0)
    col = lax.broadcasted_iota(jnp.int32, (tile, tile), 1)
    causal = col <= row

    m_scr[...] = jnp.full(m_scr.shape, NEG_INF, F32)
    l_scr[...] = jnp.zeros(l_scr.shape, F32)
    acc_scr[...] = jnp.zeros(acc_scr.shape, F32)

    def step(j, masked):
        k0 = pl.multiple_of(j * tile, tile)
        kblk = k_ref[pl.ds(k0, tile), :]
        vblk = v_ref[pl.ds(k0, tile), :]
        off = slope2 * (j * tile).astype(F32)
        for c in range(2):
            qc = q_ref[:, c * ATT_QK_DIM:(c + 1) * ATT_QK_DIM]
            kc = kblk[:, c * ATT_QK_DIM:(c + 1) * ATT_QK_DIM]
            s = lax.dot_general(qc, kc, (((1,), (1,)), ((), ())), preferred_element_type=F32)
            s = s + kbias
            if masked:
                s = jnp.where(causal, s, NEG_INF)
            m_old = m_scr[c]
            m_new = jnp.maximum(m_old, jnp.max(s, axis=1, keepdims=True) + off)
            alpha = jnp.exp2(m_old - m_new)
            p = jnp.exp2(s - (m_new - off))
            l_scr[c] = alpha * l_scr[c] + jnp.sum(p, axis=1, keepdims=True)
            acc_scr[c] = alpha * acc_scr[c] + jnp.dot(p.astype(BF16), vblk,
                                                      preferred_element_type=F32)
            m_scr[c] = m_new

    def body(j, carry):
        step(j, False)
        return carry

    lax.fori_loop(0, qi, body, 0)
    step(qi, True)

    lv = lamv_ref[...]
    lam = (jnp.exp(jnp.sum(lv[0:1] * lv[1:2], axis=1, keepdims=True))
           - jnp.exp(jnp.sum(lv[2:3] * lv[3:4], axis=1, keepdims=True)) + lambda_init)
    o = acc_scr[0] / l_scr[0] - lam * (acc_scr[1] / l_scr[1])
    ms = jnp.mean(o * o, axis=-1, keepdims=True)
    o = (o * lax.rsqrt(ms + SUB_EPS)) * subw_ref[...]
    o_ref[...] = (o * (1.0 - lambda_init)).astype(BF16)


def _attention(proj, slopes2, lamv, subw, batch, seq, lambda_init, tile=512):
    nq = seq // tile
    kern = functools.partial(_attn_kernel, tile=tile, lambda_init=lambda_init)
    kb, vb = COL_K // ATT_V_DIM, COL_V // ATT_V_DIM
    return pl.pallas_call(
        kern,
        grid=(batch, ATT_HEADS, nq),
        in_specs=[
            pl.BlockSpec(memory_space=pltpu.SMEM),
            pl.BlockSpec((4, ATT_QK_DIM), lambda b, h, i: (0, 0)),
            pl.BlockSpec((1, ATT_V_DIM), lambda b, h, i: (0, 0)),
            pl.BlockSpec((tile, ATT_V_DIM), lambda b, h, i: (b * nq + i, h)),
            pl.BlockSpec((seq, ATT_V_DIM), lambda b, h, i: (b, kb + h)),
            pl.BlockSpec((seq, ATT_V_DIM), lambda b, h, i: (b, vb + h)),
        ],
        out_specs=pl.BlockSpec((tile, ATT_V_DIM), lambda b, h, i: (b * nq + i, h)),
        out_shape=jax.ShapeDtypeStruct((batch * seq, MIX_WIDTH), BF16),
        scratch_shapes=[
            pltpu.VMEM((2, tile, 1), F32),
            pltpu.VMEM((2, tile, 1), F32),
            pltpu.VMEM((2, tile, ATT_V_DIM), F32),
        ],
        compiler_params=_params(("parallel", "parallel", "arbitrary")),
        name="diff_attention",
    )(slopes2, lamv, subw, proj, proj, proj)


def _ssd_kernel(z_ref, xs_ref, bc_ref, dt_ref, cwx_ref, cwbc_ref, cbx_ref, cbbc_ref,
                dtb_ref, alog_ref, dskip_ref, nw_ref, mix_ref, out_ref,
                tailx, tailbc, state, yscr):
    del mix_ref
    L = SSM_CHUNK

    @pl.when(pl.program_id(1) == 0)
    def _():
        tailx[...] = jnp.zeros(tailx.shape, F32)
        tailbc[...] = jnp.zeros(tailbc.shape, F32)
        state[...] = jnp.zeros(state.shape, F32)

    def conv_silu(u_ref, tail, w_ref, b_ref):
        u = u_ref[...].astype(F32)
        full = jnp.concatenate([tail[...], u], axis=0)
        acc = b_ref[...] + w_ref[3:4, :] * u
        for k in range(SSM_CONV - 1):
            lo = 8 - (SSM_CONV - 1) + k
            acc = acc + w_ref[k:k + 1, :] * full[lo:lo + L]
        tail[...] = u[L - 8:L]
        return acc * _sigmoid(acc)

    xs = conv_silu(xs_ref, tailx, cwx_ref, cbx_ref)
    bc = conv_silu(bc_ref, tailbc, cwbc_ref, cbbc_ref)
    xs_b = xs.astype(BF16)
    bc_b = bc.astype(BF16)

    dtr = dt_ref[...] + dtb_ref[...]
    dt = jnp.maximum(dtr, 0.0) + jnp.log1p(jnp.exp(-jnp.abs(dtr)))
    acs = dt * (-jnp.exp(alog_ref[...]))
    rowi = lax.broadcasted_iota(jnp.int32, (L, LANES), 0)
    shift = 1
    while shift < L:
        acs = acs + jnp.where(rowi >= shift, pltpu.roll(acs, shift, axis=0), 0.0)
        shift *= 2
    acs_t = acs.T
    dt_t = dt.T
    exp_a = jnp.exp(acs)
    w_t = dt_t * jnp.exp(acs_t[:, L - 1:L] - acs_t)
    cd_row = jnp.exp(acs[L - 1:L, :])

    li = lax.broadcasted_iota(jnp.int32, (L, L), 0)
    si = lax.broadcasted_iota(jnp.int32, (L, L), 1)
    causal = si <= li
    lane = lax.broadcasted_iota(jnp.int32, (L, PAIR_WIDTH), 1)
    lo_half = lane < SSM_HEAD_DIM
    lane_row = lax.broadcasted_iota(jnp.int32, (1, PAIR_WIDTH), 1) < SSM_HEAD_DIM
    zero_b = jnp.zeros((L, PAIR_WIDTH), BF16)

    gn = SSM_GROUPS * SSM_STATE
    for g in range(SSM_GROUPS):
        b_g = bc[:, g * SSM_STATE:(g + 1) * SSM_STATE]
        c_g = bc[:, gn + g * SSM_STATE:gn + (g + 1) * SSM_STATE]
        cb = lax.dot_general(bc_b[:, gn + g * SSM_STATE:gn + (g + 1) * SSM_STATE],
                             bc_b[:, g * SSM_STATE:(g + 1) * SSM_STATE],
                             (((1,), (1,)), ((), ())), preferred_element_type=F32)
        b_gt = b_g.T
        for pr in range(HEADS_PER_GROUP // 2):
            pair = g * (HEADS_PER_GROUP // 2) + pr
            lhs_y, lhs_s = [], []
            for hh in (2 * pair, 2 * pair + 1):
                seg = acs[:, hh:hh + 1] - acs_t[hh:hh + 1, :]
                decay = jnp.exp(jnp.where(causal, seg, NEG_INF))
                lhs_y.append((cb * decay * dt_t[hh:hh + 1, :]).astype(BF16))
                lhs_s.append((b_gt * w_t[hh:hh + 1, :]).astype(BF16))
            for hh in (2 * pair, 2 * pair + 1):
                lhs_y.append((c_g * exp_a[:, hh:hh + 1]).astype(BF16))
            x_p = xs_b[:, pair * PAIR_WIDTH:(pair + 1) * PAIR_WIDTH]
            x_lo = jnp.where(lo_half, x_p, zero_b)
            x_hi = jnp.where(lo_half, zero_b, x_p)
            st = state[pair]
            st_b = st.astype(BF16)
            st_lo = jnp.where(lo_half, st_b, zero_b)
            st_hi = jnp.where(lo_half, zero_b, st_b)
            y = jnp.dot(jnp.concatenate(lhs_y, axis=1),
                        jnp.concatenate([x_lo, x_hi, st_lo, st_hi], axis=0),
                        preferred_element_type=F32)
            yscr[:, pair * PAIR_WIDTH:(pair + 1) * PAIR_WIDTH] = y
            new = jnp.dot(jnp.concatenate(lhs_s, axis=1),
                          jnp.concatenate([x_lo, x_hi], axis=0),
                          preferred_element_type=F32)
            cd = jnp.where(lane_row, cd_row[:, 2 * pair:2 * pair + 1],
                           cd_row[:, 2 * pair + 1:2 * pair + 2])
            state[pair] = st * cd + new

    y = yscr[...] + dskip_ref[...] * xs
    z = z_ref[...].astype(F32)
    y = y * (z * _sigmoid(z))
    gw = SSM_WIDTH // SSM_GROUPS
    for g in range(SSM_GROUPS):
        yg = y[:, g * gw:(g + 1) * gw]
        ms = jnp.mean(yg * yg, axis=-1, keepdims=True)
        out_ref[:, g * gw:(g + 1) * gw] = (
            (yg * lax.rsqrt(ms + SUB_EPS)) * nw_ref[:, g * gw:(g + 1) * gw]).astype(BF16)


def _ssd(proj, dt_raw, conv_w, conv_b, dt_bias, a_log, d_skip_vec, norm_w, mix, batch, seq):
    nc = seq // SSM_CHUNK
    L = SSM_CHUNK
    row = lambda b, c: b * nc + c
    full = lambda shape: pl.BlockSpec(shape, lambda b, c: (0, 0))
    cwx, cwbc = conv_w[:, :SSM_WIDTH], conv_w[:, SSM_WIDTH:]
    cbx, cbbc = conv_b[:, :SSM_WIDTH], conv_b[:, SSM_WIDTH:]
    return pl.pallas_call(
        _ssd_kernel,
        grid=(batch, nc),
        in_specs=[
            pl.BlockSpec((L, SSM_WIDTH), lambda b, c: (row(b, c), COL_Z // SSM_WIDTH)),
            pl.BlockSpec((L, SSM_WIDTH), lambda b, c: (row(b, c), COL_XS // SSM_WIDTH)),
            pl.BlockSpec((L, SSM_BC_WIDTH), lambda b, c: (row(b, c), COL_BC // SSM_BC_WIDTH)),
            pl.BlockSpec((L, LANES), lambda b, c: (row(b, c), 0)),
            full((SSM_CONV, SSM_WIDTH)),
            full((SSM_CONV, SSM_BC_WIDTH)),
            full((1, SSM_WIDTH)),
            full((1, SSM_BC_WIDTH)),
            full((1, LANES)),
            full((1, LANES)),
            full((1, SSM_WIDTH)),
            full((1, SSM_WIDTH)),
            pl.BlockSpec(memory_space=pl.ANY),
        ],
        out_specs=pl.BlockSpec((L, SSM_WIDTH), lambda b, c: (row(b, c), ATT_WIDTH // SSM_WIDTH)),
        out_shape=jax.ShapeDtypeStruct(mix.shape, mix.dtype),
        scratch_shapes=[
            pltpu.VMEM((8, SSM_WIDTH), F32),
            pltpu.VMEM((8, SSM_BC_WIDTH), F32),
            pltpu.VMEM((SSM_HEADS // 2, SSM_STATE, PAIR_WIDTH), F32),
            pltpu.VMEM((L, SSM_WIDTH), F32),
        ],
        input_output_aliases={12: 0},
        compiler_params=_params(("parallel", "arbitrary")),
        name="ssd_mixer",
    )(proj, proj, proj, dt_raw, cwx, cwbc, cbx, cbbc, dt_bias, a_log, d_skip_vec, norm_w, mix)


def _out_proj_kernel(a_ref, w_ref, x_ref, nw_ref, x1_ref, h_ref, acc_ref):
    k = pl.program_id(1)

    @pl.when(k == 0)
    def _():
        acc_ref[...] = x_ref[...]

    acc_ref[...] += jnp.dot(a_ref[...], w_ref[...], preferred_element_type=F32)

    @pl.when(k == pl.num_programs(1) - 1)
    def _():
        x1 = acc_ref[...]
        x1_ref[...] = x1
        ms = jnp.mean(x1 * x1, axis=-1, keepdims=True)
        h_ref[...] = ((x1 * lax.rsqrt(ms + RMS_EPS)) * nw_ref[...]).astype(BF16)


def _out_proj(mix, w_out, x2d, norm_w, tm=512, tk=1024):
    m, kdim = mix.shape
    return pl.pallas_call(
        _out_proj_kernel,
        grid=(m // tm, kdim // tk),
        in_specs=[
            pl.BlockSpec((tm, tk), lambda i, k: (i, k)),
            pl.BlockSpec((tk, D_MODEL), lambda i, k: (k, 0)),
            pl.BlockSpec((tm, D_MODEL), lambda i, k: (i, 0)),
            pl.BlockSpec((1, D_MODEL), lambda i, k: (0, 0)),
        ],
        out_specs=[
            pl.BlockSpec((tm, D_MODEL), lambda i, k: (i, 0)),
            pl.BlockSpec((tm, D_MODEL), lambda i, k: (i, 0)),
        ],
        out_shape=[
            jax.ShapeDtypeStruct((m, D_MODEL), F32),
            jax.ShapeDtypeStruct((m, D_MODEL), BF16),
        ],
        scratch_shapes=[pltpu.VMEM((tm, D_MODEL), F32)],
        compiler_params=_params(("parallel", "arbitrary")),
        name="out_proj",
    )(mix, w_out, x2d, norm_w)


def _gate_up_kernel(h_ref, wg_ref, wu_ref, a_ref):
    h = h_ref[...]
    g = jnp.dot(h, wg_ref[...], preferred_element_type=F32)
    u = jnp.dot(h, wu_ref[...], preferred_element_type=F32)
    a_ref[...] = ((g * _sigmoid(g)) * u).astype(BF16)


def _gate_up(h, w_gate, w_up, tm=1024, tn=512):
    m = h.shape[0]
    n = w_gate.shape[1]
    return pl.pallas_call(
        _gate_up_kernel,
        grid=(m // tm, n // tn),
        in_specs=[
            pl.BlockSpec((tm, D_MODEL), lambda i, j: (i, 0)),
            pl.BlockSpec((D_MODEL, tn), lambda i, j: (0, j)),
            pl.BlockSpec((D_MODEL, tn), lambda i, j: (0, j)),
        ],
        out_specs=pl.BlockSpec((tm, tn), lambda i, j: (i, j)),
        out_shape=jax.ShapeDtypeStruct((m, n), BF16),
        compiler_params=_params(("parallel", "arbitrary")),
        name="gate_up",
    )(h, w_gate, w_up)


def _down_kernel(a_ref, w_ref, x_ref, nw_ref, o_ref, acc_ref):
    k = pl.program_id(1)

    @pl.when(k == 0)
    def _():
        acc_ref[...] = x_ref[...]

    acc_ref[...] += jnp.dot(a_ref[...], w_ref[...], preferred_element_type=F32)

    @pl.when(k == pl.num_programs(1) - 1)
    def _():
        x2 = acc_ref[...]
        ms = jnp.mean(x2 * x2, axis=-1, keepdims=True)
        o_ref[...] = (x2 * lax.rsqrt(ms + RMS_EPS)) * nw_ref[...]


def _down(a, w_down, x1, norm_w, tm=512, tk=1408):
    m, kdim = a.shape
    return pl.pallas_call(
        _down_kernel,
        grid=(m // tm, kdim // tk),
        in_specs=[
            pl.BlockSpec((tm, tk), lambda i, k: (i, k)),
            pl.BlockSpec((tk, D_MODEL), lambda i, k: (k, 0)),
            pl.BlockSpec((tm, D_MODEL), lambda i, k: (i, 0)),
            pl.BlockSpec((1, D_MODEL), lambda i, k: (0, 0)),
        ],
        out_specs=pl.BlockSpec((tm, D_MODEL), lambda i, k: (i, 0)),
        out_shape=jax.ShapeDtypeStruct((m, D_MODEL), F32),
        scratch_shapes=[pltpu.VMEM((tm, D_MODEL), F32)],
        compiler_params=_params(("parallel", "arbitrary")),
        name="down_proj",
    )(a, w_down, x1, norm_w)


def _alibi_slopes_log2(n):
    start = 2.0 ** (-8.0 / n)
    return jnp.asarray([start ** (i + 1) for i in range(n)], dtype=F32) * LOG2E


def _pad_lanes(v):
    return jnp.pad(v.astype(F32), (0, LANES - v.shape[0])).reshape(1, LANES)


def kernel(x, norm_mix_w, w_in, lambda_q1, lambda_k1, lambda_q2, lambda_k2, subln_w, conv_w, conv_b, dt_bias, a_log, d_skip, ssm_norm_w, w_out, norm_ffn_w, w_gate, w_up, w_down, norm_final_w):
    batch, seq, _ = x.shape
    assert w_in.shape[0] == 1, "single-layer block"
    layer = 0
    x2d = x.reshape(batch * seq, D_MODEL)
    col_scale = jnp.concatenate([
        jnp.full((ATT_WIDTH,), ATT_QK_DIM ** -0.5 * LOG2E, F32),
        jnp.ones((PROJ_MAIN - ATT_WIDTH,), F32)]).reshape(1, PROJ_MAIN)
    slopes2 = _alibi_slopes_log2(ATT_HEADS)
    lambda_init = 0.8 - 0.6 * math.exp(-0.3 * layer)
    w_main = w_in[layer, :, :PROJ_MAIN].astype(BF16)
    w_dt = jnp.pad(w_in[layer, :, PROJ_MAIN:], ((0, 0), (0, LANES - SSM_HEADS))).astype(BF16)
    proj, dt_raw = _in_proj(x2d, norm_mix_w[layer].reshape(1, D_MODEL), w_main, w_dt, col_scale)
    lamv = jnp.stack([lambda_q1[layer], lambda_k1[layer], lambda_q2[layer], lambda_k2[layer]]).astype(F32)
    mix = _attention(proj, slopes2, lamv, subln_w[layer].reshape(1, ATT_V_DIM).astype(F32),
                     batch, seq, lambda_init)
    mix = _ssd(proj, dt_raw, conv_w[layer], conv_b[layer].reshape(1, -1),
               _pad_lanes(dt_bias[layer]), _pad_lanes(a_log[layer]),
               jnp.repeat(d_skip[layer].astype(F32), SSM_HEAD_DIM).reshape(1, SSM_WIDTH),
               ssm_norm_w[layer].reshape(1, SSM_WIDTH), mix, batch, seq)
    x1, h = _out_proj(mix, w_out[layer].astype(BF16), x2d, norm_ffn_w[layer].reshape(1, D_MODEL))
    a = _gate_up(h, w_gate[layer].astype(BF16), w_up[layer].astype(BF16))
    out = _down(a, w_down[layer].astype(BF16), x1, norm_final_w.reshape(1, D_MODEL))
    return out.reshape(batch, seq, D_MODEL)
```

```python
import functools
import math

import jax
import jax.numpy as jnp
from jax import lax
from jax.experimental import pallas as pl
from jax.experimental.pallas import tpu as pltpu

F32 = jnp.float32
BF16 = jnp.bfloat16

D_MODEL = 2048
ATT_HEADS = 8
ATT_QK_DIM = 128
ATT_V_DIM = 2 * ATT_QK_DIM
ATT_WIDTH = ATT_HEADS * ATT_V_DIM
SSM_HEADS = 32
SSM_HEAD_DIM = 64
SSM_WIDTH = SSM_HEADS * SSM_HEAD_DIM
SSM_GROUPS = 4
SSM_STATE = 128
SSM_CONV = 4
SSM_CHUNK = 128
SSM_BC_WIDTH = 2 * SSM_GROUPS * SSM_STATE
MIX_WIDTH = ATT_WIDTH + SSM_WIDTH
PROJ_MAIN = 2 * ATT_WIDTH + ATT_WIDTH + SSM_WIDTH + SSM_WIDTH + SSM_BC_WIDTH
D_FF = 5632
RMS_EPS = 1e-6
SUB_EPS = 1e-5
LOG2E = 1.4426950408889634
LANES = 128
HEADS_PER_GROUP = SSM_HEADS // SSM_GROUPS
PAIR_WIDTH = 2 * SSM_HEAD_DIM
NEG_INF = float("-inf")

COL_Q, COL_K, COL_V = 0, ATT_WIDTH, 2 * ATT_WIDTH
COL_Z = 3 * ATT_WIDTH
COL_XS = COL_Z + SSM_WIDTH
COL_BC = COL_XS + SSM_WIDTH

VMEM_LIMIT = 56 * 1024 * 1024


def _params(semantics):
    return pltpu.CompilerParams(dimension_semantics=semantics, vmem_limit_bytes=VMEM_LIMIT)


def _sigmoid(x):
    return 1.0 / (1.0 + jnp.exp(-x))


def _in_proj_kernel(x_ref, nw_ref, w_ref, wdt_ref, cs_ref, proj_ref, dt_ref, h_ref):
    @pl.when(pl.program_id(1) == 0)
    def _():
        x = x_ref[...]
        ms = jnp.mean(x * x, axis=-1, keepdims=True)
        hb = ((x * lax.rsqrt(ms + RMS_EPS)) * nw_ref[...]).astype(BF16)
        h_ref[...] = hb
        dt_ref[...] = jnp.dot(hb, wdt_ref[...], preferred_element_type=F32)

    acc = jnp.dot(h_ref[...], w_ref[...], preferred_element_type=F32)
    proj_ref[...] = (acc * cs_ref[...]).astype(BF16)


def _in_proj(x2d, norm_w, w_main, w_dt, col_scale, tm=1024, tn=1024):
    m = x2d.shape[0]
    n = w_main.shape[1]
    return pl.pallas_call(
        _in_proj_kernel,
        grid=(m // tm, n // tn),
        in_specs=[
            pl.BlockSpec((tm, D_MODEL), lambda i, j: (i, 0)),
            pl.BlockSpec((1, D_MODEL), lambda i, j: (0, 0)),
            pl.BlockSpec((D_MODEL, tn), lambda i, j: (0, j)),
            pl.BlockSpec((D_MODEL, LANES), lambda i, j: (0, 0)),
            pl.BlockSpec((1, tn), lambda i, j: (0, j)),
        ],
        out_specs=[
            pl.BlockSpec((tm, tn), lambda i, j: (i, j)),
            pl.BlockSpec((tm, LANES), lambda i, j: (i, 0)),
        ],
        out_shape=[
            jax.ShapeDtypeStruct((m, n), BF16),
            jax.ShapeDtypeStruct((m, LANES), F32),
        ],
        scratch_shapes=[pltpu.VMEM((tm, D_MODEL), BF16)],
        compiler_params=_params(("parallel", "arbitrary")),
        name="in_proj",
    )(x2d, norm_w, w_main, w_dt, col_scale)


def _lane_tile(x, width):
    return x if width == LANES else jnp.concatenate([x] * (width // LANES), axis=1)


def _attn_kernel(slope_ref, lamv_ref, subw_ref, q_ref, k_ref, v_ref, o_ref,
                 kt_scr, m_scr, l_scr, acc_scr, *, tq, tk, rows, lambda_init):
    h = pl.program_id(1)
    qi = pl.program_id(2)
    n_kblocks = kt_scr.shape[0]
    slope2 = slope_ref[h]
    kbias = slope2 * lax.broadcasted_iota(jnp.int32, (1, tk), 1).astype(F32)
    col_minus_row = (lax.broadcasted_iota(jnp.int32, (rows, tk), 1)
                     - lax.broadcasted_iota(jnp.int32, (rows, tk), 0))

    @pl.when(qi == 0)
    def _():
        def transpose_block(jb, carry):
            kt_scr[jb] = k_ref[pl.ds(pl.multiple_of(jb * tk, tk), tk), :].T
            return carry
        lax.fori_loop(0, n_kblocks, transpose_block, 0)

    m_scr[...] = jnp.full(m_scr.shape, NEG_INF, F32)
    l_scr[...] = jnp.zeros(l_scr.shape, F32)
    acc_scr[...] = jnp.zeros(acc_scr.shape, F32)

    def chain(r, c, kt, vblk, off, mask_thr):
        rs = slice(r * rows, (r + 1) * rows)
        s = jnp.dot(q_ref[rs, c * ATT_QK_DIM:(c + 1) * ATT_QK_DIM], kt,
                    preferred_element_type=F32) + kbias
        if mask_thr is not None:
            s = jnp.where(col_minus_row <= mask_thr, s, NEG_INF)
        m_old = m_scr[c, rs, :]
        m_new = jnp.maximum(m_old, jnp.max(s, axis=1, keepdims=True) + off)
        alpha = jnp.exp2(m_old - m_new)
        p = jnp.exp2(s - _lane_tile(m_new - off, tk))
        l_scr[c, rs, :] = alpha * l_scr[c, rs, :] + jnp.sum(p, axis=1, keepdims=True)
        acc_scr[c, rs, :] = (_lane_tile(alpha, ATT_V_DIM) * acc_scr[c, rs, :]
                             + jnp.dot(p.astype(BF16), vblk, preferred_element_type=F32))
        m_scr[c, rs, :] = m_new

    def block(j, thresholds):
        k0 = pl.multiple_of(j * tk, tk)
        kt_blk = kt_scr[j]
        vblk = v_ref[pl.ds(k0, tk), :]
        off = slope2 * k0.astype(F32)
        for r, thr in enumerate(thresholds):
            if thr == "skip":
                continue
            for c in range(2):
                chain(r, c, kt_blk[c * ATT_QK_DIM:(c + 1) * ATT_QK_DIM, :], vblk, off, thr)

    n_chunks = tq // rows

    def body(j, carry):
        block(j, [None] * n_chunks)
        return carry

    n_full = (qi * tq) // tk
    lax.fori_loop(0, n_full, body, 0)
    for d in range(tq // tk):
        thresholds = []
        for r in range(n_chunks):
            if d * tk >= (r + 1) * rows:
                thresholds.append("skip")
            elif (d + 1) * tk <= r * rows:
                thresholds.append(None)
            else:
                thresholds.append(r * rows - d * tk)
        block(n_full + d, thresholds)

    lv = lamv_ref[...]
    lam = (jnp.exp(jnp.sum(lv[0:1] * lv[1:2], axis=1, keepdims=True))
           - jnp.exp(jnp.sum(lv[2:3] * lv[3:4], axis=1, keepdims=True)) + lambda_init)
    o = (acc_scr[0] / _lane_tile(l_scr[0], ATT_V_DIM)
         - lam * (acc_scr[1] / _lane_tile(l_scr[1], ATT_V_DIM)))
    ms = jnp.mean(o * o, axis=-1, keepdims=True)
    o = (o * lax.rsqrt(ms + SUB_EPS)) * subw_ref[...]
    o_ref[...] = (o * (1.0 - lambda_init)).astype(BF16)


def _attention(proj, slopes2, lamv, subw, batch, seq, lambda_init, tq=512, tk=256, rows=128):
    assert tq % tk == 0 and tq % rows == 0 and seq % tq == 0
    nq = seq // tq
    kern = functools.partial(_attn_kernel, tq=tq, tk=tk, rows=rows, lambda_init=lambda_init)
    kb, vb = COL_K // ATT_V_DIM, COL_V // ATT_V_DIM
    return pl.pallas_call(
        kern,
        grid=(batch, ATT_HEADS, nq),
        in_specs=[
            pl.BlockSpec(memory_space=pltpu.SMEM),
            pl.BlockSpec((4, ATT_QK_DIM), lambda b, h, i: (0, 0)),
            pl.BlockSpec((1, ATT_V_DIM), lambda b, h, i: (0, 0)),
            pl.BlockSpec((tq, ATT_V_DIM), lambda b, h, i: (b * nq + i, h)),
            pl.BlockSpec((seq, ATT_V_DIM), lambda b, h, i: (b, kb + h)),
            pl.BlockSpec((seq, ATT_V_DIM), lambda b, h, i: (b, vb + h)),
        ],
        out_specs=pl.BlockSpec((tq, ATT_V_DIM), lambda b, h, i: (b * nq + i, h)),
        out_shape=jax.ShapeDtypeStruct((batch * seq, MIX_WIDTH), BF16),
        scratch_shapes=[
            pltpu.VMEM((seq // tk, ATT_V_DIM, tk), BF16),
            pltpu.VMEM((2, tq, LANES), F32),
            pltpu.VMEM((2, tq, LANES), F32),
            pltpu.VMEM((2, tq, ATT_V_DIM), F32),
        ],
        compiler_params=_params(("parallel", "parallel", "arbitrary")),
        name="diff_attention",
    )(slopes2, lamv, subw, proj, proj, proj)


def _ssd_kernel(z_ref, xs_ref, bc_ref, dt_ref, cwx_ref, cwbc_ref, cbx_ref, cbbc_ref,
                dtb_ref, alog_ref, dskip_ref, nw_ref, mix_ref, out_ref,
                tailx, tailbc, state, yscr):
    del mix_ref
    L = SSM_CHUNK

    @pl.when(pl.program_id(1) == 0)
    def _():
        tailx[...] = jnp.zeros(tailx.shape, F32)
        tailbc[...] = jnp.zeros(tailbc.shape, F32)
        state[...] = jnp.zeros(state.shape, F32)

    def conv_silu(u_ref, tail, w_ref, b_ref):
        u = u_ref[...].astype(F32)
        full = jnp.concatenate([tail[...], u], axis=0)
        acc = b_ref[...] + w_ref[3:4, :] * u
        for k in range(SSM_CONV - 1):
            lo = 8 - (SSM_CONV - 1) + k
            acc = acc + w_ref[k:k + 1, :] * full[lo:lo + L]
        tail[...] = u[L - 8:L]
        return acc * _sigmoid(acc)

    xs = conv_silu(xs_ref, tailx, cwx_ref, cbx_ref)
    bc = conv_silu(bc_ref, tailbc, cwbc_ref, cbbc_ref)
    xs_b = xs.astype(BF16)
    bc_b = bc.astype(BF16)

    dtr = dt_ref[...] + dtb_ref[...]
    dt = jnp.maximum(dtr, 0.0) + jnp.log1p(jnp.exp(-jnp.abs(dtr)))
    acs = dt * (-jnp.exp(alog_ref[...]))
    rowi = lax.broadcasted_iota(jnp.int32, (L, LANES), 0)
    shift = 1
    while shift < L:
        acs = acs + jnp.where(rowi >= shift, pltpu.roll(acs, shift, axis=0), 0.0)
        shift *= 2
    acs_t = acs.T
    dt_t = dt.T
    exp_a = jnp.exp(acs)
    w_t = dt_t * jnp.exp(acs_t[:, L - 1:L] - acs_t)
    cd_row = jnp.exp(acs[L - 1:L, :])

    li = lax.broadcasted_iota(jnp.int32, (L, L), 0)
    si = lax.broadcasted_iota(jnp.int32, (L, L), 1)
    causal = si <= li
    lane = lax.broadcasted_iota(jnp.int32, (L, PAIR_WIDTH), 1)
    lo_half = lane < SSM_HEAD_DIM
    lane_row = lax.broadcasted_iota(jnp.int32, (1, PAIR_WIDTH), 1) < SSM_HEAD_DIM
    zero_b = jnp.zeros((L, PAIR_WIDTH), BF16)

    gn = SSM_GROUPS * SSM_STATE
    for g in range(SSM_GROUPS):
        b_g = bc[:, g * SSM_STATE:(g + 1) * SSM_STATE]
        c_g = bc[:, gn + g * SSM_STATE:gn + (g + 1) * SSM_STATE]
        cb = lax.dot_general(bc_b[:, gn + g * SSM_STATE:gn + (g + 1) * SSM_STATE],
                             bc_b[:, g * SSM_STATE:(g + 1) * SSM_STATE],
                             (((1,), (1,)), ((), ())), preferred_element_type=F32)
        b_gt = b_g.T
        for pr in range(HEADS_PER_GROUP // 2):
            pair = g * (HEADS_PER_GROUP // 2) + pr
            lhs_y, lhs_s = [], []
            for hh in (2 * pair, 2 * pair + 1):
                seg = acs[:, hh:hh + 1] - acs_t[hh:hh + 1, :]
                decay = jnp.exp(jnp.where(causal, seg, NEG_INF))
                lhs_y.append((cb * decay * dt_t[hh:hh + 1, :]).astype(BF16))
                lhs_s.append((b_gt * w_t[hh:hh + 1, :]).astype(BF16))
            for hh in (2 * pair, 2 * pair + 1):
                lhs_y.append((c_g * exp_a[:, hh:hh + 1]).astype(BF16))
            x_p = xs_b[:, pair * PAIR_WIDTH:(pair + 1) * PAIR_WIDTH]
            x_lo = jnp.where(lo_half, x_p, zero_b)
            x_hi = jnp.where(lo_half, zero_b, x_p)
            st = state[pair]
            st_b = st.astype(BF16)
            st_lo = jnp.where(lo_half, st_b, zero_b)
            st_hi = jnp.where(lo_half, zero_b, st_b)
            y = jnp.dot(jnp.concatenate(lhs_y, axis=1),
                        jnp.concatenate([x_lo, x_hi, st_lo, st_hi], axis=0),
                        preferred_element_type=F32)
            yscr[:, pair * PAIR_WIDTH:(pair + 1) * PAIR_WIDTH] = y
            new = jnp.dot(jnp.concatenate(lhs_s, axis=1),
                          jnp.concatenate([x_lo, x_hi], axis=0),
                          preferred_element_type=F32)
            cd = jnp.where(lane_row, cd_row[:, 2 * pair:2 * pair + 1],
                           cd_row[:, 2 * pair + 1:2 * pair + 2])
            state[pair] = st * cd + new

    y = yscr[...] + dskip_ref[...] * xs
    z = z_ref[...].astype(F32)
    y = y * (z * _sigmoid(z))
    gw = SSM_WIDTH // SSM_GROUPS
    for g in range(SSM_GROUPS):
        yg = y[:, g * gw:(g + 1) * gw]
        ms = jnp.mean(yg * yg, axis=-1, keepdims=True)
        out_ref[:, g * gw:(g + 1) * gw] = (
            (yg * lax.rsqrt(ms + SUB_EPS)) * nw_ref[:, g * gw:(g + 1) * gw]).astype(BF16)


def _ssd(proj, dt_raw, conv_w, conv_b, dt_bias, a_log, d_skip_vec, norm_w, mix, batch, seq):
    nc = seq // SSM_CHUNK
    L = SSM_CHUNK
    row = lambda b, c: b * nc + c
    full = lambda shape: pl.BlockSpec(shape, lambda b, c: (0, 0))
    cwx, cwbc = conv_w[:, :SSM_WIDTH], conv_w[:, SSM_WIDTH:]
    cbx, cbbc = conv_b[:, :SSM_WIDTH], conv_b[:, SSM_WIDTH:]
    return pl.pallas_call(
        _ssd_kernel,
        grid=(batch, nc),
        in_specs=[
            pl.BlockSpec((L, SSM_WIDTH), lambda b, c: (row(b, c), COL_Z // SSM_WIDTH)),
            pl.BlockSpec((L, SSM_WIDTH), lambda b, c: (row(b, c), COL_XS // SSM_WIDTH)),
            pl.BlockSpec((L, SSM_BC_WIDTH), lambda b, c: (row(b, c), COL_BC // SSM_BC_WIDTH)),
            pl.BlockSpec((L, LANES), lambda b, c: (row(b, c), 0)),
            full((SSM_CONV, SSM_WIDTH)),
            full((SSM_CONV, SSM_BC_WIDTH)),
            full((1, SSM_WIDTH)),
            full((1, SSM_BC_WIDTH)),
            full((1, LANES)),
            full((1, LANES)),
            full((1, SSM_WIDTH)),
            full((1, SSM_WIDTH)),
            pl.BlockSpec(memory_space=pl.ANY),
        ],
        out_specs=pl.BlockSpec((L, SSM_WIDTH), lambda b, c: (row(b, c), ATT_WIDTH // SSM_WIDTH)),
        out_shape=jax.ShapeDtypeStruct(mix.shape, mix.dtype),
        scratch_shapes=[
            pltpu.VMEM((8, SSM_WIDTH), F32),
            pltpu.VMEM((8, SSM_BC_WIDTH), F32),
            pltpu.VMEM((SSM_HEADS // 2, SSM_STATE, PAIR_WIDTH), F32),
            pltpu.VMEM((L, SSM_WIDTH), F32),
        ],
        input_output_aliases={12: 0},
        compiler_params=_params(("parallel", "arbitrary")),
        name="ssd_mixer",
    )(proj, proj, proj, dt_raw, cwx, cwbc, cbx, cbbc, dt_bias, a_log, d_skip_vec, norm_w, mix)


def _out_proj_kernel(a_ref, w_ref, x_ref, nw_ref, x1_ref, h_ref, acc_ref):
    k = pl.program_id(1)

    @pl.when(k == 0)
    def _():
        acc_ref[...] = x_ref[...]

    acc_ref[...] += jnp.dot(a_ref[...], w_ref[...], preferred_element_type=F32)

    @pl.when(k == pl.num_programs(1) - 1)
    def _():
        x1 = acc_ref[...]
        x1_ref[...] = x1
        ms = jnp.mean(x1 * x1, axis=-1, keepdims=True)
        h_ref[...] = ((x1 * lax.rsqrt(ms + RMS_EPS)) * nw_ref[...]).astype(BF16)


def _out_proj(mix, w_out, x2d, norm_w, tm=512, tk=1024):
    m, kdim = mix.shape
    return pl.pallas_call(
        _out_proj_kernel,
        grid=(m // tm, kdim // tk),
        in_specs=[
            pl.BlockSpec((tm, tk), lambda i, k: (i, k)),
            pl.BlockSpec((tk, D_MODEL), lambda i, k: (k, 0)),
            pl.BlockSpec((tm, D_MODEL), lambda i, k: (i, 0)),
            pl.BlockSpec((1, D_MODEL), lambda i, k: (0, 0)),
        ],
        out_specs=[
            pl.BlockSpec((tm, D_MODEL), lambda i, k: (i, 0)),
            pl.BlockSpec((tm, D_MODEL), lambda i, k: (i, 0)),
        ],
        out_shape=[
            jax.ShapeDtypeStruct((m, D_MODEL), F32),
            jax.ShapeDtypeStruct((m, D_MODEL), BF16),
        ],
        scratch_shapes=[pltpu.VMEM((tm, D_MODEL), F32)],
        compiler_params=_params(("parallel", "arbitrary")),
        name="out_proj",
    )(mix, w_out, x2d, norm_w)


def _gate_up_kernel(h_ref, wg_ref, wu_ref, a_ref):
    h = h_ref[...]
    g = jnp.dot(h, wg_ref[...], preferred_element_type=F32)
    u = jnp.dot(h, wu_ref[...], preferred_element_type=F32)
    a_ref[...] = ((g * _sigmoid(g)) * u).astype(BF16)


def _gate_up(h, w_gate, w_up, tm=1024, tn=512):
    m = h.shape[0]
    n = w_gate.shape[1]
    return pl.pallas_call(
        _gate_up_kernel,
        grid=(m // tm, n // tn),
        in_specs=[
            pl.BlockSpec((tm, D_MODEL), lambda i, j: (i, 0)),
            pl.BlockSpec((D_MODEL, tn), lambda i, j: (0, j)),
            pl.BlockSpec((D_MODEL, tn), lambda i, j: (0, j)),
        ],
        out_specs=pl.BlockSpec((tm, tn), lambda i, j: (i, j)),
        out_shape=jax.ShapeDtypeStruct((m, n), BF16),
        compiler_params=_params(("parallel", "arbitrary")),
        name="gate_up",
    )(h, w_gate, w_up)


def _down_kernel(a_ref, w_ref, x_ref, nw_ref, o_ref, acc_ref):
    k = pl.program_id(1)

    @pl.when(k == 0)
    def _():
        acc_ref[...] = x_ref[...]

    acc_ref[...] += jnp.dot(a_ref[...], w_ref[...], preferred_element_type=F32)

    @pl.when(k == pl.num_programs(1) - 1)
    def _():
        x2 = acc_ref[...]
        ms = jnp.mean(x2 * x2, axis=-1, keepdims=True)
        o_ref[...] = (x2 * lax.rsqrt(ms + RMS_EPS)) * nw_ref[...]


def _down(a, w_down, x1, norm_w, tm=512, tk=1408):
    m, kdim = a.shape
    return pl.pallas_call(
        _down_kernel,
        grid=(m // tm, kdim // tk),
        in_specs=[
            pl.BlockSpec((tm, tk), lambda i, k: (i, k)),
            pl.BlockSpec((tk, D_MODEL), lambda i, k: (k, 0)),
            pl.BlockSpec((tm, D_MODEL), lambda i, k: (i, 0)),
            pl.BlockSpec((1, D_MODEL), lambda i, k: (0, 0)),
        ],
        out_specs=pl.BlockSpec((tm, D_MODEL), lambda i, k: (i, 0)),
        out_shape=jax.ShapeDtypeStruct((m, D_MODEL), F32),
        scratch_shapes=[pltpu.VMEM((tm, D_MODEL), F32)],
        compiler_params=_params(("parallel", "arbitrary")),
        name="down_proj",
    )(a, w_down, x1, norm_w)


def _alibi_slopes_log2(n):
    start = 2.0 ** (-8.0 / n)
    return jnp.asarray([start ** (i + 1) for i in range(n)], dtype=F32) * LOG2E


def _pad_lanes(v):
    return jnp.pad(v.astype(F32), (0, LANES - v.shape[0])).reshape(1, LANES)


def kernel(x, norm_mix_w, w_in, lambda_q1, lambda_k1, lambda_q2, lambda_k2, subln_w, conv_w, conv_b, dt_bias, a_log, d_skip, ssm_norm_w, w_out, norm_ffn_w, w_gate, w_up, w_down, norm_final_w):
    batch, seq, _ = x.shape
    assert w_in.shape[0] == 1, "single-layer block"
    layer = 0
    x2d = x.reshape(batch * seq, D_MODEL)
    col_scale = jnp.concatenate([
        jnp.full((ATT_WIDTH,), ATT_QK_DIM ** -0.5 * LOG2E, F32),
        jnp.ones((PROJ_MAIN - ATT_WIDTH,), F32)]).reshape(1, PROJ_MAIN)
    slopes2 = _alibi_slopes_log2(ATT_HEADS)
    lambda_init = 0.8 - 0.6 * math.exp(-0.3 * layer)
    w_main = w_in[layer, :, :PROJ_MAIN].astype(BF16)
    w_dt = jnp.pad(w_in[layer, :, PROJ_MAIN:], ((0, 0), (0, LANES - SSM_HEADS))).astype(BF16)
    proj, dt_raw = _in_proj(x2d, norm_mix_w[layer].reshape(1, D_MODEL), w_main, w_dt, col_scale)
    lamv = jnp.stack([lambda_q1[layer], lambda_k1[layer], lambda_q2[layer], lambda_k2[layer]]).astype(F32)
    mix = _attention(proj, slopes2, lamv, subln_w[layer].reshape(1, ATT_V_DIM).astype(F32),
                     batch, seq, lambda_init)
    mix = _ssd(proj, dt_raw, conv_w[layer], conv_b[layer].reshape(1, -1),
               _pad_lanes(dt_bias[layer]), _pad_lanes(a_log[layer]),
               jnp.repeat(d_skip[layer].astype(F32), SSM_HEAD_DIM).reshape(1, SSM_WIDTH),
               ssm_norm_w[layer].reshape(1, SSM_WIDTH), mix, batch, seq)
    x1, h = _out_proj(mix, w_out[layer].astype(BF16), x2d, norm_ffn_w[layer].reshape(1, D_MODEL))
    a = _gate_up(h, w_gate[layer].astype(BF16), w_up[layer].astype(BF16))
    out = _down(a, w_down[layer].astype(BF16), x1, norm_final_w.reshape(1, D_MODEL))
    return out.reshape(batch, seq, D_MODEL)
```

```python
import functools
import math

import jax
import jax.numpy as jnp
from jax import lax
from jax.experimental import pallas as pl
from jax.experimental.pallas import tpu as pltpu

F32 = jnp.float32
BF16 = jnp.bfloat16

D_MODEL = 2048
ATT_HEADS = 8
ATT_QK_DIM = 128
ATT_V_DIM = 2 * ATT_QK_DIM
ATT_WIDTH = ATT_HEADS * ATT_V_DIM
SSM_HEADS = 32
SSM_HEAD_DIM = 64
SSM_WIDTH = SSM_HEADS * SSM_HEAD_DIM
SSM_GROUPS = 4
SSM_STATE = 128
SSM_CONV = 4
SSM_CHUNK = 128
SSM_BC_WIDTH = 2 * SSM_GROUPS * SSM_STATE
PROJ_MAIN = 2 * ATT_WIDTH + ATT_WIDTH + SSM_WIDTH + SSM_WIDTH + SSM_BC_WIDTH
RMS_EPS = 1e-6
SUB_EPS = 1e-5
LOG2E = 1.4426950408889634
LANES = 128
SUBLANES = 8
HEADS_PER_GROUP = SSM_HEADS // SSM_GROUPS
PAIR_WIDTH = 2 * SSM_HEAD_DIM
NEG_INF = float("-inf")

COL_Q, COL_K, COL_V = 0, ATT_WIDTH, 2 * ATT_WIDTH
COL_Z = 3 * ATT_WIDTH
COL_XS = COL_Z + SSM_WIDTH
COL_BC = COL_XS + SSM_WIDTH

VMEM_LIMIT = 56 * 1024 * 1024


def _params(semantics):
    return pltpu.CompilerParams(dimension_semantics=semantics, vmem_limit_bytes=VMEM_LIMIT)


def _sigmoid(x):
    return 1.0 / (1.0 + jnp.exp(-x))


def _rmsnorm(x, w, eps):
    ms = jnp.mean(x * x, axis=-1, keepdims=True)
    return (x * lax.rsqrt(ms + eps)) * w


def _lane_tile(x, width):
    return x if width == LANES else jnp.concatenate([x] * (width // LANES), axis=1)


def _in_proj_kernel(x_ref, nw_ref, w_ref, wdt_ref, cs_ref, proj_ref, dt_ref, h_ref):
    @pl.when(pl.program_id(1) == 0)
    def _():
        hb = _rmsnorm(x_ref[...], nw_ref[...], RMS_EPS).astype(BF16)
        h_ref[...] = hb
        dt_ref[...] = jnp.dot(hb, wdt_ref[...], preferred_element_type=F32)

    acc = jnp.dot(h_ref[...], w_ref[...], preferred_element_type=F32)
    proj_ref[...] = (acc * cs_ref[...]).astype(BF16)


def _in_proj(x2d, norm_w, w_in, w_dt, col_scale, tm=1024, tn=1024):
    m = x2d.shape[0]
    return pl.pallas_call(
        _in_proj_kernel,
        grid=(m // tm, PROJ_MAIN // tn),
        in_specs=[
            pl.BlockSpec((tm, D_MODEL), lambda i, j: (i, 0)),
            pl.BlockSpec((1, D_MODEL), lambda i, j: (0, 0)),
            pl.BlockSpec((D_MODEL, tn), lambda i, j: (0, j)),
            pl.BlockSpec((D_MODEL, LANES), lambda i, j: (0, 0)),
            pl.BlockSpec((1, tn), lambda i, j: (0, j)),
        ],
        out_specs=[
            pl.BlockSpec((tm, tn), lambda i, j: (i, j)),
            pl.BlockSpec((tm, LANES), lambda i, j: (i, 0)),
        ],
        out_shape=[
            jax.ShapeDtypeStruct((m, PROJ_MAIN), BF16),
            jax.ShapeDtypeStruct((m, LANES), F32),
        ],
        scratch_shapes=[pltpu.VMEM((tm, D_MODEL), BF16)],
        compiler_params=_params(("parallel", "arbitrary")),
        name="in_proj",
    )(x2d, norm_w, w_in, w_dt, col_scale)


def _attn_kernel(slope_ref, lamv_ref, subw_ref, q_ref, k_ref, v_ref, o_ref,
                 vt_scr, qt_scr, kb_scr, m_scr, l_scr, acc_scr, *, tq, tk, qc, lambda_init):
    h = pl.program_id(1)
    qi = pl.program_id(2)
    n_kblocks = vt_scr.shape[0]
    slope2 = slope_ref[h]

    @pl.when(qi == 0)
    def _():
        def transpose_block(jb, carry):
            vt_scr[jb] = v_ref[pl.ds(pl.multiple_of(jb * tk, tk), tk), :].T
            return carry
        lax.fori_loop(0, n_kblocks, transpose_block, 0)
        kb_scr[...] = slope2 * lax.broadcasted_iota(jnp.int32, (tk, LANES), 0).astype(F32)

    qt_scr[...] = q_ref[...].T
    m_scr[...] = jnp.full(m_scr.shape, NEG_INF, F32)
    l_scr[...] = jnp.zeros(l_scr.shape, F32)
    acc_scr[...] = jnp.zeros(acc_scr.shape, F32)
    row_minus_col = (lax.broadcasted_iota(jnp.int32, (tk, qc), 0)
                     - lax.broadcasted_iota(jnp.int32, (tk, qc), 1))

    def chain(j, c, u, thr):
        k0 = j * tk if isinstance(j, int) else pl.multiple_of(j * tk, tk)
        cs = slice(u * qc, (u + 1) * qc)
        dims = slice(c * ATT_QK_DIM, (c + 1) * ATT_QK_DIM)
        st = jnp.dot(k_ref[pl.ds(k0, tk), dims], qt_scr[dims, cs],
                     preferred_element_type=F32)
        st = st + _lane_tile(kb_scr[...], qc)
        if thr is not None:
            st = jnp.where(row_minus_col <= thr, st, NEG_INF)
        off = slope2 * jnp.asarray(j * tk, F32)
        m_old = m_scr[c, 0:1, cs]
        m_new = jnp.maximum(m_old, jnp.max(st, axis=0, keepdims=True) + off)
        alpha = jnp.exp2(m_old - m_new)
        pt = jnp.exp2(st - (m_new - off))
        l_new = alpha * l_scr[c, 0:1, cs] + jnp.sum(pt, axis=0, keepdims=True)
        l_scr[c, :, cs] = jnp.broadcast_to(l_new, (SUBLANES, qc))
        m_scr[c, :, cs] = jnp.broadcast_to(m_new, (SUBLANES, qc))
        acc_scr[c, :, cs] = alpha * acc_scr[c, :, cs] + jnp.dot(
            vt_scr[j], pt.astype(BF16), preferred_element_type=F32)

    n_qchunks = tq // qc

    def full_block(j):
        for u in range(n_qchunks):
            for c in range(2):
                chain(j, c, u, None)

    def body(jj, carry):
        full_block(2 * jj)
        full_block(2 * jj + 1)
        return carry

    n_full = (qi * tq) // tk
    lax.fori_loop(0, n_full // 2, body, 0)
    for d in range(tq // tk):
        for u in range(n_qchunks):
            if d * tk >= (u + 1) * qc:
                continue
            thr = None if (d + 1) * tk <= u * qc else u * qc - d * tk
            for c in range(2):
                chain(n_full + d, c, u, thr)

    lv = lamv_ref[...]
    lam = (jnp.exp(jnp.sum(lv[0:1] * lv[1:2], axis=1, keepdims=True))
           - jnp.exp(jnp.sum(lv[2:3] * lv[3:4], axis=1, keepdims=True)) + lambda_init)
    ot = acc_scr[0] / l_scr[0, 0:1, :] - lam * (acc_scr[1] / l_scr[1, 0:1, :])
    ms = jnp.mean(ot * ot, axis=0, keepdims=True)
    ot = (ot * lax.rsqrt(ms + SUB_EPS)) * _lane_tile(subw_ref[...], tq)
    o_ref[...] = (ot * (1.0 - lambda_init)).T.astype(BF16)


def _attention(proj, slopes2, lamv, subw_col, batch, seq, lambda_init, tq=512, tk=256, qc=256):
    assert tq % (2 * tk) == 0 and tq % qc == 0 and seq % tq == 0
    nq = seq // tq
    kern = functools.partial(_attn_kernel, tq=tq, tk=tk, qc=qc, lambda_init=lambda_init)
    kb, vb = COL_K // ATT_V_DIM, COL_V // ATT_V_DIM
    return pl.pallas_call(
        kern,
        grid=(batch, ATT_HEADS, nq),
        in_specs=[
            pl.BlockSpec(memory_space=pltpu.SMEM),
            pl.BlockSpec((4, ATT_QK_DIM), lambda b, h, i: (0, 0)),
            pl.BlockSpec((ATT_V_DIM, LANES), lambda b, h, i: (0, 0)),
            pl.BlockSpec((tq, ATT_V_DIM), lambda b, h, i: (b * nq + i, h)),
            pl.BlockSpec((seq, ATT_V_DIM), lambda b, h, i: (b, kb + h)),
            pl.BlockSpec((seq, ATT_V_DIM), lambda b, h, i: (b, vb + h)),
        ],
        out_specs=pl.BlockSpec((tq, ATT_V_DIM), lambda b, h, i: (b * nq + i, h)),
        out_shape=jax.ShapeDtypeStruct((batch * seq, ATT_WIDTH), BF16),
        scratch_shapes=[
            pltpu.VMEM((seq // tk, ATT_V_DIM, tk), BF16),
            pltpu.VMEM((ATT_V_DIM, tq), BF16),
            pltpu.VMEM((tk, LANES), F32),
            pltpu.VMEM((2, SUBLANES, tq), F32),
            pltpu.VMEM((2, SUBLANES, tq), F32),
            pltpu.VMEM((2, ATT_V_DIM, tq), F32),
        ],
        compiler_params=_params(("parallel", "parallel", "arbitrary")),
        name="diff_attention",
    )(slopes2, lamv, subw_col, proj, proj, proj)


def _ssd_kernel(z_ref, xs_ref, bc_ref, dt_ref, cwx_ref, cwbc_ref, cbx_ref, cbbc_ref,
                dtb_ref, alog_ref, dskip_ref, nw_ref, out_ref,
                tailx, tailbc, state, yscr):
    L = SSM_CHUNK

    @pl.when(pl.program_id(1) == 0)
    def _():
        tailx[...] = jnp.zeros(tailx.shape, F32)
        tailbc[...] = jnp.zeros(tailbc.shape, F32)
        state[...] = jnp.zeros(state.shape, F32)

    def conv_silu(u_ref, tail, w_ref, b_ref):
        u = u_ref[...].astype(F32)
        full = jnp.concatenate([tail[...], u], axis=0)
        acc = b_ref[...] + w_ref[3:4, :] * u
        for k in range(SSM_CONV - 1):
            lo = 8 - (SSM_CONV - 1) + k
            acc = acc + w_ref[k:k + 1, :] * full[lo:lo + L]
        tail[...] = u[L - 8:L]
        return acc * _sigmoid(acc)

    xs = conv_silu(xs_ref, tailx, cwx_ref, cbx_ref)
    bc = conv_silu(bc_ref, tailbc, cwbc_ref, cbbc_ref)
    xs_b = xs.astype(BF16)
    bc_b = bc.astype(BF16)

    dtr = dt_ref[...] + dtb_ref[...]
    dt = jnp.maximum(dtr, 0.0) + jnp.log1p(jnp.exp(-jnp.abs(dtr)))
    acs = dt * (-jnp.exp(alog_ref[...]))
    rowi = lax.broadcasted_iota(jnp.int32, (L, LANES), 0)
    shift = 1
    while shift < L:
        acs = acs + jnp.where(rowi >= shift, pltpu.roll(acs, shift, axis=0), 0.0)
        shift *= 2
    acs_t = acs.T
    dt_t = dt.T
    exp_a = jnp.exp(acs)
    w_t = dt_t * jnp.exp(acs_t[:, L - 1:L] - acs_t)
    cd_row = jnp.exp(acs[L - 1:L, :])

    li = lax.broadcasted_iota(jnp.int32, (L, L), 0)
    si = lax.broadcasted_iota(jnp.int32, (L, L), 1)
    causal = si <= li
    lane = lax.broadcasted_iota(jnp.int32, (L, PAIR_WIDTH), 1)
    lo_half = lane < SSM_HEAD_DIM
    lane_row = lax.broadcasted_iota(jnp.int32, (1, PAIR_WIDTH), 1) < SSM_HEAD_DIM
    zero_b = jnp.zeros((L, PAIR_WIDTH), BF16)

    gn = SSM_GROUPS * SSM_STATE
    for g in range(SSM_GROUPS):
        b_g = bc[:, g * SSM_STATE:(g + 1) * SSM_STATE]
        c_g = bc[:, gn + g * SSM_STATE:gn + (g + 1) * SSM_STATE]
        cb = lax.dot_general(bc_b[:, gn + g * SSM_STATE:gn + (g + 1) * SSM_STATE],
                             bc_b[:, g * SSM_STATE:(g + 1) * SSM_STATE],
                             (((1,), (1,)), ((), ())), preferred_element_type=F32)
        b_gt = b_g.T
        for pr in range(HEADS_PER_GROUP // 2):
            pair = g * (HEADS_PER_GROUP // 2) + pr
            lhs_y, lhs_s = [], []
            for hh in (2 * pair, 2 * pair + 1):
                seg = acs[:, hh:hh + 1] - acs_t[hh:hh + 1, :]
                decay = jnp.exp(jnp.where(causal, seg, NEG_INF))
                lhs_y.append((cb * decay * dt_t[hh:hh + 1, :]).astype(BF16))
                lhs_s.append((b_gt * w_t[hh:hh + 1, :]).astype(BF16))
            for hh in (2 * pair, 2 * pair + 1):
                lhs_y.append((c_g * exp_a[:, hh:hh + 1]).astype(BF16))
            x_p = xs_b[:, pair * PAIR_WIDTH:(pair + 1) * PAIR_WIDTH]
            x_lo = jnp.where(lo_half, x_p, zero_b)
            x_hi = jnp.where(lo_half, zero_b, x_p)
            st = state[pair]
            st_b = st.astype(BF16)
            st_lo = jnp.where(lo_half, st_b, zero_b)
            st_hi = jnp.where(lo_half, zero_b, st_b)
            y = jnp.dot(jnp.concatenate(lhs_y, axis=1),
                        jnp.concatenate([x_lo, x_hi, st_lo, st_hi], axis=0),
                        preferred_element_type=F32)
            yscr[:, pair * PAIR_WIDTH:(pair + 1) * PAIR_WIDTH] = y
            new = jnp.dot(jnp.concatenate(lhs_s, axis=1),
                          jnp.concatenate([x_lo, x_hi], axis=0),
                          preferred_element_type=F32)
            cd = jnp.where(lane_row, cd_row[:, 2 * pair:2 * pair + 1],
                           cd_row[:, 2 * pair + 1:2 * pair + 2])
            state[pair] = st * cd + new

    y = yscr[...] + dskip_ref[...] * xs
    z = z_ref[...].astype(F32)
    y = y * (z * _sigmoid(z))
    gw = SSM_WIDTH // SSM_GROUPS
    for g in range(SSM_GROUPS):
        cols = slice(g * gw, (g + 1) * gw)
        out_ref[:, cols] = _rmsnorm(y[:, cols], nw_ref[:, cols], SUB_EPS).astype(BF16)


def _ssd(proj, dt_raw, conv_w, conv_b, dt_bias, a_log, d_skip_vec, norm_w, batch, seq):
    nc = seq // SSM_CHUNK
    L = SSM_CHUNK
    row = lambda b, c: b * nc + c
    full = lambda shape: pl.BlockSpec(shape, lambda b, c: (0, 0))
    cwx, cwbc = conv_w[:, :SSM_WIDTH], conv_w[:, SSM_WIDTH:]
    cbx, cbbc = conv_b[:, :SSM_WIDTH], conv_b[:, SSM_WIDTH:]
    return pl.pallas_call(
        _ssd_kernel,
        grid=(batch, nc),
        in_specs=[
            pl.BlockSpec((L, SSM_WIDTH), lambda b, c: (row(b, c), COL_Z // SSM_WIDTH)),
            pl.BlockSpec((L, SSM_WIDTH), lambda b, c: (row(b, c), COL_XS // SSM_WIDTH)),
            pl.BlockSpec((L, SSM_BC_WIDTH), lambda b, c: (row(b, c), COL_BC // SSM_BC_WIDTH)),
            pl.BlockSpec((L, LANES), lambda b, c: (row(b, c), 0)),
            full((SSM_CONV, SSM_WIDTH)),
            full((SSM_CONV, SSM_BC_WIDTH)),
            full((1, SSM_WIDTH)),
            full((1, SSM_BC_WIDTH)),
            full((1, LANES)),
            full((1, LANES)),
            full((1, SSM_WIDTH)),
            full((1, SSM_WIDTH)),
        ],
        out_specs=pl.BlockSpec((L, SSM_WIDTH), lambda b, c: (row(b, c), 0)),
        out_shape=jax.ShapeDtypeStruct((batch * seq, SSM_WIDTH), BF16),
        scratch_shapes=[
            pltpu.VMEM((8, SSM_WIDTH), F32),
            pltpu.VMEM((8, SSM_BC_WIDTH), F32),
            pltpu.VMEM((SSM_HEADS // 2, SSM_STATE, PAIR_WIDTH), F32),
            pltpu.VMEM((L, SSM_WIDTH), F32),
        ],
        compiler_params=_params(("parallel", "arbitrary")),
        name="ssd_mixer",
    )(proj, proj, proj, dt_raw, cwx, cwbc, cbx, cbbc, dt_bias, a_log, d_skip_vec, norm_w)


def _out_proj_kernel(att_ref, ssm_ref, w_ref, x_ref, x1_ref):
    acc = jnp.dot(att_ref[...], w_ref[0], preferred_element_type=F32)
    acc = acc + jnp.dot(ssm_ref[...], w_ref[1], preferred_element_type=F32)
    x1_ref[...] = x_ref[...] + acc


def _out_proj(att, ssm, w_out2, x2d, tm=1024, tn=512):
    m = att.shape[0]
    return pl.pallas_call(
        _out_proj_kernel,
        grid=(m // tm, D_MODEL // tn),
        in_specs=[
            pl.BlockSpec((tm, ATT_WIDTH), lambda i, j: (i, 0)),
            pl.BlockSpec((tm, SSM_WIDTH), lambda i, j: (i, 0)),
            pl.BlockSpec((2, ATT_WIDTH, tn), lambda i, j: (0, 0, j)),
            pl.BlockSpec((tm, tn), lambda i, j: (i, j)),
        ],
        out_specs=pl.BlockSpec((tm, tn), lambda i, j: (i, j)),
        out_shape=jax.ShapeDtypeStruct((m, D_MODEL), F32),
        compiler_params=_params(("parallel", "arbitrary")),
        name="out_proj",
    )(att, ssm, w_out2, x2d)


def _gate_up_kernel(x_ref, nw_ref, wg_ref, wu_ref, a_ref, h_ref):
    @pl.when(pl.program_id(1) == 0)
    def _():
        h_ref[...] = _rmsnorm(x_ref[...], nw_ref[...], RMS_EPS).astype(BF16)

    h = h_ref[...]
    g = jnp.dot(h, wg_ref[...], preferred_element_type=F32)
    u = jnp.dot(h, wu_ref[...], preferred_element_type=F32)
    a_ref[...] = ((g * _sigmoid(g)) * u).astype(BF16)


def _gate_up(x1, norm_w, w_gate, w_up, tm=1024, tn=512):
    m = x1.shape[0]
    n = w_gate.shape[1]
    return pl.pallas_call(
        _gate_up_kernel,
        grid=(m // tm, n // tn),
        in_specs=[
            pl.BlockSpec((tm, D_MODEL), lambda i, j: (i, 0)),
            pl.BlockSpec((1, D_MODEL), lambda i, j: (0, 0)),
            pl.BlockSpec((D_MODEL, tn), lambda i, j: (0, j)),
            pl.BlockSpec((D_MODEL, tn), lambda i, j: (0, j)),
        ],
        out_specs=pl.BlockSpec((tm, tn), lambda i, j: (i, j)),
        out_shape=jax.ShapeDtypeStruct((m, n), BF16),
        scratch_shapes=[pltpu.VMEM((tm, D_MODEL), BF16)],
        compiler_params=_params(("parallel", "arbitrary")),
        name="gate_up",
    )(x1, norm_w, w_gate, w_up)


def _down_kernel(a_ref, w_ref, x_ref, nw_ref, o_ref, x2_ref):
    j = pl.program_id(1)
    x2_ref[j] = x_ref[...] + jnp.dot(a_ref[...], w_ref[...], preferred_element_type=F32)

    @pl.when(j == pl.num_programs(1) - 1)
    def _():
        x2 = jnp.concatenate([x2_ref[t] for t in range(x2_ref.shape[0])], axis=1)
        o_ref[...] = _rmsnorm(x2, nw_ref[...], RMS_EPS)


def _down(a, w_down, x1, norm_w, tm=512, tn=512):
    m, kdim = a.shape
    return pl.pallas_call(
        _down_kernel,
        grid=(m // tm, D_MODEL // tn),
        in_specs=[
            pl.BlockSpec((tm, kdim), lambda i, j: (i, 0)),
            pl.BlockSpec((kdim, tn), lambda i, j: (0, j)),
            pl.BlockSpec((tm, tn), lambda i, j: (i, j)),
            pl.BlockSpec((1, D_MODEL), lambda i, j: (0, 0)),
        ],
        out_specs=pl.BlockSpec((tm, D_MODEL), lambda i, j: (i, 0)),
        out_shape=jax.ShapeDtypeStruct((m, D_MODEL), F32),
        scratch_shapes=[pltpu.VMEM((D_MODEL // tn, tm, tn), F32)],
        compiler_params=_params(("parallel", "arbitrary")),
        name="down_proj",
    )(a, w_down, x1, norm_w)


def _alibi_slopes_log2(n):
    start = 2.0 ** (-8.0 / n)
    return jnp.asarray([start ** (i + 1) for i in range(n)], dtype=F32) * LOG2E


def _pad_lanes(v):
    return jnp.pad(v.astype(F32), (0, LANES - v.shape[0])).reshape(1, LANES)


def kernel(x, norm_mix_w, w_in, lambda_q1, lambda_k1, lambda_q2, lambda_k2, subln_w, conv_w, conv_b, dt_bias, a_log, d_skip, ssm_norm_w, w_out, norm_ffn_w, w_gate, w_up, w_down, norm_final_w):
    batch, seq, _ = x.shape
    assert w_in.shape[0] == 1, "single-layer block"
    layer = 0
    x2d = x.reshape(batch * seq, D_MODEL)
    col_scale = jnp.concatenate([
        jnp.full((ATT_WIDTH,), ATT_QK_DIM ** -0.5 * LOG2E, F32),
        jnp.ones((PROJ_MAIN - ATT_WIDTH,), F32)]).reshape(1, PROJ_MAIN)
    slopes2 = _alibi_slopes_log2(ATT_HEADS)
    lambda_init = 0.8 - 0.6 * math.exp(-0.3 * layer)
    w_in_b = w_in[layer].astype(BF16)
    w_dt = jnp.pad(w_in[layer, :, PROJ_MAIN:], ((0, 0), (0, LANES - SSM_HEADS))).astype(BF16)
    proj, dt_raw = _in_proj(x2d, norm_mix_w[layer].reshape(1, D_MODEL), w_in_b, w_dt, col_scale)
    lamv = jnp.stack([lambda_q1[layer], lambda_k1[layer], lambda_q2[layer], lambda_k2[layer]]).astype(F32)
    subw_col = jnp.broadcast_to(subln_w[layer].astype(F32).reshape(ATT_V_DIM, 1), (ATT_V_DIM, LANES))
    att = _attention(proj, slopes2, lamv, subw_col, batch, seq, lambda_init)
    ssm = _ssd(proj, dt_raw, conv_w[layer], conv_b[layer].reshape(1, -1),
               _pad_lanes(dt_bias[layer]), _pad_lanes(a_log[layer]),
               jnp.repeat(d_skip[layer].astype(F32), SSM_HEAD_DIM).reshape(1, SSM_WIDTH),
               ssm_norm_w[layer].reshape(1, SSM_WIDTH), batch, seq)
    w_out2 = w_out[layer].astype(BF16).reshape(2, ATT_WIDTH, D_MODEL)
    x1 = _out_proj(att, ssm, w_out2, x2d)
    a = _gate_up(x1, norm_ffn_w[layer].reshape(1, D_MODEL), w_gate[layer].astype(BF16),
                 w_up[layer].astype(BF16))
    out = _down(a, w_down[layer].astype(BF16), x1, norm_final_w.reshape(1, D_MODEL))
    return out.reshape(batch, seq, D_MODEL)
```

```python
import functools
import math

import jax
import jax.numpy as jnp
from jax import lax
from jax.experimental import pallas as pl
from jax.experimental.pallas import tpu as pltpu

F32 = jnp.float32
BF16 = jnp.bfloat16

D_MODEL = 2048
ATT_HEADS = 8
ATT_QK_DIM = 128
ATT_V_DIM = 2 * ATT_QK_DIM
ATT_WIDTH = ATT_HEADS * ATT_V_DIM
SSM_HEADS = 32
SSM_HEAD_DIM = 64
SSM_WIDTH = SSM_HEADS * SSM_HEAD_DIM
SSM_GROUPS = 4
SSM_STATE = 128
SSM_CONV = 4
SSM_CHUNK = 128
SSM_BC_WIDTH = 2 * SSM_GROUPS * SSM_STATE
PROJ_MAIN = 2 * ATT_WIDTH + ATT_WIDTH + SSM_WIDTH + SSM_WIDTH + SSM_BC_WIDTH
RMS_EPS = 1e-6
SUB_EPS = 1e-5
LOG2E = 1.4426950408889634
LANES = 128
SUBLANES = 8
ONES_ROWS = 16
N_EARLY_CHAINS = 4
HEADS_PER_GROUP = SSM_HEADS // SSM_GROUPS
PAIR_WIDTH = 2 * SSM_HEAD_DIM
NEG_INF = float("-inf")

COL_Q, COL_K, COL_V = 0, ATT_WIDTH, 2 * ATT_WIDTH
COL_Z = 3 * ATT_WIDTH
COL_XS = COL_Z + SSM_WIDTH
COL_BC = COL_XS + SSM_WIDTH

VMEM_LIMIT = 56 * 1024 * 1024


def _params(semantics):
    return pltpu.CompilerParams(dimension_semantics=semantics, vmem_limit_bytes=VMEM_LIMIT)


def _sigmoid(x):
    return 1.0 / (1.0 + jnp.exp(-x))


def _rmsnorm(x, w, eps):
    ms = jnp.mean(x * x, axis=-1, keepdims=True)
    return (x * lax.rsqrt(ms + eps)) * w


def _lane_tile(x, width):
    return x if width == LANES else jnp.concatenate([x] * (width // LANES), axis=1)


def _in_proj_kernel(x_ref, nw_ref, w_ref, wdt_ref, cs_ref, proj_ref, dt_ref, h_ref):
    @pl.when(pl.program_id(1) == 0)
    def _():
        hb = _rmsnorm(x_ref[...], nw_ref[...], RMS_EPS).astype(BF16)
        h_ref[...] = hb
        dt_ref[...] = jnp.dot(hb, wdt_ref[...], preferred_element_type=F32)

    acc = jnp.dot(h_ref[...], w_ref[...], preferred_element_type=F32)
    proj_ref[...] = (acc * cs_ref[...]).astype(BF16)


def _in_proj(x2d, norm_w, w_in, w_dt, col_scale, tm=1024, tn=1024):
    m = x2d.shape[0]
    return pl.pallas_call(
        _in_proj_kernel,
        grid=(m // tm, PROJ_MAIN // tn),
        in_specs=[
            pl.BlockSpec((tm, D_MODEL), lambda i, j: (i, 0)),
            pl.BlockSpec((1, D_MODEL), lambda i, j: (0, 0)),
            pl.BlockSpec((D_MODEL, tn), lambda i, j: (0, j)),
            pl.BlockSpec((D_MODEL, LANES), lambda i, j: (0, 0)),
            pl.BlockSpec((1, tn), lambda i, j: (0, j)),
        ],
        out_specs=[
            pl.BlockSpec((tm, tn), lambda i, j: (i, j)),
            pl.BlockSpec((tm, LANES), lambda i, j: (i, 0)),
        ],
        out_shape=[
            jax.ShapeDtypeStruct((m, PROJ_MAIN), BF16),
            jax.ShapeDtypeStruct((m, LANES), F32),
        ],
        scratch_shapes=[pltpu.VMEM((tm, D_MODEL), BF16)],
        compiler_params=_params(("parallel", "arbitrary")),
        name="in_proj",
    )(x2d, norm_w, w_in, w_dt, col_scale)


def _attn_kernel(slope_ref, lamv_ref, subw_ref, q_ref, k_ref, v_ref, o_ref,
                 vt_scr, qt_scr, kaug_scr, st_scr, m_scr, acc_scr, *, tq, tk, qc, lambda_init):
    h = pl.program_id(1)
    qi = pl.program_id(2)
    n_kblocks = vt_scr.shape[0]
    slope2 = slope_ref[h]
    aug_lane = lax.broadcasted_iota(jnp.int32, (tk, ATT_QK_DIM), 1)

    @pl.when(qi == 0)
    def _():
        def transpose_block(jb, carry):
            vt = v_ref[pl.ds(pl.multiple_of(jb * tk, tk), tk), :].T
            vt_scr[jb] = jnp.concatenate([vt, jnp.ones((ONES_ROWS, tk), BF16)], axis=0)
            return carry
        lax.fori_loop(0, n_kblocks, transpose_block, 0)
        bias = slope2 * lax.broadcasted_iota(jnp.int32, (tk, ATT_QK_DIM), 0).astype(F32)
        hi = bias.astype(BF16).astype(F32)
        mid = (bias - hi).astype(BF16).astype(F32)
        lo = (bias - hi) - mid
        kaug_scr[...] = jnp.where(aug_lane == 0, hi, jnp.where(
            aug_lane == 1, mid, jnp.where(aug_lane == 2, lo, 0.0))).astype(BF16)

    qt = q_ref[...].T
    pick = jnp.where(lax.broadcasted_iota(jnp.int32, (ATT_QK_DIM, tq), 0) < 3, 1.0, 0.0).astype(BF16)
    for c in range(2):
        qt_scr[c] = jnp.concatenate([qt[c * ATT_QK_DIM:(c + 1) * ATT_QK_DIM, :], pick], axis=0)
    m_scr[...] = jnp.full(m_scr.shape, NEG_INF, F32)
    acc_scr[...] = jnp.zeros(acc_scr.shape, F32)
    row_minus_col = (lax.broadcasted_iota(jnp.int32, (tk, qc), 0)
                     - lax.broadcasted_iota(jnp.int32, (tk, qc), 1))

    def scores(j, c, u):
        k0 = j * tk if isinstance(j, int) else pl.multiple_of(j * tk, tk)
        dims = slice(c * ATT_QK_DIM, (c + 1) * ATT_QK_DIM)
        k_aug = jnp.concatenate([k_ref[pl.ds(k0, tk), dims], kaug_scr[...]], axis=1)
        return jnp.dot(k_aug, qt_scr[c, :, u * qc:(u + 1) * qc], preferred_element_type=F32)

    early = [(u, c) for u in range(tq // qc) for c in range(2)][:N_EARLY_CHAINS]

    def prefetch_scores(j):
        for slot, (u, c) in enumerate(early):
            st_scr[slot] = scores(j, c, u)

    def chain(j, c, u, thr, prefetched=False):
        cs = slice(u * qc, (u + 1) * qc)
        st = st_scr[early.index((u, c))] if prefetched and (u, c) in early else scores(j, c, u)
        if thr is not None:
            st = jnp.where(row_minus_col <= thr, st, NEG_INF)
        off = slope2 * jnp.asarray(j * tk, F32)
        m_old = m_scr[c, 0:1, cs]
        m_new = jnp.maximum(m_old, jnp.max(st, axis=0, keepdims=True) + off)
        alpha = jnp.exp2(m_old - m_new)
        pt = jnp.exp2(st - (m_new - off))
        m_scr[c, :, cs] = jnp.broadcast_to(m_new, (SUBLANES, qc))
        acc_scr[c, :, cs] = alpha * acc_scr[c, :, cs] + jnp.dot(
            vt_scr[j], pt.astype(BF16), preferred_element_type=F32)

    n_qchunks = tq // qc

    def full_block(j, prefetched):
        for u in range(n_qchunks):
            for c in range(2):
                chain(j, c, u, None, prefetched)

    def body(jj, carry):
        full_block(2 * jj, True)
        full_block(2 * jj + 1, False)
        prefetch_scores(2 * jj + 2)
        return carry

    n_full = (qi * tq) // tk
    prefetch_scores(0)
    lax.fori_loop(0, n_full // 2, body, 0)
    for d in range(tq // tk):
        for u in range(n_qchunks):
            if d * tk >= (u + 1) * qc:
                continue
            thr = None if (d + 1) * tk <= u * qc else u * qc - d * tk
            for c in range(2):
                chain(n_full + d, c, u, thr, prefetched=(d == 0))

    lv = lamv_ref[...]
    lam = (jnp.exp(jnp.sum(lv[0:1] * lv[1:2], axis=1, keepdims=True))
           - jnp.exp(jnp.sum(lv[2:3] * lv[3:4], axis=1, keepdims=True)) + lambda_init)
    inv_l0 = 1.0 / acc_scr[0, ATT_V_DIM:ATT_V_DIM + 1, :]
    inv_l1 = lam / acc_scr[1, ATT_V_DIM:ATT_V_DIM + 1, :]
    ot = acc_scr[0, :ATT_V_DIM, :] * inv_l0 - acc_scr[1, :ATT_V_DIM, :] * inv_l1
    ms = jnp.mean(ot * ot, axis=0, keepdims=True)
    ot = (ot * lax.rsqrt(ms + SUB_EPS)) * _lane_tile(subw_ref[...], tq)
    o_ref[...] = (ot * (1.0 - lambda_init)).T.astype(BF16)


def _attention(proj, slopes2, lamv, subw_col, batch, seq, lambda_init, tq=1024, tk=256, qc=256):
    assert tq % (2 * tk) == 0 and tq % qc == 0 and seq % tq == 0
    nq = seq // tq
    kern = functools.partial(_attn_kernel, tq=tq, tk=tk, qc=qc, lambda_init=lambda_init)
    kb, vb = COL_K // ATT_V_DIM, COL_V // ATT_V_DIM
    return pl.pallas_call(
        kern,
        grid=(batch, ATT_HEADS, nq),
        in_specs=[
            pl.BlockSpec(memory_space=pltpu.SMEM),
            pl.BlockSpec((4, ATT_QK_DIM), lambda b, h, i: (0, 0)),
            pl.BlockSpec((ATT_V_DIM, LANES), lambda b, h, i: (0, 0)),
            pl.BlockSpec((tq, ATT_V_DIM), lambda b, h, i: (b * nq + i, h)),
            pl.BlockSpec((seq, ATT_V_DIM), lambda b, h, i: (b, kb + h)),
            pl.BlockSpec((seq, ATT_V_DIM), lambda b, h, i: (b, vb + h)),
        ],
        out_specs=pl.BlockSpec((tq, ATT_V_DIM), lambda b, h, i: (b * nq + i, h)),
        out_shape=jax.ShapeDtypeStruct((batch * seq, ATT_WIDTH), BF16),
        scratch_shapes=[
            pltpu.VMEM((seq // tk, ATT_V_DIM + ONES_ROWS, tk), BF16),
            pltpu.VMEM((2, 2 * ATT_QK_DIM, tq), BF16),
            pltpu.VMEM((tk, ATT_QK_DIM), BF16),
            pltpu.VMEM((N_EARLY_CHAINS, tk, qc), F32),
            pltpu.VMEM((2, SUBLANES, tq), F32),
            pltpu.VMEM((2, ATT_V_DIM + ONES_ROWS, tq), F32),
        ],
        compiler_params=_params(("parallel", "parallel", "arbitrary")),
        name="diff_attention",
    )(slopes2, lamv, subw_col, proj, proj, proj)


def _ssd_kernel(z_ref, xs_ref, bc_ref, dt_ref, cwx_ref, cwbc_ref, cbx_ref, cbbc_ref,
                dtb_ref, alog_ref, dskip_ref, nw_ref, out_ref,
                tailx, tailbc, state, yscr):
    L = SSM_CHUNK

    @pl.when(pl.program_id(1) == 0)
    def _():
        tailx[...] = jnp.zeros(tailx.shape, F32)
        tailbc[...] = jnp.zeros(tailbc.shape, F32)
        state[...] = jnp.zeros(state.shape, F32)

    def conv_silu(u_ref, tail, w_ref, b_ref):
        u = u_ref[...].astype(F32)
        full = jnp.concatenate([tail[...], u], axis=0)
        acc = b_ref[...] + w_ref[SSM_CONV - 1:SSM_CONV, :] * u
        for k in range(SSM_CONV - 1):
            lo = SUBLANES - (SSM_CONV - 1) + k
            acc = acc + w_ref[k:k + 1, :] * full[lo:lo + L]
        tail[...] = u[L - SUBLANES:L]
        return acc * _sigmoid(acc)

    xs = conv_silu(xs_ref, tailx, cwx_ref, cbx_ref)
    bc = conv_silu(bc_ref, tailbc, cwbc_ref, cbbc_ref)
    xs_b = xs.astype(BF16)
    bc_b = bc.astype(BF16)

    dtr = dt_ref[...] + dtb_ref[...]
    dt = jnp.maximum(dtr, 0.0) + jnp.log1p(jnp.exp(-jnp.abs(dtr)))
    acs = dt * (-jnp.exp(alog_ref[...]))
    rowi = lax.broadcasted_iota(jnp.int32, (L, LANES), 0)
    shift = 1
    while shift < L:
        acs = acs + jnp.where(rowi >= shift, pltpu.roll(acs, shift, axis=0), 0.0)
        shift *= 2
    acs_t = acs.T
    dt_t = dt.T
    exp_a = jnp.exp(acs)
    w_t = dt_t * jnp.exp(acs_t[:, L - 1:L] - acs_t)
    cd_row = jnp.exp(acs[L - 1:L, :])

    li = lax.broadcasted_iota(jnp.int32, (L, L), 0)
    si = lax.broadcasted_iota(jnp.int32, (L, L), 1)
    causal = si <= li
    lane = lax.broadcasted_iota(jnp.int32, (L, PAIR_WIDTH), 1)
    lo_half = lane < SSM_HEAD_DIM
    lane_row = lax.broadcasted_iota(jnp.int32, (1, PAIR_WIDTH), 1) < SSM_HEAD_DIM
    zero_b = jnp.zeros((L, PAIR_WIDTH), BF16)

    gn = SSM_GROUPS * SSM_STATE
    for g in range(SSM_GROUPS):
        b_g = bc[:, g * SSM_STATE:(g + 1) * SSM_STATE]
        c_g = bc[:, gn + g * SSM_STATE:gn + (g + 1) * SSM_STATE]
        cb = lax.dot_general(bc_b[:, gn + g * SSM_STATE:gn + (g + 1) * SSM_STATE],
                             bc_b[:, g * SSM_STATE:(g + 1) * SSM_STATE],
                             (((1,), (1,)), ((), ())), preferred_element_type=F32)
        b_gt = b_g.T
        for pr in range(HEADS_PER_GROUP // 2):
            pair = g * (HEADS_PER_GROUP // 2) + pr
            lhs_y, lhs_s = [], []
            for hh in (2 * pair, 2 * pair + 1):
                seg = acs[:, hh:hh + 1] - acs_t[hh:hh + 1, :]
                decay = jnp.exp(jnp.where(causal, seg, NEG_INF))
                lhs_y.append((cb * decay * dt_t[hh:hh + 1, :]).astype(BF16))
                lhs_s.append((b_gt * w_t[hh:hh + 1, :]).astype(BF16))
            for hh in (2 * pair, 2 * pair + 1):
                lhs_y.append((c_g * exp_a[:, hh:hh + 1]).astype(BF16))
            x_p = xs_b[:, pair * PAIR_WIDTH:(pair + 1) * PAIR_WIDTH]
            x_lo = jnp.where(lo_half, x_p, zero_b)
            x_hi = jnp.where(lo_half, zero_b, x_p)
            st = state[pair]
            st_b = st.astype(BF16)
            st_lo = jnp.where(lo_half, st_b, zero_b)
            st_hi = jnp.where(lo_half, zero_b, st_b)
            y = jnp.dot(jnp.concatenate(lhs_y, axis=1),
                        jnp.concatenate([x_lo, x_hi, st_lo, st_hi], axis=0),
                        preferred_element_type=F32)
            yscr[:, pair * PAIR_WIDTH:(pair + 1) * PAIR_WIDTH] = y
            new = jnp.dot(jnp.concatenate(lhs_s, axis=1),
                          jnp.concatenate([x_lo, x_hi], axis=0),
                          preferred_element_type=F32)
            cd = jnp.where(lane_row, cd_row[:, 2 * pair:2 * pair + 1],
                           cd_row[:, 2 * pair + 1:2 * pair + 2])
            state[pair] = st * cd + new

    y = yscr[...] + dskip_ref[...] * xs
    z = z_ref[...].astype(F32)
    y = y * (z * _sigmoid(z))
    gw = SSM_WIDTH // SSM_GROUPS
    for g in range(SSM_GROUPS):
        cols = slice(g * gw, (g + 1) * gw)
        out_ref[:, cols] = _rmsnorm(y[:, cols], nw_ref[:, cols], SUB_EPS).astype(BF16)


def _ssd(proj, dt_raw, conv_w, conv_b, dt_bias, a_log, d_skip_vec, norm_w, batch, seq):
    nc = seq // SSM_CHUNK
    L = SSM_CHUNK
    row = lambda b, c: b * nc + c
    full = lambda shape: pl.BlockSpec(shape, lambda b, c: (0, 0))
    cwx, cwbc = conv_w[:, :SSM_WIDTH], conv_w[:, SSM_WIDTH:]
    cbx, cbbc = conv_b[:, :SSM_WIDTH], conv_b[:, SSM_WIDTH:]
    return pl.pallas_call(
        _ssd_kernel,
        grid=(batch, nc),
        in_specs=[
            pl.BlockSpec((L, SSM_WIDTH), lambda b, c: (row(b, c), COL_Z // SSM_WIDTH)),
            pl.BlockSpec((L, SSM_WIDTH), lambda b, c: (row(b, c), COL_XS // SSM_WIDTH)),
            pl.BlockSpec((L, SSM_BC_WIDTH), lambda b, c: (row(b, c), COL_BC // SSM_BC_WIDTH)),
            pl.BlockSpec((L, LANES), lambda b, c: (row(b, c), 0)),
            full((SSM_CONV, SSM_WIDTH)),
            full((SSM_CONV, SSM_BC_WIDTH)),
            full((1, SSM_WIDTH)),
            full((1, SSM_BC_WIDTH)),
            full((1, LANES)),
            full((1, LANES)),
            full((1, SSM_WIDTH)),
            full((1, SSM_WIDTH)),
        ],
        out_specs=pl.BlockSpec((L, SSM_WIDTH), lambda b, c: (row(b, c), 0)),
        out_shape=jax.ShapeDtypeStruct((batch * seq, SSM_WIDTH), BF16),
        scratch_shapes=[
            pltpu.VMEM((SUBLANES, SSM_WIDTH), F32),
            pltpu.VMEM((SUBLANES, SSM_BC_WIDTH), F32),
            pltpu.VMEM((SSM_HEADS // 2, SSM_STATE, PAIR_WIDTH), F32),
            pltpu.VMEM((L, SSM_WIDTH), F32),
        ],
        compiler_params=_params(("parallel", "arbitrary")),
        name="ssd_mixer",
    )(proj, proj, proj, dt_raw, cwx, cwbc, cbx, cbbc, dt_bias, a_log, d_skip_vec, norm_w)


def _out_proj_kernel(att_ref, ssm_ref, w_ref, x_ref, x1_ref):
    acc = jnp.dot(att_ref[...], w_ref[0], preferred_element_type=F32)
    acc = acc + jnp.dot(ssm_ref[...], w_ref[1], preferred_element_type=F32)
    x1_ref[...] = x_ref[...] + acc


def _out_proj(att, ssm, w_out2, x2d, tm=1024, tn=512):
    m = att.shape[0]
    return pl.pallas_call(
        _out_proj_kernel,
        grid=(m // tm, D_MODEL // tn),
        in_specs=[
            pl.BlockSpec((tm, ATT_WIDTH), lambda i, j: (i, 0)),
            pl.BlockSpec((tm, SSM_WIDTH), lambda i, j: (i, 0)),
            pl.BlockSpec((2, ATT_WIDTH, tn), lambda i, j: (0, 0, j)),
            pl.BlockSpec((tm, tn), lambda i, j: (i, j)),
        ],
        out_specs=pl.BlockSpec((tm, tn), lambda i, j: (i, j)),
        out_shape=jax.ShapeDtypeStruct((m, D_MODEL), F32),
        compiler_params=_params(("parallel", "arbitrary")),
        name="out_proj",
    )(att, ssm, w_out2, x2d)


def _gate_up_kernel(x_ref, nw_ref, wg_ref, wu_ref, a_ref, h_ref):
    @pl.when(pl.program_id(1) == 0)
    def _():
        h_ref[...] = _rmsnorm(x_ref[...], nw_ref[...], RMS_EPS).astype(BF16)

    h = h_ref[...]
    g = jnp.dot(h, wg_ref[...], preferred_element_type=F32)
    u = jnp.dot(h, wu_ref[...], preferred_element_type=F32)
    a_ref[...] = ((g * _sigmoid(g)) * u).astype(BF16)


def _gate_up(x1, norm_w, w_gate, w_up, tm=1024, tn=512):
    m = x1.shape[0]
    n = w_gate.shape[1]
    return pl.pallas_call(
        _gate_up_kernel,
        grid=(m // tm, n // tn),
        in_specs=[
            pl.BlockSpec((tm, D_MODEL), lambda i, j: (i, 0)),
            pl.BlockSpec((1, D_MODEL), lambda i, j: (0, 0)),
            pl.BlockSpec((D_MODEL, tn), lambda i, j: (0, j)),
            pl.BlockSpec((D_MODEL, tn), lambda i, j: (0, j)),
        ],
        out_specs=pl.BlockSpec((tm, tn), lambda i, j: (i, j)),
        out_shape=jax.ShapeDtypeStruct((m, n), BF16),
        scratch_shapes=[pltpu.VMEM((tm, D_MODEL), BF16)],
        compiler_params=_params(("parallel", "arbitrary")),
        name="gate_up",
    )(x1, norm_w, w_gate, w_up)


def _down_kernel(a_ref, w_ref, x_ref, nw_ref, o_ref, x2_ref):
    j = pl.program_id(1)
    x2_ref[j] = x_ref[...] + jnp.dot(a_ref[...], w_ref[...], preferred_element_type=F32)

    @pl.when(j == pl.num_programs(1) - 1)
    def _():
        x2 = jnp.concatenate([x2_ref[t] for t in range(x2_ref.shape[0])], axis=1)
        o_ref[...] = _rmsnorm(x2, nw_ref[...], RMS_EPS)


def _down(a, w_down, x1, norm_w, tm=512, tn=512):
    m, kdim = a.shape
    return pl.pallas_call(
        _down_kernel,
        grid=(m // tm, D_MODEL // tn),
        in_specs=[
            pl.BlockSpec((tm, kdim), lambda i, j: (i, 0)),
            pl.BlockSpec((kdim, tn), lambda i, j: (0, j)),
            pl.BlockSpec((tm, tn), lambda i, j: (i, j)),
            pl.BlockSpec((1, D_MODEL), lambda i, j: (0, 0)),
        ],
        out_specs=pl.BlockSpec((tm, D_MODEL), lambda i, j: (i, 0)),
        out_shape=jax.ShapeDtypeStruct((m, D_MODEL), F32),
        scratch_shapes=[pltpu.VMEM((D_MODEL // tn, tm, tn), F32)],
        compiler_params=_params(("parallel", "arbitrary")),
        name="down_proj",
    )(a, w_down, x1, norm_w)


def _alibi_slopes_log2(n):
    start = 2.0 ** (-8.0 / n)
    return jnp.asarray([start ** (i + 1) for i in range(n)], dtype=F32) * LOG2E


def _pad_lanes(v):
    return jnp.pad(v.astype(F32), (0, LANES - v.shape[0])).reshape(1, LANES)


def kernel(x, norm_mix_w, w_in, lambda_q1, lambda_k1, lambda_q2, lambda_k2, subln_w, conv_w, conv_b, dt_bias, a_log, d_skip, ssm_norm_w, w_out, norm_ffn_w, w_gate, w_up, w_down, norm_final_w):
    batch, seq, _ = x.shape
    assert w_in.shape[0] == 1, "single-layer block"
    layer = 0
    x2d = x.reshape(batch * seq, D_MODEL)
    col_scale = jnp.concatenate([
        jnp.full((ATT_WIDTH,), ATT_QK_DIM ** -0.5 * LOG2E, F32),
        jnp.ones((PROJ_MAIN - ATT_WIDTH,), F32)]).reshape(1, PROJ_MAIN)
    slopes2 = _alibi_slopes_log2(ATT_HEADS)
    lambda_init = 0.8 - 0.6 * math.exp(-0.3 * layer)
    w_in_b = w_in[layer].astype(BF16)
    w_dt = jnp.pad(w_in[layer, :, PROJ_MAIN:], ((0, 0), (0, LANES - SSM_HEADS))).astype(BF16)
    proj, dt_raw = _in_proj(x2d, norm_mix_w[layer].reshape(1, D_MODEL), w_in_b, w_dt, col_scale)
    lamv = jnp.stack([lambda_q1[layer], lambda_k1[layer], lambda_q2[layer], lambda_k2[layer]]).astype(F32)
    subw_col = jnp.broadcast_to(subln_w[layer].astype(F32).reshape(ATT_V_DIM, 1), (ATT_V_DIM, LANES))
    att = _attention(proj, slopes2, lamv, subw_col, batch, seq, lambda_init)
    ssm = _ssd(proj, dt_raw, conv_w[layer], conv_b[layer].reshape(1, -1),
               _pad_lanes(dt_bias[layer]), _pad_lanes(a_log[layer]),
               jnp.repeat(d_skip[layer].astype(F32), SSM_HEAD_DIM).reshape(1, SSM_WIDTH),
               ssm_norm_w[layer].reshape(1, SSM_WIDTH), batch, seq)
    w_out2 = w_out[layer].astype(BF16).reshape(2, ATT_WIDTH, D_MODEL)
    x1 = _out_proj(att, ssm, w_out2, x2d)
    a = _gate_up(x1, norm_ffn_w[layer].reshape(1, D_MODEL), w_gate[layer].astype(BF16),
                 w_up[layer].astype(BF16))
    out = _down(a, w_down[layer].astype(BF16), x1, norm_final_w.reshape(1, D_MODEL))
    return out.reshape(batch, seq, D_MODEL)
```

```python
import functools
import math

import jax
import jax.numpy as jnp
from jax import lax
from jax.experimental import pallas as pl
from jax.experimental.pallas import tpu as pltpu

F32 = jnp.float32
BF16 = jnp.bfloat16

D_MODEL = 2048
ATT_HEADS = 8
ATT_QK_DIM = 128
ATT_V_DIM = 2 * ATT_QK_DIM
ATT_WIDTH = ATT_HEADS * ATT_V_DIM
SSM_HEADS = 32
SSM_HEAD_DIM = 64
SSM_WIDTH = SSM_HEADS * SSM_HEAD_DIM
SSM_GROUPS = 4
SSM_STATE = 128
SSM_CONV = 4
SSM_CHUNK = 128
SSM_BC_WIDTH = 2 * SSM_GROUPS * SSM_STATE
PROJ_MAIN = 2 * ATT_WIDTH + ATT_WIDTH + SSM_WIDTH + SSM_WIDTH + SSM_BC_WIDTH
RMS_EPS = 1e-6
SUB_EPS = 1e-5
LOG2E = 1.4426950408889634
LANES = 128
SUBLANES = 8
ONES_ROWS = 16
N_EARLY_CHAINS = 8
HEADS_PER_GROUP = SSM_HEADS // SSM_GROUPS
PAIR_WIDTH = 2 * SSM_HEAD_DIM
NEG_INF = float("-inf")

COL_Q, COL_K, COL_V = 0, ATT_WIDTH, 2 * ATT_WIDTH
COL_Z = 3 * ATT_WIDTH
COL_XS = COL_Z + SSM_WIDTH
COL_BC = COL_XS + SSM_WIDTH

VMEM_LIMIT = 56 * 1024 * 1024


def _params(semantics):
    return pltpu.CompilerParams(dimension_semantics=semantics, vmem_limit_bytes=VMEM_LIMIT)


def _sigmoid(x):
    return 1.0 / (1.0 + jnp.exp(-x))


def _rmsnorm(x, w, eps):
    ms = jnp.mean(x * x, axis=-1, keepdims=True)
    return (x * lax.rsqrt(ms + eps)) * w


def _lane_tile(x, width):
    return x if width == LANES else jnp.concatenate([x] * (width // LANES), axis=1)


_CONTRACT_LAST = (((1,), (1,)), ((), ()))


def _in_proj_kernel(x_ref, nw_ref, wt_ref, wdt_ref, cs_ref, proj_ref, dt_ref, h_ref):
    @pl.when(pl.program_id(1) == 0)
    def _():
        hb = _rmsnorm(x_ref[...], nw_ref[...], RMS_EPS).astype(BF16)
        h_ref[...] = hb
        dt_ref[...] = lax.dot_general(hb, wdt_ref[...], _CONTRACT_LAST,
                                      preferred_element_type=F32)

    acc = lax.dot_general(h_ref[...], wt_ref[...].astype(BF16), _CONTRACT_LAST,
                          preferred_element_type=F32)
    proj_ref[...] = (acc * cs_ref[...]).astype(BF16)


def _in_proj(x2d, norm_w, w_in_t, w_dt_t, col_scale, tm=1024, tn=1024):
    m = x2d.shape[0]
    return pl.pallas_call(
        _in_proj_kernel,
        grid=(m // tm, PROJ_MAIN // tn),
        in_specs=[
            pl.BlockSpec((tm, D_MODEL), lambda i, j: (i, 0)),
            pl.BlockSpec((1, D_MODEL), lambda i, j: (0, 0)),
            pl.BlockSpec((tn, D_MODEL), lambda i, j: (j, 0)),
            pl.BlockSpec((LANES, D_MODEL), lambda i, j: (0, 0)),
            pl.BlockSpec((1, tn), lambda i, j: (0, j)),
        ],
        out_specs=[
            pl.BlockSpec((tm, tn), lambda i, j: (i, j)),
            pl.BlockSpec((tm, LANES), lambda i, j: (i, 0)),
        ],
        out_shape=[
            jax.ShapeDtypeStruct((m, PROJ_MAIN), BF16),
            jax.ShapeDtypeStruct((m, LANES), F32),
        ],
        scratch_shapes=[pltpu.VMEM((tm, D_MODEL), BF16)],
        compiler_params=_params(("parallel", "arbitrary")),
        name="in_proj",
    )(x2d, norm_w, w_in_t, w_dt_t, col_scale)


def _attn_kernel(slope_ref, lamv_ref, subw_ref, q_ref, k_ref, v_ref, o_ref,
                 vt_scr, qt_scr, kaug_scr, st_scr, m_scr, acc_scr, *, tq, tk, qc, lambda_init):
    h = pl.program_id(1)
    qi = pl.program_id(2)
    n_kblocks = vt_scr.shape[0]
    slope2 = slope_ref[h]
    aug_lane = lax.broadcasted_iota(jnp.int32, (tk, ATT_QK_DIM), 1)

    @pl.when(qi == 0)
    def _():
        def transpose_block(jb, carry):
            vt = v_ref[pl.ds(pl.multiple_of(jb * tk, tk), tk), :].T
            vt_scr[jb] = jnp.concatenate([vt, jnp.ones((ONES_ROWS, tk), BF16)], axis=0)
            return carry
        lax.fori_loop(0, n_kblocks, transpose_block, 0)
        bias = slope2 * lax.broadcasted_iota(jnp.int32, (tk, ATT_QK_DIM), 0).astype(F32)
        hi = bias.astype(BF16).astype(F32)
        mid = (bias - hi).astype(BF16).astype(F32)
        lo = (bias - hi) - mid
        kaug_scr[...] = jnp.where(aug_lane == 0, hi, jnp.where(
            aug_lane == 1, mid, jnp.where(aug_lane == 2, lo, 0.0))).astype(BF16)

    qt = q_ref[...].T
    pick = jnp.where(lax.broadcasted_iota(jnp.int32, (ATT_QK_DIM, tq), 0) < 3, 1.0, 0.0).astype(BF16)
    for c in range(2):
        qt_scr[c] = jnp.concatenate([qt[c * ATT_QK_DIM:(c + 1) * ATT_QK_DIM, :], pick], axis=0)
    m_scr[...] = jnp.full(m_scr.shape, NEG_INF, F32)
    acc_scr[...] = jnp.zeros(acc_scr.shape, F32)
    row_minus_col = (lax.broadcasted_iota(jnp.int32, (tk, qc), 0)
                     - lax.broadcasted_iota(jnp.int32, (tk, qc), 1))

    def scores(j, c, u):
        k0 = j * tk if isinstance(j, int) else pl.multiple_of(j * tk, tk)
        dims = slice(c * ATT_QK_DIM, (c + 1) * ATT_QK_DIM)
        k_aug = jnp.concatenate([k_ref[pl.ds(k0, tk), dims], kaug_scr[...]], axis=1)
        return jnp.dot(k_aug, qt_scr[c, :, u * qc:(u + 1) * qc], preferred_element_type=F32)

    n_qchunks = tq // qc

    def block_chains(d=None):
        out = []
        for u in range(n_qchunks):
            if d is not None and d * tk >= (u + 1) * qc:
                continue
            thr = None if d is None or (d + 1) * tk <= u * qc else u * qc - d * tk
            out += [(u, c, thr) for c in range(2)]
        return out

    def prefetch_scores(j, chains):
        for slot, (u, c, _) in enumerate(chains[:N_EARLY_CHAINS]):
            st_scr[slot] = scores(j, c, u)

    def chain(j, c, u, thr, slot=None):
        cs = slice(u * qc, (u + 1) * qc)
        st = scores(j, c, u) if slot is None else st_scr[slot]
        if thr is not None:
            st = jnp.where(row_minus_col <= thr, st, NEG_INF)
        off = slope2 * jnp.asarray(j * tk, F32)
        m_old = m_scr[c, 0:1, cs]
        m_new = jnp.maximum(m_old, jnp.max(st, axis=0, keepdims=True) + off)
        alpha = jnp.exp2(m_old - m_new)
        pt = jnp.exp2(st - (m_new - off))
        m_scr[c, :, cs] = jnp.broadcast_to(m_new, (SUBLANES, qc))
        acc_scr[c, :, cs] = alpha * acc_scr[c, :, cs] + jnp.dot(
            vt_scr[j], pt.astype(BF16), preferred_element_type=F32)

    def run_block(j, chains, prefetched):
        for i, (u, c, thr) in enumerate(chains):
            chain(j, c, u, thr, i if prefetched and i < N_EARLY_CHAINS else None)

    def body(jj, carry):
        run_block(2 * jj, block_chains(), True)
        run_block(2 * jj + 1, block_chains(), False)
        prefetch_scores(2 * jj + 2, block_chains())
        return carry

    n_full = (qi * tq) // tk
    prefetch_scores(0, block_chains())
    lax.fori_loop(0, n_full // 2, body, 0)
    for d in range(tq // tk):
        run_block(n_full + d, block_chains(d), prefetched=(d == 0))

    lv = lamv_ref[...]
    lam = (jnp.exp(jnp.sum(lv[0:1] * lv[1:2], axis=1, keepdims=True))
           - jnp.exp(jnp.sum(lv[2:3] * lv[3:4], axis=1, keepdims=True)) + lambda_init)
    inv_l0 = 1.0 / acc_scr[0, ATT_V_DIM:ATT_V_DIM + 1, :]
    inv_l1 = lam / acc_scr[1, ATT_V_DIM:ATT_V_DIM + 1, :]
    ot = acc_scr[0, :ATT_V_DIM, :] * inv_l0 - acc_scr[1, :ATT_V_DIM, :] * inv_l1
    ms = jnp.mean(ot * ot, axis=0, keepdims=True)
    ot = (ot * lax.rsqrt(ms + SUB_EPS)) * _lane_tile(subw_ref[...], tq)
    o_ref[...] = (ot * (1.0 - lambda_init)).T.astype(BF16)


def _attention(proj, slopes2, lamv, subw_col, batch, seq, lambda_init, tq=1024, tk=256, qc=256):
    assert tq % (2 * tk) == 0 and tq % qc == 0 and seq % tq == 0
    nq = seq // tq
    kern = functools.partial(_attn_kernel, tq=tq, tk=tk, qc=qc, lambda_init=lambda_init)
    kb, vb = COL_K // ATT_V_DIM, COL_V // ATT_V_DIM
    return pl.pallas_call(
        kern,
        grid=(batch, ATT_HEADS, nq),
        in_specs=[
            pl.BlockSpec(memory_space=pltpu.SMEM),
            pl.BlockSpec((4, ATT_QK_DIM), lambda b, h, i: (0, 0)),
            pl.BlockSpec((ATT_V_DIM, LANES), lambda b, h, i: (0, 0)),
            pl.BlockSpec((tq, ATT_V_DIM), lambda b, h, i: (b * nq + i, h)),
            pl.BlockSpec((seq, ATT_V_DIM), lambda b, h, i: (b, kb + h)),
            pl.BlockSpec((seq, ATT_V_DIM), lambda b, h, i: (b, vb + h)),
        ],
        out_specs=pl.BlockSpec((tq, ATT_V_DIM), lambda b, h, i: (b * nq + i, h)),
        out_shape=jax.ShapeDtypeStruct((batch * seq, ATT_WIDTH), BF16),
        scratch_shapes=[
            pltpu.VMEM((seq // tk, ATT_V_DIM + ONES_ROWS, tk), BF16),
            pltpu.VMEM((2, 2 * ATT_QK_DIM, tq), BF16),
            pltpu.VMEM((tk, ATT_QK_DIM), BF16),
            pltpu.VMEM((N_EARLY_CHAINS, tk, qc), F32),
            pltpu.VMEM((2, SUBLANES, tq), F32),
            pltpu.VMEM((2, ATT_V_DIM + ONES_ROWS, tq), F32),
        ],
        compiler_params=_params(("parallel", "parallel", "arbitrary")),
        name="diff_attention",
    )(slopes2, lamv, subw_col, proj, proj, proj)


def _ssd_kernel(z_ref, xs_ref, bc_ref, dt_ref, cwx_ref, cwbc_ref, cbx_ref, cbbc_ref,
                dtb_ref, alog_ref, dskip_ref, nw_ref, out_ref,
                tailx, tailbc, state, yscr):
    L = SSM_CHUNK

    @pl.when(pl.program_id(1) == 0)
    def _():
        tailx[...] = jnp.zeros(tailx.shape, F32)
        tailbc[...] = jnp.zeros(tailbc.shape, F32)
        state[...] = jnp.zeros(state.shape, F32)

    def conv_silu(u_ref, tail, w_ref, b_ref):
        u = u_ref[...].astype(F32)
        full = jnp.concatenate([tail[...], u], axis=0)
        acc = b_ref[...] + w_ref[SSM_CONV - 1:SSM_CONV, :] * u
        for k in range(SSM_CONV - 1):
            lo = SUBLANES - (SSM_CONV - 1) + k
            acc = acc + w_ref[k:k + 1, :] * full[lo:lo + L]
        tail[...] = u[L - SUBLANES:L]
        return acc * _sigmoid(acc)

    xs = conv_silu(xs_ref, tailx, cwx_ref, cbx_ref)
    bc = conv_silu(bc_ref, tailbc, cwbc_ref, cbbc_ref)
    xs_b = xs.astype(BF16)
    bc_b = bc.astype(BF16)

    dtr = dt_ref[...] + dtb_ref[...]
    dt = jnp.maximum(dtr, 0.0) + jnp.log1p(jnp.exp(-jnp.abs(dtr)))
    acs = dt * (-jnp.exp(alog_ref[...]))
    rowi = lax.broadcasted_iota(jnp.int32, (L, LANES), 0)
    shift = 1
    while shift < L:
        acs = acs + jnp.where(rowi >= shift, pltpu.roll(acs, shift, axis=0), 0.0)
        shift *= 2
    acs_t = acs.T
    dt_t = dt.T
    exp_a = jnp.exp(acs)
    w_t = dt_t * jnp.exp(acs_t[:, L - 1:L] - acs_t)
    cd_row = jnp.exp(acs[L - 1:L, :])

    li = lax.broadcasted_iota(jnp.int32, (L, L), 0)
    si = lax.broadcasted_iota(jnp.int32, (L, L), 1)
    causal = si <= li
    lane = lax.broadcasted_iota(jnp.int32, (L, PAIR_WIDTH), 1)
    lo_half = lane < SSM_HEAD_DIM
    lane_row = lax.broadcasted_iota(jnp.int32, (1, PAIR_WIDTH), 1) < SSM_HEAD_DIM
    zero_b = jnp.zeros((L, PAIR_WIDTH), BF16)

    gn = SSM_GROUPS * SSM_STATE
    for g in range(SSM_GROUPS):
        b_g = bc[:, g * SSM_STATE:(g + 1) * SSM_STATE]
        c_g = bc[:, gn + g * SSM_STATE:gn + (g + 1) * SSM_STATE]
        cb = lax.dot_general(bc_b[:, gn + g * SSM_STATE:gn + (g + 1) * SSM_STATE],
                             bc_b[:, g * SSM_STATE:(g + 1) * SSM_STATE],
                             (((1,), (1,)), ((), ())), preferred_element_type=F32)
        b_gt = b_g.T
        for pr in range(HEADS_PER_GROUP // 2):
            pair = g * (HEADS_PER_GROUP // 2) + pr
            lhs_y, lhs_s = [], []
            for hh in (2 * pair, 2 * pair + 1):
                seg = acs[:, hh:hh + 1] - acs_t[hh:hh + 1, :]
                decay = jnp.exp(jnp.where(causal, seg, NEG_INF))
                lhs_y.append((cb * decay * dt_t[hh:hh + 1, :]).astype(BF16))
                lhs_s.append((b_gt * w_t[hh:hh + 1, :]).astype(BF16))
            for hh in (2 * pair, 2 * pair + 1):
                lhs_y.append((c_g * exp_a[:, hh:hh + 1]).astype(BF16))
            x_p = xs_b[:, pair * PAIR_WIDTH:(pair + 1) * PAIR_WIDTH]
            x_lo = jnp.where(lo_half, x_p, zero_b)
            x_hi = jnp.where(lo_half, zero_b, x_p)
            st = state[pair]
            st_b = st.astype(BF16)
            st_lo = jnp.where(lo_half, st_b, zero_b)
            st_hi = jnp.where(lo_half, zero_b, st_b)
            y = jnp.dot(jnp.concatenate(lhs_y, axis=1),
                        jnp.concatenate([x_lo, x_hi, st_lo, st_hi], axis=0),
                        preferred_element_type=F32)
            yscr[:, pair * PAIR_WIDTH:(pair + 1) * PAIR_WIDTH] = y
            new = jnp.dot(jnp.concatenate(lhs_s, axis=1),
                          jnp.concatenate([x_lo, x_hi], axis=0),
                          preferred_element_type=F32)
            cd = jnp.where(lane_row, cd_row[:, 2 * pair:2 * pair + 1],
                           cd_row[:, 2 * pair + 1:2 * pair + 2])
            state[pair] = st * cd + new

    y = yscr[...] + dskip_ref[...] * xs
    z = z_ref[...].astype(F32)
    y = y * (z * _sigmoid(z))
    gw = SSM_WIDTH // SSM_GROUPS
    for g in range(SSM_GROUPS):
        cols = slice(g * gw, (g + 1) * gw)
        out_ref[:, cols] = _rmsnorm(y[:, cols], nw_ref[:, cols], SUB_EPS).astype(BF16)


def _ssd(proj, dt_raw, conv_w, conv_b, dt_bias, a_log, d_skip_vec, norm_w, batch, seq):
    nc = seq // SSM_CHUNK
    L = SSM_CHUNK
    row = lambda b, c: b * nc + c
    full = lambda shape: pl.BlockSpec(shape, lambda b, c: (0, 0))
    cwx, cwbc = conv_w[:, :SSM_WIDTH], conv_w[:, SSM_WIDTH:]
    cbx, cbbc = conv_b[:, :SSM_WIDTH], conv_b[:, SSM_WIDTH:]
    return pl.pallas_call(
        _ssd_kernel,
        grid=(batch, nc),
        in_specs=[
            pl.BlockSpec((L, SSM_WIDTH), lambda b, c: (row(b, c), COL_Z // SSM_WIDTH)),
            pl.BlockSpec((L, SSM_WIDTH), lambda b, c: (row(b, c), COL_XS // SSM_WIDTH)),
            pl.BlockSpec((L, SSM_BC_WIDTH), lambda b, c: (row(b, c), COL_BC // SSM_BC_WIDTH)),
            pl.BlockSpec((L, LANES), lambda b, c: (row(b, c), 0)),
            full((SSM_CONV, SSM_WIDTH)),
            full((SSM_CONV, SSM_BC_WIDTH)),
            full((1, SSM_WIDTH)),
            full((1, SSM_BC_WIDTH)),
            full((1, LANES)),
            full((1, LANES)),
            full((1, SSM_WIDTH)),
            full((1, SSM_WIDTH)),
        ],
        out_specs=pl.BlockSpec((L, SSM_WIDTH), lambda b, c: (row(b, c), 0)),
        out_shape=jax.ShapeDtypeStruct((batch * seq, SSM_WIDTH), BF16),
        scratch_shapes=[
            pltpu.VMEM((SUBLANES, SSM_WIDTH), F32),
            pltpu.VMEM((SUBLANES, SSM_BC_WIDTH), F32),
            pltpu.VMEM((SSM_HEADS // 2, SSM_STATE, PAIR_WIDTH), F32),
            pltpu.VMEM((L, SSM_WIDTH), F32),
        ],
        compiler_params=_params(("parallel", "arbitrary")),
        name="ssd_mixer",
    )(proj, proj, proj, dt_raw, cwx, cwbc, cbx, cbbc, dt_bias, a_log, d_skip_vec, norm_w)


def _out_proj_kernel(att_ref, ssm_ref, w_ref, x_ref, x1_ref):
    acc = jnp.dot(att_ref[...], w_ref[0].astype(BF16), preferred_element_type=F32)
    acc = acc + jnp.dot(ssm_ref[...], w_ref[1].astype(BF16), preferred_element_type=F32)
    x1_ref[...] = x_ref[...] + acc


def _out_proj(att, ssm, w_out2, x2d, tm=1024, tn=512):
    m = att.shape[0]
    return pl.pallas_call(
        _out_proj_kernel,
        grid=(m // tm, D_MODEL // tn),
        in_specs=[
            pl.BlockSpec((tm, ATT_WIDTH), lambda i, j: (i, 0)),
            pl.BlockSpec((tm, SSM_WIDTH), lambda i, j: (i, 0)),
            pl.BlockSpec((2, ATT_WIDTH, tn), lambda i, j: (0, 0, j)),
            pl.BlockSpec((tm, tn), lambda i, j: (i, j)),
        ],
        out_specs=pl.BlockSpec((tm, tn), lambda i, j: (i, j)),
        out_shape=jax.ShapeDtypeStruct((m, D_MODEL), F32),
        compiler_params=_params(("parallel", "arbitrary")),
        name="out_proj",
    )(att, ssm, w_out2, x2d)


def _gate_up_kernel(x_ref, nw_ref, wg_ref, wu_ref, a_ref, h_ref):
    @pl.when(pl.program_id(1) == 0)
    def _():
        h_ref[...] = _rmsnorm(x_ref[...], nw_ref[...], RMS_EPS).astype(BF16)

    h = h_ref[...]
    g = jnp.dot(h, wg_ref[...].astype(BF16), preferred_element_type=F32)
    u = jnp.dot(h, wu_ref[...].astype(BF16), preferred_element_type=F32)
    a_ref[...] = ((g * _sigmoid(g)) * u).astype(BF16)


def _gate_up(x1, norm_w, w_gate, w_up, tm=1024, tn=512):
    m = x1.shape[0]
    n = w_gate.shape[1]
    return pl.pallas_call(
        _gate_up_kernel,
        grid=(m // tm, n // tn),
        in_specs=[
            pl.BlockSpec((tm, D_MODEL), lambda i, j: (i, 0)),
            pl.BlockSpec((1, D_MODEL), lambda i, j: (0, 0)),
            pl.BlockSpec((D_MODEL, tn), lambda i, j: (0, j)),
            pl.BlockSpec((D_MODEL, tn), lambda i, j: (0, j)),
        ],
        out_specs=pl.BlockSpec((tm, tn), lambda i, j: (i, j)),
        out_shape=jax.ShapeDtypeStruct((m, n), BF16),
        scratch_shapes=[pltpu.VMEM((tm, D_MODEL), BF16)],
        compiler_params=_params(("parallel", "arbitrary")),
        name="gate_up",
    )(x1, norm_w, w_gate, w_up)


def _down_kernel(a_ref, w_ref, x_ref, nw_ref, o_ref, x2_ref):
    j = pl.program_id(1)
    x2_ref[j] = x_ref[...] + jnp.dot(a_ref[...], w_ref[...], preferred_element_type=F32)

    @pl.when(j == pl.num_programs(1) - 1)
    def _():
        x2 = jnp.concatenate([x2_ref[t] for t in range(x2_ref.shape[0])], axis=1)
        o_ref[...] = _rmsnorm(x2, nw_ref[...], RMS_EPS)


def _down(a, w_down, x1, norm_w, tm=512, tn=512):
    m, kdim = a.shape
    return pl.pallas_call(
        _down_kernel,
        grid=(m // tm, D_MODEL // tn),
        in_specs=[
            pl.BlockSpec((tm, kdim), lambda i, j: (i, 0)),
            pl.BlockSpec((kdim, tn), lambda i, j: (0, j)),
            pl.BlockSpec((tm, tn), lambda i, j: (i, j)),
            pl.BlockSpec((1, D_MODEL), lambda i, j: (0, 0)),
        ],
        out_specs=pl.BlockSpec((tm, D_MODEL), lambda i, j: (i, 0)),
        out_shape=jax.ShapeDtypeStruct((m, D_MODEL), F32),
        scratch_shapes=[pltpu.VMEM((D_MODEL // tn, tm, tn), F32)],
        compiler_params=_params(("parallel", "arbitrary")),
        name="down_proj",
    )(a, w_down, x1, norm_w)


def _alibi_slopes_log2(n):
    start = 2.0 ** (-8.0 / n)
    return jnp.asarray([start ** (i + 1) for i in range(n)], dtype=F32) * LOG2E


def _pad_lanes(v):
    return jnp.pad(v.astype(F32), (0, LANES - v.shape[0])).reshape(1, LANES)


def kernel(x, norm_mix_w, w_in, lambda_q1, lambda_k1, lambda_q2, lambda_k2, subln_w, conv_w, conv_b, dt_bias, a_log, d_skip, ssm_norm_w, w_out, norm_ffn_w, w_gate, w_up, w_down, norm_final_w):
    batch, seq, _ = x.shape
    assert w_in.shape[0] == 1, "single-layer block"
    layer = 0
    x2d = x.reshape(batch * seq, D_MODEL)
    col_scale = jnp.concatenate([
        jnp.full((ATT_WIDTH,), ATT_QK_DIM ** -0.5 * LOG2E, F32),
        jnp.ones((PROJ_MAIN - ATT_WIDTH,), F32)]).reshape(1, PROJ_MAIN)
    slopes2 = _alibi_slopes_log2(ATT_HEADS)
    lambda_init = 0.8 - 0.6 * math.exp(-0.3 * layer)
    w_in_t = w_in[layer].T
    w_dt_t = jnp.pad(w_in_t[PROJ_MAIN:], ((0, LANES - SSM_HEADS), (0, 0))).astype(BF16)
    proj, dt_raw = _in_proj(x2d, norm_mix_w[layer].reshape(1, D_MODEL), w_in_t, w_dt_t, col_scale)
    lamv = jnp.stack([lambda_q1[layer], lambda_k1[layer], lambda_q2[layer], lambda_k2[layer]]).astype(F32)
    subw_col = jnp.broadcast_to(subln_w[layer].astype(F32).reshape(ATT_V_DIM, 1), (ATT_V_DIM, LANES))
    att = _attention(proj, slopes2, lamv, subw_col, batch, seq, lambda_init)
    ssm = _ssd(proj, dt_raw, conv_w[layer], conv_b[layer].reshape(1, -1),
               _pad_lanes(dt_bias[layer]), _pad_lanes(a_log[layer]),
               jnp.repeat(d_skip[layer].astype(F32), SSM_HEAD_DIM).reshape(1, SSM_WIDTH),
               ssm_norm_w[layer].reshape(1, SSM_WIDTH), batch, seq)
    w_out2 = w_out[layer].reshape(2, ATT_WIDTH, D_MODEL)
    x1 = _out_proj(att, ssm, w_out2, x2d)
    a = _gate_up(x1, norm_ffn_w[layer].reshape(1, D_MODEL), w_gate[layer], w_up[layer])
    out = _down(a, w_down[layer].astype(BF16), x1, norm_final_w.reshape(1, D_MODEL))
    return out.reshape(batch, seq, D_MODEL)
```

```python
import functools
import math

import jax
import jax.numpy as jnp
from jax import lax
from jax.experimental import pallas as pl
from jax.experimental.pallas import tpu as pltpu

F32 = jnp.float32
BF16 = jnp.bfloat16

D_MODEL = 2048
ATT_HEADS = 8
ATT_QK_DIM = 128
ATT_V_DIM = 2 * ATT_QK_DIM
ATT_WIDTH = ATT_HEADS * ATT_V_DIM
SSM_HEADS = 32
SSM_HEAD_DIM = 64
SSM_WIDTH = SSM_HEADS * SSM_HEAD_DIM
SSM_GROUPS = 4
SSM_STATE = 128
SSM_CONV = 4
SSM_CHUNK = 128
SSM_BC_WIDTH = 2 * SSM_GROUPS * SSM_STATE
PROJ_MAIN = 2 * ATT_WIDTH + ATT_WIDTH + SSM_WIDTH + SSM_WIDTH + SSM_BC_WIDTH
RMS_EPS = 1e-6
SUB_EPS = 1e-5
LOG2E = 1.4426950408889634
LANES = 128
SUBLANES = 8
ONES_ROWS = 16
BLOCKS_PER_ITER = 2
N_EARLY_CHAINS = 8
HEADS_PER_GROUP = SSM_HEADS // SSM_GROUPS
PAIR_WIDTH = 2 * SSM_HEAD_DIM
NEG_INF = float("-inf")

COL_Q, COL_K, COL_V = 0, ATT_WIDTH, 2 * ATT_WIDTH
COL_Z = 3 * ATT_WIDTH
COL_XS = COL_Z + SSM_WIDTH
COL_BC = COL_XS + SSM_WIDTH

VMEM_LIMIT = 56 * 1024 * 1024


def _params(semantics):
    return pltpu.CompilerParams(dimension_semantics=semantics, vmem_limit_bytes=VMEM_LIMIT)


def _sigmoid(x):
    return 1.0 / (1.0 + jnp.exp(-x))


def _rmsnorm(x, w, eps):
    ms = jnp.mean(x * x, axis=-1, keepdims=True)
    return (x * lax.rsqrt(ms + eps)) * w


def _lane_tile(x, width):
    return x if width == LANES else jnp.concatenate([x] * (width // LANES), axis=1)


_CONTRACT_LAST = (((1,), (1,)), ((), ()))


def _in_proj_kernel(x_ref, nw_ref, wt_ref, wdt_ref, cs_ref, proj_ref, dt_ref, h_ref):
    @pl.when(pl.program_id(1) == 0)
    def _():
        hb = _rmsnorm(x_ref[...], nw_ref[...], RMS_EPS).astype(BF16)
        h_ref[...] = hb
        dt_ref[...] = lax.dot_general(hb, wdt_ref[...], _CONTRACT_LAST,
                                      preferred_element_type=F32)

    acc = lax.dot_general(h_ref[...], wt_ref[...].astype(BF16), _CONTRACT_LAST,
                          preferred_element_type=F32)
    proj_ref[...] = (acc * cs_ref[...]).astype(BF16)


def _in_proj(x2d, norm_w, w_in_t, w_dt_t, col_scale, tm=1024, tn=1024):
    m = x2d.shape[0]
    return pl.pallas_call(
        _in_proj_kernel,
        grid=(m // tm, PROJ_MAIN // tn),
        in_specs=[
            pl.BlockSpec((tm, D_MODEL), lambda i, j: (i, 0)),
            pl.BlockSpec((1, D_MODEL), lambda i, j: (0, 0)),
            pl.BlockSpec((tn, D_MODEL), lambda i, j: (j, 0)),
            pl.BlockSpec((LANES, D_MODEL), lambda i, j: (0, 0)),
            pl.BlockSpec((1, tn), lambda i, j: (0, j)),
        ],
        out_specs=[
            pl.BlockSpec((tm, tn), lambda i, j: (i, j)),
            pl.BlockSpec((tm, LANES), lambda i, j: (i, 0)),
        ],
        out_shape=[
            jax.ShapeDtypeStruct((m, PROJ_MAIN), BF16),
            jax.ShapeDtypeStruct((m, LANES), F32),
        ],
        scratch_shapes=[pltpu.VMEM((tm, D_MODEL), BF16)],
        compiler_params=_params(("parallel", "arbitrary")),
        name="in_proj",
    )(x2d, norm_w, w_in_t, w_dt_t, col_scale)


def _attn_kernel(slope_ref, lamv_ref, subw_ref, q_ref, k_ref, v_ref, o_ref,
                 vt_scr, qt_scr, kaug_scr, st_scr, m_scr, acc_scr, *, tq, tk, qc, lambda_init):
    h = pl.program_id(1)
    qi = pl.program_id(2)
    n_kblocks = vt_scr.shape[0]
    slope2 = slope_ref[h]
    aug_lane = lax.broadcasted_iota(jnp.int32, (tk, ATT_QK_DIM), 1)

    @pl.when(qi == 0)
    def _():
        def transpose_block(jb, carry):
            vt = v_ref[pl.ds(pl.multiple_of(jb * tk, tk), tk), :].T
            vt_scr[jb] = jnp.concatenate([vt, jnp.ones((ONES_ROWS, tk), BF16)], axis=0)
            return carry
        lax.fori_loop(0, n_kblocks, transpose_block, 0)
        bias = slope2 * lax.broadcasted_iota(jnp.int32, (tk, ATT_QK_DIM), 0).astype(F32)
        hi = bias.astype(BF16).astype(F32)
        mid = (bias - hi).astype(BF16).astype(F32)
        lo = (bias - hi) - mid
        kaug_scr[...] = jnp.where(aug_lane == 0, hi, jnp.where(
            aug_lane == 1, mid, jnp.where(aug_lane == 2, lo, 0.0))).astype(BF16)

    qt = q_ref[...].T
    pick = jnp.where(lax.broadcasted_iota(jnp.int32, (ATT_QK_DIM, tq), 0) < 3, 1.0, 0.0).astype(BF16)
    for c in range(2):
        qt_scr[c] = jnp.concatenate([qt[c * ATT_QK_DIM:(c + 1) * ATT_QK_DIM, :], pick], axis=0)
    m_scr[...] = jnp.full(m_scr.shape, NEG_INF, F32)
    acc_scr[...] = jnp.zeros(acc_scr.shape, F32)
    row_minus_col = (lax.broadcasted_iota(jnp.int32, (tk, qc), 0)
                     - lax.broadcasted_iota(jnp.int32, (tk, qc), 1))

    def scores(j, c, u):
        k0 = j * tk if isinstance(j, int) else pl.multiple_of(j * tk, tk)
        dims = slice(c * ATT_QK_DIM, (c + 1) * ATT_QK_DIM)
        k_aug = jnp.concatenate([k_ref[pl.ds(k0, tk), dims], kaug_scr[...]], axis=1)
        return jnp.dot(k_aug, qt_scr[c, :, u * qc:(u + 1) * qc], preferred_element_type=F32)

    n_qchunks = tq // qc

    def block_chains(d=None):
        out = []
        for u in range(n_qchunks):
            if d is not None and d * tk >= (u + 1) * qc:
                continue
            thr = None if d is None or (d + 1) * tk <= u * qc else u * qc - d * tk
            out += [(u, c, thr) for c in range(2)]
        return out

    def prefetch_scores(j, chains):
        for slot, (u, c, _) in enumerate(chains[:N_EARLY_CHAINS]):
            st_scr[slot] = scores(j, c, u)

    def chain(j, c, u, thr, slot=None):
        cs = slice(u * qc, (u + 1) * qc)
        st = scores(j, c, u) if slot is None else st_scr[slot]
        if thr is not None:
            st = jnp.where(row_minus_col <= thr, st, NEG_INF)
        off = slope2 * jnp.asarray(j * tk, F32)
        m_old = m_scr[c, 0:1, cs]
        m_new = jnp.maximum(m_old, jnp.max(st, axis=0, keepdims=True) + off)
        alpha = jnp.exp2(m_old - m_new)
        pt = jnp.exp2(st - (m_new - off))
        m_scr[c, :, cs] = jnp.broadcast_to(m_new, (SUBLANES, qc))
        acc_scr[c, :, cs] = alpha * acc_scr[c, :, cs] + jnp.dot(
            vt_scr[j], pt.astype(BF16), preferred_element_type=F32)

    def run_block(j, chains, prefetched):
        for i, (u, c, thr) in enumerate(chains):
            chain(j, c, u, thr, i if prefetched and i < N_EARLY_CHAINS else None)

    def body(jj, carry):
        for i in range(BLOCKS_PER_ITER):
            run_block(BLOCKS_PER_ITER * jj + i, block_chains(), i == 0)
        prefetch_scores(BLOCKS_PER_ITER * (jj + 1), block_chains())
        return carry

    n_full = (qi * tq) // tk
    prefetch_scores(0, block_chains())
    lax.fori_loop(0, n_full // BLOCKS_PER_ITER, body, 0)
    for d in range(tq // tk):
        run_block(n_full + d, block_chains(d), prefetched=(d == 0))

    lv = lamv_ref[...]
    lam = (jnp.exp(jnp.sum(lv[0:1] * lv[1:2], axis=1, keepdims=True))
           - jnp.exp(jnp.sum(lv[2:3] * lv[3:4], axis=1, keepdims=True)) + lambda_init)
    inv_l0 = 1.0 / acc_scr[0, ATT_V_DIM:ATT_V_DIM + 1, :]
    inv_l1 = lam / acc_scr[1, ATT_V_DIM:ATT_V_DIM + 1, :]
    ot = acc_scr[0, :ATT_V_DIM, :] * inv_l0 - acc_scr[1, :ATT_V_DIM, :] * inv_l1
    ms = jnp.mean(ot * ot, axis=0, keepdims=True)
    ot = (ot * lax.rsqrt(ms + SUB_EPS)) * _lane_tile(subw_ref[...], tq)
    o_ref[...] = (ot * (1.0 - lambda_init)).astype(BF16).T


def _attention(proj, slopes2, lamv, subw_col, batch, seq, lambda_init, tq=1024, tk=256, qc=256):
    assert tq % (BLOCKS_PER_ITER * tk) == 0 and tq % qc == 0 and seq % tq == 0
    nq = seq // tq
    kern = functools.partial(_attn_kernel, tq=tq, tk=tk, qc=qc, lambda_init=lambda_init)
    kb, vb = COL_K // ATT_V_DIM, COL_V // ATT_V_DIM
    return pl.pallas_call(
        kern,
        grid=(batch, ATT_HEADS, nq),
        in_specs=[
            pl.BlockSpec(memory_space=pltpu.SMEM),
            pl.BlockSpec((4, ATT_QK_DIM), lambda b, h, i: (0, 0)),
            pl.BlockSpec((ATT_V_DIM, LANES), lambda b, h, i: (0, 0)),
            pl.BlockSpec((tq, ATT_V_DIM), lambda b, h, i: (b * nq + i, h)),
            pl.BlockSpec((seq, ATT_V_DIM), lambda b, h, i: (b, kb + h)),
            pl.BlockSpec((seq, ATT_V_DIM), lambda b, h, i: (b, vb + h)),
        ],
        out_specs=pl.BlockSpec((tq, ATT_V_DIM), lambda b, h, i: (b * nq + i, h)),
        out_shape=jax.ShapeDtypeStruct((batch * seq, ATT_WIDTH), BF16),
        scratch_shapes=[
            pltpu.VMEM((seq // tk, ATT_V_DIM + ONES_ROWS, tk), BF16),
            pltpu.VMEM((2, 2 * ATT_QK_DIM, tq), BF16),
            pltpu.VMEM((tk, ATT_QK_DIM), BF16),
            pltpu.VMEM((N_EARLY_CHAINS, tk, qc), F32),
            pltpu.VMEM((2, SUBLANES, tq), F32),
            pltpu.VMEM((2, ATT_V_DIM + ONES_ROWS, tq), F32),
        ],
        compiler_params=_params(("parallel", "parallel", "arbitrary")),
        name="diff_attention",
    )(slopes2, lamv, subw_col, proj, proj, proj)


def _ssd_kernel(z_ref, xs_ref, bc_ref, dt_ref, cwx_ref, cwbc_ref, cbx_ref, cbbc_ref,
                dtb_ref, alog_ref, dskip_ref, nw_ref, shift_ref, out_ref,
                tailx, tailbc, state, yscr):
    L = SSM_CHUNK

    @pl.when(pl.program_id(1) == 0)
    def _():
        tailx[...] = jnp.zeros(tailx.shape, BF16)
        tailbc[...] = jnp.zeros(tailbc.shape, BF16)
        state[...] = jnp.zeros(state.shape, F32)

    def conv_silu(u_ref, prev, w_ref, b_ref):
        u_b = u_ref[...]
        both = jnp.concatenate([prev[...], u_b], axis=0)
        wins = jnp.dot(shift_ref[...], both, preferred_element_type=F32)
        acc = b_ref[...] + w_ref[SSM_CONV - 1:SSM_CONV, :] * u_b.astype(F32)
        for k in range(SSM_CONV - 1):
            acc = acc + w_ref[k:k + 1, :] * wins[k * L:(k + 1) * L]
        prev[...] = u_b
        return acc * _sigmoid(acc)

    xs = conv_silu(xs_ref, tailx, cwx_ref, cbx_ref)
    bc = conv_silu(bc_ref, tailbc, cwbc_ref, cbbc_ref)
    xs_b = xs.astype(BF16)
    bc_b = bc.astype(BF16)

    dtr = dt_ref[...] + dtb_ref[...]
    dt = jnp.maximum(dtr, 0.0) + jnp.log1p(jnp.exp(-jnp.abs(dtr)))
    acs = dt * (-jnp.exp(alog_ref[...]))
    rowi = lax.broadcasted_iota(jnp.int32, (L, LANES), 0)
    shift = 1
    while shift < L:
        acs = acs + jnp.where(rowi >= shift, pltpu.roll(acs, shift, axis=0), 0.0)
        shift *= 2
    acs_t = acs.T
    dt_t = dt.T
    exp_a = jnp.exp(acs)
    w_t = dt_t * jnp.exp(acs_t[:, L - 1:L] - acs_t)
    cd_row = jnp.exp(acs[L - 1:L, :])

    li = lax.broadcasted_iota(jnp.int32, (L, L), 0)
    si = lax.broadcasted_iota(jnp.int32, (L, L), 1)
    causal = si <= li
    lane = lax.broadcasted_iota(jnp.int32, (L, PAIR_WIDTH), 1)
    lo_half = lane < SSM_HEAD_DIM
    lane_row = lax.broadcasted_iota(jnp.int32, (1, PAIR_WIDTH), 1) < SSM_HEAD_DIM
    zero_b = jnp.zeros((L, PAIR_WIDTH), BF16)

    gn = SSM_GROUPS * SSM_STATE
    for g in range(SSM_GROUPS):
        b_g = bc[:, g * SSM_STATE:(g + 1) * SSM_STATE]
        c_g = bc[:, gn + g * SSM_STATE:gn + (g + 1) * SSM_STATE]
        cb = lax.dot_general(bc_b[:, gn + g * SSM_STATE:gn + (g + 1) * SSM_STATE],
                             bc_b[:, g * SSM_STATE:(g + 1) * SSM_STATE],
                             (((1,), (1,)), ((), ())), preferred_element_type=F32)
        b_gt = b_g.T
        for pr in range(HEADS_PER_GROUP // 2):
            pair = g * (HEADS_PER_GROUP // 2) + pr
            lhs_y, lhs_s = [], []
            for hh in (2 * pair, 2 * pair + 1):
                seg = acs[:, hh:hh + 1] - acs_t[hh:hh + 1, :]
                decay = jnp.exp(jnp.where(causal, seg, NEG_INF))
                lhs_y.append((cb * decay * dt_t[hh:hh + 1, :]).astype(BF16))
                lhs_s.append((b_gt * w_t[hh:hh + 1, :]).astype(BF16))
            for hh in (2 * pair, 2 * pair + 1):
                lhs_y.append((c_g * exp_a[:, hh:hh + 1]).astype(BF16))
            x_p = xs_b[:, pair * PAIR_WIDTH:(pair + 1) * PAIR_WIDTH]
            x_lo = jnp.where(lo_half, x_p, zero_b)
            x_hi = jnp.where(lo_half, zero_b, x_p)
            st = state[pair]
            st_b = st.astype(BF16)
            st_lo = jnp.where(lo_half, st_b, zero_b)
            st_hi = jnp.where(lo_half, zero_b, st_b)
            y = jnp.dot(jnp.concatenate(lhs_y, axis=1),
                        jnp.concatenate([x_lo, x_hi, st_lo, st_hi], axis=0),
                        preferred_element_type=F32)
            yscr[:, pair * PAIR_WIDTH:(pair + 1) * PAIR_WIDTH] = y
            new = jnp.dot(jnp.concatenate(lhs_s, axis=1),
                          jnp.concatenate([x_lo, x_hi], axis=0),
                          preferred_element_type=F32)
            cd = jnp.where(lane_row, cd_row[:, 2 * pair:2 * pair + 1],
                           cd_row[:, 2 * pair + 1:2 * pair + 2])
            state[pair] = st * cd + new

    y = yscr[...] + dskip_ref[...] * xs
    z = z_ref[...].astype(F32)
    y = y * (z * _sigmoid(z))
    gw = SSM_WIDTH // SSM_GROUPS
    for g in range(SSM_GROUPS):
        cols = slice(g * gw, (g + 1) * gw)
        out_ref[:, cols] = _rmsnorm(y[:, cols], nw_ref[:, cols], SUB_EPS).astype(BF16)


def _ssd(proj, dt_raw, conv_w, conv_b, dt_bias, a_log, d_skip_vec, norm_w, batch, seq):
    nc = seq // SSM_CHUNK
    L = SSM_CHUNK
    row = lambda b, c: b * nc + c
    full = lambda shape: pl.BlockSpec(shape, lambda b, c: (0, 0))
    cwx, cwbc = conv_w[:, :SSM_WIDTH], conv_w[:, SSM_WIDTH:]
    cbx, cbbc = conv_b[:, :SSM_WIDTH], conv_b[:, SSM_WIDTH:]
    tap_row = jnp.arange((SSM_CONV - 1) * L)
    src_row = L - (SSM_CONV - 1) + tap_row // L + tap_row % L
    shift = (src_row[:, None] == jnp.arange(2 * L)[None, :]).astype(BF16)
    return pl.pallas_call(
        _ssd_kernel,
        grid=(batch, nc),
        in_specs=[
            pl.BlockSpec((L, SSM_WIDTH), lambda b, c: (row(b, c), COL_Z // SSM_WIDTH)),
            pl.BlockSpec((L, SSM_WIDTH), lambda b, c: (row(b, c), COL_XS // SSM_WIDTH)),
            pl.BlockSpec((L, SSM_BC_WIDTH), lambda b, c: (row(b, c), COL_BC // SSM_BC_WIDTH)),
            pl.BlockSpec((L, LANES), lambda b, c: (row(b, c), 0)),
            full((SSM_CONV, SSM_WIDTH)),
            full((SSM_CONV, SSM_BC_WIDTH)),
            full((1, SSM_WIDTH)),
            full((1, SSM_BC_WIDTH)),
            full((1, LANES)),
            full((1, LANES)),
            full((1, SSM_WIDTH)),
            full((1, SSM_WIDTH)),
            full(((SSM_CONV - 1) * L, 2 * L)),
        ],
        out_specs=pl.BlockSpec((L, SSM_WIDTH), lambda b, c: (row(b, c), 0)),
        out_shape=jax.ShapeDtypeStruct((batch * seq, SSM_WIDTH), BF16),
        scratch_shapes=[
            pltpu.VMEM((L, SSM_WIDTH), BF16),
            pltpu.VMEM((L, SSM_BC_WIDTH), BF16),
            pltpu.VMEM((SSM_HEADS // 2, SSM_STATE, PAIR_WIDTH), F32),
            pltpu.VMEM((L, SSM_WIDTH), F32),
        ],
        compiler_params=_params(("parallel", "arbitrary")),
        name="ssd_mixer",
    )(proj, proj, proj, dt_raw, cwx, cwbc, cbx, cbbc, dt_bias, a_log, d_skip_vec, norm_w, shift)


def _out_proj_kernel(att_ref, ssm_ref, w_ref, x_ref, x1_ref):
    acc = jnp.dot(att_ref[...], w_ref[0].astype(BF16), preferred_element_type=F32)
    acc = acc + jnp.dot(ssm_ref[...], w_ref[1].astype(BF16), preferred_element_type=F32)
    x1_ref[...] = x_ref[...] + acc


def _out_proj(att, ssm, w_out2, x2d, tm=1024, tn=512):
    m = att.shape[0]
    return pl.pallas_call(
        _out_proj_kernel,
        grid=(m // tm, D_MODEL // tn),
        in_specs=[
            pl.BlockSpec((tm, ATT_WIDTH), lambda i, j: (i, 0)),
            pl.BlockSpec((tm, SSM_WIDTH), lambda i, j: (i, 0)),
            pl.BlockSpec((2, ATT_WIDTH, tn), lambda i, j: (0, 0, j)),
            pl.BlockSpec((tm, tn), lambda i, j: (i, j)),
        ],
        out_specs=pl.BlockSpec((tm, tn), lambda i, j: (i, j)),
        out_shape=jax.ShapeDtypeStruct((m, D_MODEL), F32),
        compiler_params=_params(("parallel", "arbitrary")),
        name="out_proj",
    )(att, ssm, w_out2, x2d)


def _gate_up_kernel(x_ref, nw_ref, wg_ref, wu_ref, a_ref, h_ref):
    @pl.when(pl.program_id(1) == 0)
    def _():
        h_ref[...] = _rmsnorm(x_ref[...], nw_ref[...], RMS_EPS).astype(BF16)

    h = h_ref[...]
    g = jnp.dot(h, wg_ref[...].astype(BF16), preferred_element_type=F32)
    u = jnp.dot(h, wu_ref[...].astype(BF16), preferred_element_type=F32)
    a_ref[...] = ((g * _sigmoid(g)) * u).astype(BF16)


def _gate_up(x1, norm_w, w_gate, w_up, tm=1024, tn=512):
    m = x1.shape[0]
    n = w_gate.shape[1]
    return pl.pallas_call(
        _gate_up_kernel,
        grid=(m // tm, n // tn),
        in_specs=[
            pl.BlockSpec((tm, D_MODEL), lambda i, j: (i, 0)),
            pl.BlockSpec((1, D_MODEL), lambda i, j: (0, 0)),
            pl.BlockSpec((D_MODEL, tn), lambda i, j: (0, j)),
            pl.BlockSpec((D_MODEL, tn), lambda i, j: (0, j)),
        ],
        out_specs=pl.BlockSpec((tm, tn), lambda i, j: (i, j)),
        out_shape=jax.ShapeDtypeStruct((m, n), BF16),
        scratch_shapes=[pltpu.VMEM((tm, D_MODEL), BF16)],
        compiler_params=_params(("parallel", "arbitrary")),
        name="gate_up",
    )(x1, norm_w, w_gate, w_up)


def _down_kernel(a_ref, w_ref, x_ref, nw_ref, o_ref, x2_ref):
    j = pl.program_id(1)
    x2_ref[j] = x_ref[...] + jnp.dot(a_ref[...], w_ref[...], preferred_element_type=F32)

    @pl.when(j == pl.num_programs(1) - 1)
    def _():
        x2 = jnp.concatenate([x2_ref[t] for t in range(x2_ref.shape[0])], axis=1)
        o_ref[...] = _rmsnorm(x2, nw_ref[...], RMS_EPS)


def _down(a, w_down, x1, norm_w, tm=512, tn=1024):
    m, kdim = a.shape
    return pl.pallas_call(
        _down_kernel,
        grid=(m // tm, D_MODEL // tn),
        in_specs=[
            pl.BlockSpec((tm, kdim), lambda i, j: (i, 0)),
            pl.BlockSpec((kdim, tn), lambda i, j: (0, j)),
            pl.BlockSpec((tm, tn), lambda i, j: (i, j)),
            pl.BlockSpec((1, D_MODEL), lambda i, j: (0, 0)),
        ],
        out_specs=pl.BlockSpec((tm, D_MODEL), lambda i, j: (i, 0)),
        out_shape=jax.ShapeDtypeStruct((m, D_MODEL), F32),
        scratch_shapes=[pltpu.VMEM((D_MODEL // tn, tm, tn), F32)],
        compiler_params=_params(("parallel", "arbitrary")),
        name="down_proj",
    )(a, w_down, x1, norm_w)


def _alibi_slopes_log2(n):
    start = 2.0 ** (-8.0 / n)
    return jnp.asarray([start ** (i + 1) for i in range(n)], dtype=F32) * LOG2E


def _pad_lanes(v):
    return jnp.pad(v.astype(F32), (0, LANES - v.shape[0])).reshape(1, LANES)


def kernel(x, norm_mix_w, w_in, lambda_q1, lambda_k1, lambda_q2, lambda_k2, subln_w, conv_w, conv_b, dt_bias, a_log, d_skip, ssm_norm_w, w_out, norm_ffn_w, w_gate, w_up, w_down, norm_final_w):
    batch, seq, _ = x.shape
    assert w_in.shape[0] == 1, "single-layer block"
    layer = 0
    x2d = x.reshape(batch * seq, D_MODEL)
    col_scale = jnp.concatenate([
        jnp.full((ATT_WIDTH,), ATT_QK_DIM ** -0.5 * LOG2E, F32),
        jnp.ones((PROJ_MAIN - ATT_WIDTH,), F32)]).reshape(1, PROJ_MAIN)
    slopes2 = _alibi_slopes_log2(ATT_HEADS)
    lambda_init = 0.8 - 0.6 * math.exp(-0.3 * layer)
    w_in_t = w_in[layer].T
    w_dt_t = jnp.pad(w_in_t[PROJ_MAIN:], ((0, LANES - SSM_HEADS), (0, 0))).astype(BF16)
    proj, dt_raw = _in_proj(x2d, norm_mix_w[layer].reshape(1, D_MODEL), w_in_t, w_dt_t, col_scale)
    lamv = jnp.stack([lambda_q1[layer], lambda_k1[layer], lambda_q2[layer], lambda_k2[layer]]).astype(F32)
    subw_col = jnp.broadcast_to(subln_w[layer].astype(F32).reshape(ATT_V_DIM, 1), (ATT_V_DIM, LANES))
    att = _attention(proj, slopes2, lamv, subw_col, batch, seq, lambda_init)
    ssm = _ssd(proj, dt_raw, conv_w[layer], conv_b[layer].reshape(1, -1),
               _pad_lanes(dt_bias[layer]), _pad_lanes(a_log[layer]),
               jnp.repeat(d_skip[layer].astype(F32), SSM_HEAD_DIM).reshape(1, SSM_WIDTH),
               ssm_norm_w[layer].reshape(1, SSM_WIDTH), batch, seq)
    w_out2 = w_out[layer].reshape(2, ATT_WIDTH, D_MODEL)
    x1 = _out_proj(att, ssm, w_out2, x2d)
    a = _gate_up(x1, norm_ffn_w[layer].reshape(1, D_MODEL), w_gate[layer], w_up[layer])
    out = _down(a, w_down[layer].astype(BF16), x1, norm_final_w.reshape(1, D_MODEL))
    return out.reshape(batch, seq, D_MODEL)
```

```python
import functools
import math

import jax
import jax.numpy as jnp
from jax import lax
from jax.experimental import pallas as pl
from jax.experimental.pallas import tpu as pltpu

F32 = jnp.float32
BF16 = jnp.bfloat16

D_MODEL = 2048
ATT_HEADS = 8
ATT_QK_DIM = 128
ATT_V_DIM = 2 * ATT_QK_DIM
ATT_WIDTH = ATT_HEADS * ATT_V_DIM
SSM_HEADS = 32
SSM_HEAD_DIM = 64
SSM_WIDTH = SSM_HEADS * SSM_HEAD_DIM
SSM_GROUPS = 4
SSM_STATE = 128
SSM_CONV = 4
SSM_CHUNK = 128
SSM_BC_WIDTH = 2 * SSM_GROUPS * SSM_STATE
PROJ_MAIN = 2 * ATT_WIDTH + ATT_WIDTH + SSM_WIDTH + SSM_WIDTH + SSM_BC_WIDTH
RMS_EPS = 1e-6
SUB_EPS = 1e-5
LOG2E = 1.4426950408889634
LANES = 128
SUBLANES = 8
ONES_ROWS = 16
BLOCKS_PER_ITER = 2
N_EARLY_CHAINS = 8
HEADS_PER_GROUP = SSM_HEADS // SSM_GROUPS
PAIR_WIDTH = 2 * SSM_HEAD_DIM
NEG_INF = float("-inf")

COL_Q, COL_K, COL_V = 0, ATT_WIDTH, 2 * ATT_WIDTH
COL_Z = 3 * ATT_WIDTH
COL_XS = COL_Z + SSM_WIDTH
COL_BC = COL_XS + SSM_WIDTH

VMEM_LIMIT = 56 * 1024 * 1024


def _params(semantics):
    return pltpu.CompilerParams(dimension_semantics=semantics, vmem_limit_bytes=VMEM_LIMIT)


def _sigmoid(x):
    return 1.0 / (1.0 + jnp.exp(-x))


def _rmsnorm(x, w, eps):
    ms = jnp.mean(x * x, axis=-1, keepdims=True)
    return (x * lax.rsqrt(ms + eps)) * w


def _lane_tile(x, width):
    return x if width == LANES else jnp.concatenate([x] * (width // LANES), axis=1)


_CONTRACT_LAST = (((1,), (1,)), ((), ()))


def _in_proj_kernel(x_ref, nw_ref, wt_ref, wdt_ref, cs_ref, proj_ref, dt_ref, h_ref):
    def project(hb):
        acc = lax.dot_general(hb, wt_ref[...].astype(BF16), _CONTRACT_LAST,
                              preferred_element_type=F32)
        proj_ref[...] = (acc * cs_ref[...]).astype(BF16)

    @pl.when(pl.program_id(1) == 0)
    def _():
        hb = _rmsnorm(x_ref[...], nw_ref[...], RMS_EPS).astype(BF16)
        h_ref[...] = hb
        dt_ref[...] = lax.dot_general(hb, wdt_ref[...], _CONTRACT_LAST,
                                      preferred_element_type=F32)
        project(hb)

    @pl.when(pl.program_id(1) > 0)
    def _():
        project(h_ref[...])


def _in_proj(x2d, norm_w, w_in_t, w_dt_t, col_scale, tm=1024, tn=1024):
    m = x2d.shape[0]
    return pl.pallas_call(
        _in_proj_kernel,
        grid=(m // tm, PROJ_MAIN // tn),
        in_specs=[
            pl.BlockSpec((tm, D_MODEL), lambda i, j: (i, 0)),
            pl.BlockSpec((1, D_MODEL), lambda i, j: (0, 0)),
            pl.BlockSpec((tn, D_MODEL), lambda i, j: (j, 0)),
            pl.BlockSpec((LANES, D_MODEL), lambda i, j: (0, 0)),
            pl.BlockSpec((1, tn), lambda i, j: (0, j)),
        ],
        out_specs=[
            pl.BlockSpec((tm, tn), lambda i, j: (i, j)),
            pl.BlockSpec((tm, LANES), lambda i, j: (i, 0)),
        ],
        out_shape=[
            jax.ShapeDtypeStruct((m, PROJ_MAIN), BF16),
            jax.ShapeDtypeStruct((m, LANES), F32),
        ],
        scratch_shapes=[pltpu.VMEM((tm, D_MODEL), BF16)],
        compiler_params=_params(("parallel", "arbitrary")),
        name="in_proj",
    )(x2d, norm_w, w_in_t, w_dt_t, col_scale)


def _attn_kernel(slope_ref, lamv_ref, subw_ref, q_ref, k_ref, v_ref, o_ref,
                 vt_scr, qt_scr, kaug_scr, st_scr, m_scr, acc_scr, *, tq, tk, qc, lambda_init):
    h = pl.program_id(1)
    qi = pl.program_id(2)
    slope2 = slope_ref[h]
    aug_lane = lax.broadcasted_iota(jnp.int32, (tk, ATT_QK_DIM), 1)

    @pl.when(qi == 0)
    def _():
        bias = slope2 * lax.broadcasted_iota(jnp.int32, (tk, ATT_QK_DIM), 0).astype(F32)
        hi = bias.astype(BF16).astype(F32)
        mid = (bias - hi).astype(BF16).astype(F32)
        lo = (bias - hi) - mid
        kaug_scr[...] = jnp.where(aug_lane == 0, hi, jnp.where(
            aug_lane == 1, mid, jnp.where(aug_lane == 2, lo, 0.0))).astype(BF16)

    qt = q_ref[...].T
    pick = jnp.where(lax.broadcasted_iota(jnp.int32, (ATT_QK_DIM, tq), 0) < 3, 1.0, 0.0).astype(BF16)
    for c in range(2):
        qt_scr[c] = jnp.concatenate([qt[c * ATT_QK_DIM:(c + 1) * ATT_QK_DIM, :], pick], axis=0)
    m_scr[...] = jnp.full(m_scr.shape, NEG_INF, F32)
    acc_scr[...] = jnp.zeros(acc_scr.shape, F32)
    row_minus_col = (lax.broadcasted_iota(jnp.int32, (tk, qc), 0)
                     - lax.broadcasted_iota(jnp.int32, (tk, qc), 1))

    def scores(j, c, u):
        k0 = j * tk if isinstance(j, int) else pl.multiple_of(j * tk, tk)
        dims = slice(c * ATT_QK_DIM, (c + 1) * ATT_QK_DIM)
        k_aug = jnp.concatenate([k_ref[pl.ds(k0, tk), dims], kaug_scr[...]], axis=1)
        return jnp.dot(k_aug, qt_scr[c, :, u * qc:(u + 1) * qc], preferred_element_type=F32)

    n_qchunks = tq // qc

    def block_chains(d=None):
        out = []
        for u in range(n_qchunks):
            if d is not None and d * tk >= (u + 1) * qc:
                continue
            thr = None if d is None or (d + 1) * tk <= u * qc else u * qc - d * tk
            out += [(u, c, thr) for c in range(2)]
        return out

    def prefetch_scores(j, chains):
        for slot, (u, c, _) in enumerate(chains[:N_EARLY_CHAINS]):
            st_scr[slot] = scores(j, c, u)

    def chain(j, c, u, thr, slot=None):
        cs = slice(u * qc, (u + 1) * qc)
        st = scores(j, c, u) if slot is None else st_scr[slot]
        if thr is not None:
            st = jnp.where(row_minus_col <= thr, st, NEG_INF)
        off = slope2 * jnp.asarray(j * tk, F32)
        m_old = m_scr[c, 0:1, cs]
        m_new = jnp.maximum(m_old, jnp.max(st, axis=0, keepdims=True) + off)
        alpha = jnp.exp2(m_old - m_new)
        pt = jnp.exp2(st - (m_new - off))
        m_scr[c, :, cs] = jnp.broadcast_to(m_new, (SUBLANES, qc))
        acc_scr[c, :, cs] = alpha * acc_scr[c, :, cs] + jnp.dot(
            vt_scr[j], pt.astype(BF16), preferred_element_type=F32)

    def run_block(j, chains, prefetched):
        for i, (u, c, thr) in enumerate(chains):
            chain(j, c, u, thr, i if prefetched and i < N_EARLY_CHAINS else None)

    def body(jj, carry):
        for i in range(BLOCKS_PER_ITER):
            run_block(BLOCKS_PER_ITER * jj + i, block_chains(), i == 0)
        prefetch_scores(BLOCKS_PER_ITER * (jj + 1), block_chains())
        return carry

    n_full = (qi * tq) // tk
    prefetch_scores(0, block_chains())
    for d in range(tq // tk):
        vt = v_ref[pl.ds(pl.multiple_of((n_full + d) * tk, tk), tk), :].T
        vt_scr[n_full + d] = jnp.concatenate([vt, jnp.ones((ONES_ROWS, tk), BF16)], axis=0)
    lax.fori_loop(0, n_full // BLOCKS_PER_ITER, body, 0)
    for d in range(tq // tk):
        run_block(n_full + d, block_chains(d), prefetched=(d == 0))

    lv = lamv_ref[...]
    lam = (jnp.exp(jnp.sum(lv[0:1] * lv[1:2], axis=1, keepdims=True))
           - jnp.exp(jnp.sum(lv[2:3] * lv[3:4], axis=1, keepdims=True)) + lambda_init)
    inv_l0 = 1.0 / acc_scr[0, ATT_V_DIM:ATT_V_DIM + 1, :]
    inv_l1 = lam / acc_scr[1, ATT_V_DIM:ATT_V_DIM + 1, :]
    ot = acc_scr[0, :ATT_V_DIM, :] * inv_l0 - acc_scr[1, :ATT_V_DIM, :] * inv_l1
    ms = jnp.mean(ot * ot, axis=0, keepdims=True)
    ot = (ot * lax.rsqrt(ms + SUB_EPS)) * _lane_tile(subw_ref[...], tq)
    o_ref[...] = (ot * (1.0 - lambda_init)).astype(BF16).T


def _attention(proj, slopes2, lamv, subw_col, batch, seq, lambda_init, tq=1024, tk=256, qc=256):
    assert tq % (BLOCKS_PER_ITER * tk) == 0 and tq % qc == 0 and seq % tq == 0
    nq = seq // tq
    kern = functools.partial(_attn_kernel, tq=tq, tk=tk, qc=qc, lambda_init=lambda_init)
    kb, vb = COL_K // ATT_V_DIM, COL_V // ATT_V_DIM
    return pl.pallas_call(
        kern,
        grid=(batch, ATT_HEADS, nq),
        in_specs=[
            pl.BlockSpec(memory_space=pltpu.SMEM),
            pl.BlockSpec((4, ATT_QK_DIM), lambda b, h, i: (0, 0)),
            pl.BlockSpec((ATT_V_DIM, LANES), lambda b, h, i: (0, 0)),
            pl.BlockSpec((tq, ATT_V_DIM), lambda b, h, i: (b * nq + i, h)),
            pl.BlockSpec((seq, ATT_V_DIM), lambda b, h, i: (b, kb + h)),
            pl.BlockSpec((seq, ATT_V_DIM), lambda b, h, i: (b, vb + h)),
        ],
        out_specs=pl.BlockSpec((tq, ATT_V_DIM), lambda b, h, i: (b * nq + i, h)),
        out_shape=jax.ShapeDtypeStruct((batch * seq, ATT_WIDTH), BF16),
        scratch_shapes=[
            pltpu.VMEM((seq // tk, ATT_V_DIM + ONES_ROWS, tk), BF16),
            pltpu.VMEM((2, 2 * ATT_QK_DIM, tq), BF16),
            pltpu.VMEM((tk, ATT_QK_DIM), BF16),
            pltpu.VMEM((N_EARLY_CHAINS, tk, qc), F32),
            pltpu.VMEM((2, SUBLANES, tq), F32),
            pltpu.VMEM((2, ATT_V_DIM + ONES_ROWS, tq), F32),
        ],
        compiler_params=_params(("parallel", "parallel", "arbitrary")),
        name="diff_attention",
    )(slopes2, lamv, subw_col, proj, proj, proj)


def _ssd_kernel(z_ref, xs_ref, bc_ref, dt_ref, cwx_ref, cwbc_ref, cbx_ref, cbbc_ref,
                dtb_ref, alog_ref, dskip_ref, nw_ref, out_ref,
                tailx, tailbc, state, yscr):
    L = SSM_CHUNK

    @pl.when(pl.program_id(1) == 0)
    def _():
        tailx[...] = jnp.zeros(tailx.shape, F32)
        tailbc[...] = jnp.zeros(tailbc.shape, F32)
        state[...] = jnp.zeros(state.shape, F32)

    def conv_silu(u_ref, tail, w_ref, b_ref):
        u = u_ref[...].astype(F32)
        full = jnp.concatenate([tail[...], u], axis=0)
        acc = b_ref[...] + w_ref[SSM_CONV - 1:SSM_CONV, :] * u
        for k in range(SSM_CONV - 1):
            lo = SUBLANES - (SSM_CONV - 1) + k
            acc = acc + w_ref[k:k + 1, :] * full[lo:lo + L]
        tail[...] = u[L - SUBLANES:L]
        return acc * _sigmoid(acc)

    xs = conv_silu(xs_ref, tailx, cwx_ref, cbx_ref)
    bc = conv_silu(bc_ref, tailbc, cwbc_ref, cbbc_ref)
    xs_b = xs.astype(BF16)
    bc_b = bc.astype(BF16)

    dtr = dt_ref[...] + dtb_ref[...]
    dt = jnp.maximum(dtr, 0.0) + jnp.log1p(jnp.exp(-jnp.abs(dtr)))
    acs = dt * (-jnp.exp(alog_ref[...]))
    rowi = lax.broadcasted_iota(jnp.int32, (L, LANES), 0)
    shift = 1
    while shift < L:
        acs = acs + jnp.where(rowi >= shift, pltpu.roll(acs, shift, axis=0), 0.0)
        shift *= 2
    acs = acs * LOG2E
    acs_t = acs.T
    dt_t = dt.T
    exp_a = jnp.exp2(acs)
    w_t = dt_t * jnp.exp2(acs_t[:, L - 1:L] - acs_t)
    cd_row = jnp.exp2(acs[L - 1:L, :])
    src_t = acs_t - jnp.log2(dt_t)

    li = lax.broadcasted_iota(jnp.int32, (L, L), 0)
    si = lax.broadcasted_iota(jnp.int32, (L, L), 1)
    causal = si <= li
    lane = lax.broadcasted_iota(jnp.int32, (L, PAIR_WIDTH), 1)
    lo_half = lane < SSM_HEAD_DIM
    lane_row = lax.broadcasted_iota(jnp.int32, (1, PAIR_WIDTH), 1) < SSM_HEAD_DIM
    zero_b = jnp.zeros((L, PAIR_WIDTH), BF16)

    gn = SSM_GROUPS * SSM_STATE
    for g in range(SSM_GROUPS):
        b_g = bc[:, g * SSM_STATE:(g + 1) * SSM_STATE]
        c_g = bc[:, gn + g * SSM_STATE:gn + (g + 1) * SSM_STATE]
        cb = lax.dot_general(bc_b[:, gn + g * SSM_STATE:gn + (g + 1) * SSM_STATE],
                             bc_b[:, g * SSM_STATE:(g + 1) * SSM_STATE],
                             (((1,), (1,)), ((), ())), preferred_element_type=F32)
        b_gt = b_g.T
        for pr in range(HEADS_PER_GROUP // 2):
            pair = g * (HEADS_PER_GROUP // 2) + pr
            lhs_y, lhs_s = [], []
            for hh in (2 * pair, 2 * pair + 1):
                seg = acs[:, hh:hh + 1] - src_t[hh:hh + 1, :]
                lhs_y.append((cb * jnp.exp2(jnp.where(causal, seg, NEG_INF))).astype(BF16))
                lhs_s.append((b_gt * w_t[hh:hh + 1, :]).astype(BF16))
            for hh in (2 * pair, 2 * pair + 1):
                lhs_y.append((c_g * exp_a[:, hh:hh + 1]).astype(BF16))
            x_p = xs_b[:, pair * PAIR_WIDTH:(pair + 1) * PAIR_WIDTH]
            x_lo = jnp.where(lo_half, x_p, zero_b)
            x_hi = jnp.where(lo_half, zero_b, x_p)
            st = state[pair]
            st_b = st.astype(BF16)
            st_lo = jnp.where(lo_half, st_b, zero_b)
            st_hi = jnp.where(lo_half, zero_b, st_b)
            y = jnp.dot(jnp.concatenate(lhs_y, axis=1),
                        jnp.concatenate([x_lo, x_hi, st_lo, st_hi], axis=0),
                        preferred_element_type=F32)
            yscr[:, pair * PAIR_WIDTH:(pair + 1) * PAIR_WIDTH] = y
            new = jnp.dot(jnp.concatenate(lhs_s, axis=1),
                          jnp.concatenate([x_lo, x_hi], axis=0),
                          preferred_element_type=F32)
            cd = jnp.where(lane_row, cd_row[:, 2 * pair:2 * pair + 1],
                           cd_row[:, 2 * pair + 1:2 * pair + 2])
            state[pair] = st * cd + new

    y = yscr[...] + dskip_ref[...] * xs
    z = z_ref[...].astype(F32)
    y = y * (z * _sigmoid(z))
    gw = SSM_WIDTH // SSM_GROUPS
    for g in range(SSM_GROUPS):
        cols = slice(g * gw, (g + 1) * gw)
        out_ref[:, cols] = _rmsnorm(y[:, cols], nw_ref[:, cols], SUB_EPS).astype(BF16)


def _ssd(proj, dt_raw, conv_w, conv_b, dt_bias, a_log, d_skip_vec, norm_w, batch, seq):
    nc = seq // SSM_CHUNK
    L = SSM_CHUNK
    row = lambda b, c: b * nc + c
    full = lambda shape: pl.BlockSpec(shape, lambda b, c: (0, 0))
    cwx, cwbc = conv_w[:, :SSM_WIDTH], conv_w[:, SSM_WIDTH:]
    cbx, cbbc = conv_b[:, :SSM_WIDTH], conv_b[:, SSM_WIDTH:]
    return pl.pallas_call(
        _ssd_kernel,
        grid=(batch, nc),
        in_specs=[
            pl.BlockSpec((L, SSM_WIDTH), lambda b, c: (row(b, c), COL_Z // SSM_WIDTH)),
            pl.BlockSpec((L, SSM_WIDTH), lambda b, c: (row(b, c), COL_XS // SSM_WIDTH)),
            pl.BlockSpec((L, SSM_BC_WIDTH), lambda b, c: (row(b, c), COL_BC // SSM_BC_WIDTH)),
            pl.BlockSpec((L, LANES), lambda b, c: (row(b, c), 0)),
            full((SSM_CONV, SSM_WIDTH)),
            full((SSM_CONV, SSM_BC_WIDTH)),
            full((1, SSM_WIDTH)),
            full((1, SSM_BC_WIDTH)),
            full((1, LANES)),
            full((1, LANES)),
            full((1, SSM_WIDTH)),
            full((1, SSM_WIDTH)),
        ],
        out_specs=pl.BlockSpec((L, SSM_WIDTH), lambda b, c: (row(b, c), 0)),
        out_shape=jax.ShapeDtypeStruct((batch * seq, SSM_WIDTH), BF16),
        scratch_shapes=[
            pltpu.VMEM((SUBLANES, SSM_WIDTH), F32),
            pltpu.VMEM((SUBLANES, SSM_BC_WIDTH), F32),
            pltpu.VMEM((SSM_HEADS // 2, SSM_STATE, PAIR_WIDTH), F32),
            pltpu.VMEM((L, SSM_WIDTH), F32),
        ],
        compiler_params=_params(("parallel", "arbitrary")),
        name="ssd_mixer",
    )(proj, proj, proj, dt_raw, cwx, cwbc, cbx, cbbc, dt_bias, a_log, d_skip_vec, norm_w)


def _out_proj_kernel(att_ref, ssm_ref, w_ref, x_ref, x1_ref, wb_ref):
    @pl.when(pl.program_id(1) == 0)
    def _():
        wb_ref[...] = w_ref[...].astype(BF16)

    acc = jnp.dot(att_ref[...], wb_ref[0], preferred_element_type=F32)
    acc = acc + jnp.dot(ssm_ref[...], wb_ref[1], preferred_element_type=F32)
    x1_ref[...] = x_ref[...] + acc


def _out_proj(att, ssm, w_out2, x2d, tm=512, tn=1024):
    m = att.shape[0]
    return pl.pallas_call(
        _out_proj_kernel,
        grid=(D_MODEL // tn, m // tm),
        in_specs=[
            pl.BlockSpec((tm, ATT_WIDTH), lambda j, i: (i, 0)),
            pl.BlockSpec((tm, SSM_WIDTH), lambda j, i: (i, 0)),
            pl.BlockSpec((2, ATT_WIDTH, tn), lambda j, i: (0, 0, j), pipeline_mode=pl.Buffered(1)),
            pl.BlockSpec((tm, tn), lambda j, i: (i, j)),
        ],
        out_specs=pl.BlockSpec((tm, tn), lambda j, i: (i, j)),
        out_shape=jax.ShapeDtypeStruct((m, D_MODEL), F32),
        scratch_shapes=[pltpu.VMEM((2, ATT_WIDTH, tn), BF16)],
        compiler_params=_params(("parallel", "arbitrary")),
        name="out_proj",
    )(att, ssm, w_out2, x2d)


def _gate_up_kernel(x_ref, nw_ref, wg_ref, wu_ref, a_ref, h_ref):
    def gated(h):
        g = jnp.dot(h, wg_ref[...].astype(BF16), preferred_element_type=F32)
        u = jnp.dot(h, wu_ref[...].astype(BF16), preferred_element_type=F32)
        a_ref[...] = ((g * _sigmoid(g)) * u).astype(BF16)

    @pl.when(pl.program_id(1) == 0)
    def _():
        h = _rmsnorm(x_ref[...], nw_ref[...], RMS_EPS).astype(BF16)
        h_ref[...] = h
        gated(h)

    @pl.when(pl.program_id(1) > 0)
    def _():
        gated(h_ref[...])


def _gate_up(x1, norm_w, w_gate, w_up, tm=1024, tn=512):
    m = x1.shape[0]
    n = w_gate.shape[1]
    return pl.pallas_call(
        _gate_up_kernel,
        grid=(m // tm, n // tn),
        in_specs=[
            pl.BlockSpec((tm, D_MODEL), lambda i, j: (i, 0)),
            pl.BlockSpec((1, D_MODEL), lambda i, j: (0, 0)),
            pl.BlockSpec((D_MODEL, tn), lambda i, j: (0, j)),
            pl.BlockSpec((D_MODEL, tn), lambda i, j: (0, j)),
        ],
        out_specs=pl.BlockSpec((tm, tn), lambda i, j: (i, j)),
        out_shape=jax.ShapeDtypeStruct((m, n), BF16),
        scratch_shapes=[pltpu.VMEM((tm, D_MODEL), BF16)],
        compiler_params=_params(("parallel", "arbitrary")),
        name="gate_up",
    )(x1, norm_w, w_gate, w_up)


def _down_kernel(a_ref, w_ref, x_ref, nw_ref, o_ref, x2_ref):
    j = pl.program_id(1)
    x2_ref[j] = x_ref[...] + jnp.dot(a_ref[...], w_ref[...], preferred_element_type=F32)

    @pl.when(j == pl.num_programs(1) - 1)
    def _():
        x2 = jnp.concatenate([x2_ref[t] for t in range(x2_ref.shape[0])], axis=1)
        o_ref[...] = _rmsnorm(x2, nw_ref[...], RMS_EPS)


def _down(a, w_down, x1, norm_w, tm=512, tn=1024):
    m, kdim = a.shape
    return pl.pallas_call(
        _down_kernel,
        grid=(m // tm, D_MODEL // tn),
        in_specs=[
            pl.BlockSpec((tm, kdim), lambda i, j: (i, 0)),
            pl.BlockSpec((kdim, tn), lambda i, j: (0, j)),
            pl.BlockSpec((tm, tn), lambda i, j: (i, j)),
            pl.BlockSpec((1, D_MODEL), lambda i, j: (0, 0)),
        ],
        out_specs=pl.BlockSpec((tm, D_MODEL), lambda i, j: (i, 0)),
        out_shape=jax.ShapeDtypeStruct((m, D_MODEL), F32),
        scratch_shapes=[pltpu.VMEM((D_MODEL // tn, tm, tn), F32)],
        compiler_params=_params(("parallel", "arbitrary")),
        name="down_proj",
    )(a, w_down, x1, norm_w)


def _alibi_slopes_log2(n):
    start = 2.0 ** (-8.0 / n)
    return jnp.asarray([start ** (i + 1) for i in range(n)], dtype=F32) * LOG2E


def _pad_lanes(v):
    return jnp.pad(v.astype(F32), (0, LANES - v.shape[0])).reshape(1, LANES)


def kernel(x, norm_mix_w, w_in, lambda_q1, lambda_k1, lambda_q2, lambda_k2, subln_w, conv_w, conv_b, dt_bias, a_log, d_skip, ssm_norm_w, w_out, norm_ffn_w, w_gate, w_up, w_down, norm_final_w):
    batch, seq, _ = x.shape
    assert w_in.shape[0] == 1, "single-layer block"
    layer = 0
    x2d = x.reshape(batch * seq, D_MODEL)
    col_scale = jnp.concatenate([
        jnp.full((ATT_WIDTH,), ATT_QK_DIM ** -0.5 * LOG2E, F32),
        jnp.ones((PROJ_MAIN - ATT_WIDTH,), F32)]).reshape(1, PROJ_MAIN)
    slopes2 = _alibi_slopes_log2(ATT_HEADS)
    lambda_init = 0.8 - 0.6 * math.exp(-0.3 * layer)
    w_in_t = w_in[layer].T
    w_dt_t = jnp.pad(w_in_t[PROJ_MAIN:], ((0, LANES - SSM_HEADS), (0, 0))).astype(BF16)
    proj, dt_raw = _in_proj(x2d, norm_mix_w[layer].reshape(1, D_MODEL), w_in_t, w_dt_t, col_scale)
    lamv = jnp.stack([lambda_q1[layer], lambda_k1[layer], lambda_q2[layer], lambda_k2[layer]]).astype(F32)
    subw_col = jnp.broadcast_to(subln_w[layer].astype(F32).reshape(ATT_V_DIM, 1), (ATT_V_DIM, LANES))
    att = _attention(proj, slopes2, lamv, subw_col, batch, seq, lambda_init)
    ssm = _ssd(proj, dt_raw, conv_w[layer], conv_b[layer].reshape(1, -1),
               _pad_lanes(dt_bias[layer]), _pad_lanes(a_log[layer]),
               jnp.repeat(d_skip[layer].astype(F32), SSM_HEAD_DIM).reshape(1, SSM_WIDTH),
               ssm_norm_w[layer].reshape(1, SSM_WIDTH), batch, seq)
    w_out2 = w_out[layer].reshape(2, ATT_WIDTH, D_MODEL)
    x1 = _out_proj(att, ssm, w_out2, x2d)
    a = _gate_up(x1, norm_ffn_w[layer].reshape(1, D_MODEL), w_gate[layer], w_up[layer])
    out = _down(a, w_down[layer].astype(BF16), x1, norm_final_w.reshape(1, D_MODEL))
    return out.reshape(batch, seq, D_MODEL)
```

```python
import functools
import math

import jax
import jax.numpy as jnp
from jax import lax
from jax.experimental import pallas as pl
from jax.experimental.pallas import tpu as pltpu

F32 = jnp.float32
BF16 = jnp.bfloat16

D_MODEL = 2048
ATT_HEADS = 8
ATT_QK_DIM = 128
ATT_V_DIM = 2 * ATT_QK_DIM
ATT_WIDTH = ATT_HEADS * ATT_V_DIM
SSM_HEADS = 32
SSM_HEAD_DIM = 64
SSM_WIDTH = SSM_HEADS * SSM_HEAD_DIM
SSM_GROUPS = 4
SSM_STATE = 128
SSM_CONV = 4
SSM_CHUNK = 128
SSM_BC_WIDTH = 2 * SSM_GROUPS * SSM_STATE
PROJ_MAIN = 2 * ATT_WIDTH + ATT_WIDTH + SSM_WIDTH + SSM_WIDTH + SSM_BC_WIDTH
RMS_EPS = 1e-6
SUB_EPS = 1e-5
LOG2E = 1.4426950408889634
LANES = 128
SUBLANES = 8
ONES_ROWS = 16
BLOCKS_PER_ITER = 2
N_EARLY_CHAINS = 8
HEADS_PER_GROUP = SSM_HEADS // SSM_GROUPS
PAIR_WIDTH = 2 * SSM_HEAD_DIM
NEG_INF = float("-inf")

COL_Q, COL_K, COL_V = 0, ATT_WIDTH, 2 * ATT_WIDTH
COL_Z = 3 * ATT_WIDTH
COL_XS = COL_Z + SSM_WIDTH
COL_BC = COL_XS + SSM_WIDTH

VMEM_LIMIT = 56 * 1024 * 1024


def _params(semantics):
    return pltpu.CompilerParams(dimension_semantics=semantics, vmem_limit_bytes=VMEM_LIMIT)


def _sigmoid(x):
    return 1.0 / (1.0 + jnp.exp(-x))


def _rmsnorm(x, w, eps):
    ms = jnp.mean(x * x, axis=-1, keepdims=True)
    return (x * lax.rsqrt(ms + eps)) * w


def _lane_tile(x, width):
    return x if width == LANES else jnp.concatenate([x] * (width // LANES), axis=1)


_CONTRACT_LAST = (((1,), (1,)), ((), ()))


def _in_proj_kernel(x_ref, nw_ref, wt_ref, wdt_ref, cs_ref, proj_ref, dt_ref, h_ref):
    def project(hb):
        acc = lax.dot_general(hb, wt_ref[...].astype(BF16), _CONTRACT_LAST,
                              preferred_element_type=F32)
        proj_ref[...] = (acc * cs_ref[...]).astype(BF16)

    @pl.when(pl.program_id(1) == 0)
    def _():
        hb = _rmsnorm(x_ref[...], nw_ref[...], RMS_EPS).astype(BF16)
        h_ref[...] = hb
        dt_ref[...] = lax.dot_general(hb, wdt_ref[...], _CONTRACT_LAST,
                                      preferred_element_type=F32)
        project(hb)

    @pl.when(pl.program_id(1) > 0)
    def _():
        project(h_ref[...])


def _in_proj(x2d, norm_w, w_in_t, w_dt_t, col_scale, tm=1024, tn=1024):
    m = x2d.shape[0]
    return pl.pallas_call(
        _in_proj_kernel,
        grid=(m // tm, PROJ_MAIN // tn),
        in_specs=[
            pl.BlockSpec((tm, D_MODEL), lambda i, j: (i, 0)),
            pl.BlockSpec((1, D_MODEL), lambda i, j: (0, 0)),
            pl.BlockSpec((tn, D_MODEL), lambda i, j: (j, 0)),
            pl.BlockSpec((LANES, D_MODEL), lambda i, j: (0, 0)),
            pl.BlockSpec((1, tn), lambda i, j: (0, j)),
        ],
        out_specs=[
            pl.BlockSpec((tm, tn), lambda i, j: (i, j)),
            pl.BlockSpec((tm, LANES), lambda i, j: (i, 0)),
        ],
        out_shape=[
            jax.ShapeDtypeStruct((m, PROJ_MAIN), BF16),
            jax.ShapeDtypeStruct((m, LANES), F32),
        ],
        scratch_shapes=[pltpu.VMEM((tm, D_MODEL), BF16)],
        compiler_params=_params(("parallel", "arbitrary")),
        name="in_proj",
    )(x2d, norm_w, w_in_t, w_dt_t, col_scale)


def _attn_kernel(slope_ref, lamv_ref, subw_ref, q_ref, k_ref, v_ref, o_ref,
                 vt_scr, qt_scr, kaug_scr, st_scr, m_scr, acc_scr, *, tq, tk, qc, lambda_init):
    h = pl.program_id(1)
    qi = pl.program_id(2)
    slope2 = slope_ref[h]
    aug_lane = lax.broadcasted_iota(jnp.int32, (tk, ATT_QK_DIM), 1)

    @pl.when(qi == 0)
    def _():
        bias = slope2 * lax.broadcasted_iota(jnp.int32, (tk, ATT_QK_DIM), 0).astype(F32)
        hi = bias.astype(BF16).astype(F32)
        mid = (bias - hi).astype(BF16).astype(F32)
        lo = (bias - hi) - mid
        kaug_scr[...] = jnp.where(aug_lane == 0, hi, jnp.where(
            aug_lane == 1, mid, jnp.where(aug_lane == 2, lo, 0.0))).astype(BF16)

    qt = q_ref[...].T
    pick = jnp.where(lax.broadcasted_iota(jnp.int32, (ATT_QK_DIM, tq), 0) < 3, 1.0, 0.0).astype(BF16)
    for c in range(2):
        qt_scr[c] = jnp.concatenate([qt[c * ATT_QK_DIM:(c + 1) * ATT_QK_DIM, :], pick], axis=0)
    m_scr[...] = jnp.full(m_scr.shape, NEG_INF, F32)
    acc_scr[...] = jnp.zeros(acc_scr.shape, F32)
    row_minus_col = (lax.broadcasted_iota(jnp.int32, (tk, qc), 0)
                     - lax.broadcasted_iota(jnp.int32, (tk, qc), 1))

    def scores(j, c, u):
        k0 = j * tk if isinstance(j, int) else pl.multiple_of(j * tk, tk)
        dims = slice(c * ATT_QK_DIM, (c + 1) * ATT_QK_DIM)
        k_aug = jnp.concatenate([k_ref[pl.ds(k0, tk), dims], kaug_scr[...]], axis=1)
        return jnp.dot(k_aug, qt_scr[c, :, u * qc:(u + 1) * qc], preferred_element_type=F32)

    n_qchunks = tq // qc

    def block_chains(d=None):
        out = []
        for u in range(n_qchunks):
            if d is not None and d * tk >= (u + 1) * qc:
                continue
            thr = None if d is None or (d + 1) * tk <= u * qc else u * qc - d * tk
            out += [(u, c, thr) for c in range(2)]
        return out

    def prefetch_scores(j, chains):
        for slot, (u, c, _) in enumerate(chains[:N_EARLY_CHAINS]):
            st_scr[slot] = scores(j, c, u)

    def chain(j, c, u, thr, slot=None):
        cs = slice(u * qc, (u + 1) * qc)
        st = scores(j, c, u) if slot is None else st_scr[slot]
        if thr is not None:
            st = jnp.where(row_minus_col <= thr, st, NEG_INF)
        off = slope2 * jnp.asarray(j * tk, F32)
        m_old = m_scr[c, 0:1, cs]
        m_new = jnp.maximum(m_old, jnp.max(st, axis=0, keepdims=True) + off)
        alpha = jnp.exp2(m_old - m_new)
        pt = jnp.exp2(st - (m_new - off))
        m_scr[c, :, cs] = jnp.broadcast_to(m_new, (SUBLANES, qc))
        acc_scr[c, :, cs] = alpha * acc_scr[c, :, cs] + jnp.dot(
            vt_scr[j], pt.astype(BF16), preferred_element_type=F32)

    def run_block(j, chains, prefetched):
        for i, (u, c, thr) in enumerate(chains):
            chain(j, c, u, thr, i if prefetched and i < N_EARLY_CHAINS else None)

    def body(jj, carry):
        for i in range(BLOCKS_PER_ITER):
            run_block(BLOCKS_PER_ITER * jj + i, block_chains(), i == 0)
        prefetch_scores(BLOCKS_PER_ITER * (jj + 1), block_chains())
        return carry

    n_full = (qi * tq) // tk
    prefetch_scores(0, block_chains())
    for d in range(tq // tk):
        vt = v_ref[pl.ds(pl.multiple_of((n_full + d) * tk, tk), tk), :].T
        vt_scr[n_full + d] = jnp.concatenate([vt, jnp.ones((ONES_ROWS, tk), BF16)], axis=0)
    lax.fori_loop(0, n_full // BLOCKS_PER_ITER, body, 0)
    for d in range(tq // tk):
        run_block(n_full + d, block_chains(d), prefetched=(d == 0))

    lv = lamv_ref[...]
    lam = (jnp.exp(jnp.sum(lv[0:1] * lv[1:2], axis=1, keepdims=True))
           - jnp.exp(jnp.sum(lv[2:3] * lv[3:4], axis=1, keepdims=True)) + lambda_init)
    inv_l0 = 1.0 / acc_scr[0, ATT_V_DIM:ATT_V_DIM + 1, :]
    inv_l1 = lam / acc_scr[1, ATT_V_DIM:ATT_V_DIM + 1, :]
    ot = acc_scr[0, :ATT_V_DIM, :] * inv_l0 - acc_scr[1, :ATT_V_DIM, :] * inv_l1
    ms = jnp.mean(ot * ot, axis=0, keepdims=True)
    ot = (ot * lax.rsqrt(ms + SUB_EPS)) * _lane_tile(subw_ref[...], tq)
    o_ref[...] = (ot * (1.0 - lambda_init)).astype(BF16).T


def _attention(proj, slopes2, lamv, subw_col, batch, seq, lambda_init, tq=1024, tk=256, qc=256):
    assert tq % (BLOCKS_PER_ITER * tk) == 0 and tq % qc == 0 and seq % tq == 0
    nq = seq // tq
    kern = functools.partial(_attn_kernel, tq=tq, tk=tk, qc=qc, lambda_init=lambda_init)
    kb, vb = COL_K // ATT_V_DIM, COL_V // ATT_V_DIM
    return pl.pallas_call(
        kern,
        grid=(batch, ATT_HEADS, nq),
        in_specs=[
            pl.BlockSpec(memory_space=pltpu.SMEM),
            pl.BlockSpec((4, ATT_QK_DIM), lambda b, h, i: (0, 0)),
            pl.BlockSpec((ATT_V_DIM, LANES), lambda b, h, i: (0, 0)),
            pl.BlockSpec((tq, ATT_V_DIM), lambda b, h, i: (b * nq + i, h)),
            pl.BlockSpec((seq, ATT_V_DIM), lambda b, h, i: (b, kb + h)),
            pl.BlockSpec((seq, ATT_V_DIM), lambda b, h, i: (b, vb + h)),
        ],
        out_specs=pl.BlockSpec((tq, ATT_V_DIM), lambda b, h, i: (b * nq + i, h)),
        out_shape=jax.ShapeDtypeStruct((batch * seq, ATT_WIDTH), BF16),
        scratch_shapes=[
            pltpu.VMEM((seq // tk, ATT_V_DIM + ONES_ROWS, tk), BF16),
            pltpu.VMEM((2, 2 * ATT_QK_DIM, tq), BF16),
            pltpu.VMEM((tk, ATT_QK_DIM), BF16),
            pltpu.VMEM((N_EARLY_CHAINS, tk, qc), F32),
            pltpu.VMEM((2, SUBLANES, tq), F32),
            pltpu.VMEM((2, ATT_V_DIM + ONES_ROWS, tq), F32),
        ],
        compiler_params=_params(("parallel", "parallel", "arbitrary")),
        name="diff_attention",
    )(slopes2, lamv, subw_col, proj, proj, proj)


def _ssd_kernel(z_ref, xs_ref, bc_ref, dt_ref, cwx_ref, cwbc_ref, cbx_ref, cbbc_ref,
                dtb_ref, alog_ref, dskip_ref, nw_ref, out_ref,
                tailx, tailbc, state, yscr):
    L = SSM_CHUNK

    @pl.when(pl.program_id(1) == 0)
    def _():
        tailx[...] = jnp.zeros(tailx.shape, F32)
        tailbc[...] = jnp.zeros(tailbc.shape, F32)
        state[...] = jnp.zeros(state.shape, F32)

    def conv_silu(u_ref, tail, w_ref, b_ref):
        u = u_ref[...].astype(F32)
        full = jnp.concatenate([tail[...], u], axis=0)
        acc = b_ref[...] + w_ref[SSM_CONV - 1:SSM_CONV, :] * u
        for k in range(SSM_CONV - 1):
            lo = SUBLANES - (SSM_CONV - 1) + k
            acc = acc + w_ref[k:k + 1, :] * full[lo:lo + L]
        tail[...] = u[L - SUBLANES:L]
        return acc * _sigmoid(acc)

    xs = conv_silu(xs_ref, tailx, cwx_ref, cbx_ref)
    bc = conv_silu(bc_ref, tailbc, cwbc_ref, cbbc_ref)
    xs_b = xs.astype(BF16)
    bc_b = bc.astype(BF16)

    dtr = dt_ref[...] + dtb_ref[...]
    dt = jnp.maximum(dtr, 0.0) + jnp.log1p(jnp.exp(-jnp.abs(dtr)))
    acs = dt * (-jnp.exp(alog_ref[...]))
    rowi = lax.broadcasted_iota(jnp.int32, (L, LANES), 0)
    shift = 1
    while shift < L:
        acs = acs + jnp.where(rowi >= shift, pltpu.roll(acs, shift, axis=0), 0.0)
        shift *= 2
    acs = acs * LOG2E
    acs_t = acs.T
    dt_t = dt.T
    exp_a = jnp.exp2(acs)
    w_t = dt_t * jnp.exp2(acs_t[:, L - 1:L] - acs_t)
    cd_row = jnp.exp2(acs[L - 1:L, :])
    src_t = acs_t - jnp.log2(dt_t)

    li = lax.broadcasted_iota(jnp.int32, (L, L), 0)
    si = lax.broadcasted_iota(jnp.int32, (L, L), 1)
    causal = si <= li
    lane = lax.broadcasted_iota(jnp.int32, (L, PAIR_WIDTH), 1)
    lo_half = lane < SSM_HEAD_DIM
    lane_row = lax.broadcasted_iota(jnp.int32, (1, PAIR_WIDTH), 1) < SSM_HEAD_DIM
    zero_b = jnp.zeros((L, PAIR_WIDTH), BF16)

    gn = SSM_GROUPS * SSM_STATE
    for g in range(SSM_GROUPS):
        b_g = bc[:, g * SSM_STATE:(g + 1) * SSM_STATE]
        c_g = bc[:, gn + g * SSM_STATE:gn + (g + 1) * SSM_STATE]
        cb = lax.dot_general(bc_b[:, gn + g * SSM_STATE:gn + (g + 1) * SSM_STATE],
                             bc_b[:, g * SSM_STATE:(g + 1) * SSM_STATE],
                             (((1,), (1,)), ((), ())), preferred_element_type=F32)
        b_gt = b_g.T
        for pr in range(HEADS_PER_GROUP // 2):
            pair = g * (HEADS_PER_GROUP // 2) + pr
            lhs_y, lhs_s = [], []
            for hh in (2 * pair, 2 * pair + 1):
                seg = acs[:, hh:hh + 1] - src_t[hh:hh + 1, :]
                lhs_y.append((cb * jnp.exp2(jnp.where(causal, seg, NEG_INF))).astype(BF16))
                lhs_s.append((b_gt * w_t[hh:hh + 1, :]).astype(BF16))
            for hh in (2 * pair, 2 * pair + 1):
                lhs_y.append((c_g * exp_a[:, hh:hh + 1]).astype(BF16))
            x_p = xs_b[:, pair * PAIR_WIDTH:(pair + 1) * PAIR_WIDTH]
            x_lo = jnp.where(lo_half, x_p, zero_b)
            x_hi = jnp.where(lo_half, zero_b, x_p)
            st = state[pair]
            st_b = st.astype(BF16)
            st_lo = jnp.where(lo_half, st_b, zero_b)
            st_hi = jnp.where(lo_half, zero_b, st_b)
            y = jnp.dot(jnp.concatenate(lhs_y, axis=1),
                        jnp.concatenate([x_lo, x_hi, st_lo, st_hi], axis=0),
                        preferred_element_type=F32)
            yscr[:, pair * PAIR_WIDTH:(pair + 1) * PAIR_WIDTH] = y
            new = jnp.dot(jnp.concatenate(lhs_s, axis=1),
                          jnp.concatenate([x_lo, x_hi], axis=0),
                          preferred_element_type=F32)
            cd = jnp.where(lane_row, cd_row[:, 2 * pair:2 * pair + 1],
                           cd_row[:, 2 * pair + 1:2 * pair + 2])
            state[pair] = st * cd + new

    y = yscr[...] + dskip_ref[...] * xs
    z = z_ref[...].astype(F32)
    y = y * (z * _sigmoid(z))
    gw = SSM_WIDTH // SSM_GROUPS
    for g in range(SSM_GROUPS):
        cols = slice(g * gw, (g + 1) * gw)
        out_ref[:, cols] = _rmsnorm(y[:, cols], nw_ref[:, cols], SUB_EPS).astype(BF16)


def _ssd(proj, dt_raw, conv_w, conv_b, dt_bias, a_log, d_skip_vec, norm_w, batch, seq):
    nc = seq // SSM_CHUNK
    L = SSM_CHUNK
    row = lambda b, c: b * nc + c
    full = lambda shape: pl.BlockSpec(shape, lambda b, c: (0, 0))
    cwx, cwbc = conv_w[:, :SSM_WIDTH], conv_w[:, SSM_WIDTH:]
    cbx, cbbc = conv_b[:, :SSM_WIDTH], conv_b[:, SSM_WIDTH:]
    return pl.pallas_call(
        _ssd_kernel,
        grid=(batch, nc),
        in_specs=[
            pl.BlockSpec((L, SSM_WIDTH), lambda b, c: (row(b, c), COL_Z // SSM_WIDTH)),
            pl.BlockSpec((L, SSM_WIDTH), lambda b, c: (row(b, c), COL_XS // SSM_WIDTH)),
            pl.BlockSpec((L, SSM_BC_WIDTH), lambda b, c: (row(b, c), COL_BC // SSM_BC_WIDTH)),
            pl.BlockSpec((L, LANES), lambda b, c: (row(b, c), 0)),
            full((SSM_CONV, SSM_WIDTH)),
            full((SSM_CONV, SSM_BC_WIDTH)),
            full((1, SSM_WIDTH)),
            full((1, SSM_BC_WIDTH)),
            full((1, LANES)),
            full((1, LANES)),
            full((1, SSM_WIDTH)),
            full((1, SSM_WIDTH)),
        ],
        out_specs=pl.BlockSpec((L, SSM_WIDTH), lambda b, c: (row(b, c), 0)),
        out_shape=jax.ShapeDtypeStruct((batch * seq, SSM_WIDTH), BF16),
        scratch_shapes=[
            pltpu.VMEM((SUBLANES, SSM_WIDTH), F32),
            pltpu.VMEM((SUBLANES, SSM_BC_WIDTH), F32),
            pltpu.VMEM((SSM_HEADS // 2, SSM_STATE, PAIR_WIDTH), F32),
            pltpu.VMEM((L, SSM_WIDTH), F32),
        ],
        compiler_params=_params(("parallel", "arbitrary")),
        name="ssd_mixer",
    )(proj, proj, proj, dt_raw, cwx, cwbc, cbx, cbbc, dt_bias, a_log, d_skip_vec, norm_w)


def _out_proj_kernel(att_ref, ssm_ref, w_ref, x_ref, wd_ref, x1_ref, wdb_ref, wb_ref):
    @pl.when(pl.program_id(1) == 0)
    def _():
        wb_ref[...] = w_ref[...].astype(BF16)

    acc = jnp.dot(att_ref[...], wb_ref[0], preferred_element_type=F32)
    acc = acc + jnp.dot(ssm_ref[...], wb_ref[1], preferred_element_type=F32)
    x1_ref[...] = x_ref[...] + acc
    wdb_ref[...] = wd_ref[...].astype(BF16)


def _out_proj(att, ssm, w_out2, x2d, w_down, tm=512, tn=1024):
    m = att.shape[0]
    ni = m // tm
    n_steps = (D_MODEL // tn) * ni
    d_ff = w_down.shape[0]
    wd_rows = d_ff // n_steps
    assert wd_rows * n_steps == d_ff and wd_rows % ONES_ROWS == 0
    return pl.pallas_call(
        _out_proj_kernel,
        grid=(D_MODEL // tn, ni),
        in_specs=[
            pl.BlockSpec((tm, ATT_WIDTH), lambda j, i: (i, 0)),
            pl.BlockSpec((tm, SSM_WIDTH), lambda j, i: (i, 0)),
            pl.BlockSpec((2, ATT_WIDTH, tn), lambda j, i: (0, 0, j), pipeline_mode=pl.Buffered(1)),
            pl.BlockSpec((tm, tn), lambda j, i: (i, j)),
            pl.BlockSpec((wd_rows, D_MODEL), lambda j, i: (j * ni + i, 0)),
        ],
        out_specs=[
            pl.BlockSpec((tm, tn), lambda j, i: (i, j)),
            pl.BlockSpec((wd_rows, D_MODEL), lambda j, i: (j * ni + i, 0)),
        ],
        out_shape=[
            jax.ShapeDtypeStruct((m, D_MODEL), F32),
            jax.ShapeDtypeStruct(w_down.shape, BF16),
        ],
        scratch_shapes=[pltpu.VMEM((2, ATT_WIDTH, tn), BF16)],
        compiler_params=_params(("parallel", "arbitrary")),
        name="out_proj",
    )(att, ssm, w_out2, x2d, w_down)


def _gate_up_kernel(x_ref, nw_ref, wg_ref, wu_ref, a_ref, h_ref):
    def gated(h):
        g = jnp.dot(h, wg_ref[...].astype(BF16), preferred_element_type=F32)
        u = jnp.dot(h, wu_ref[...].astype(BF16), preferred_element_type=F32)
        a_ref[...] = ((g * _sigmoid(g)) * u).astype(BF16)

    @pl.when(pl.program_id(1) == 0)
    def _():
        h = _rmsnorm(x_ref[...], nw_ref[...], RMS_EPS).astype(BF16)
        h_ref[...] = h
        gated(h)

    @pl.when(pl.program_id(1) > 0)
    def _():
        gated(h_ref[...])


def _gate_up(x1, norm_w, w_gate, w_up, tm=1024, tn=512):
    m = x1.shape[0]
    n = w_gate.shape[1]
    return pl.pallas_call(
        _gate_up_kernel,
        grid=(m // tm, n // tn),
        in_specs=[
            pl.BlockSpec((tm, D_MODEL), lambda i, j: (i, 0)),
            pl.BlockSpec((1, D_MODEL), lambda i, j: (0, 0)),
            pl.BlockSpec((D_MODEL, tn), lambda i, j: (0, j)),
            pl.BlockSpec((D_MODEL, tn), lambda i, j: (0, j)),
        ],
        out_specs=pl.BlockSpec((tm, tn), lambda i, j: (i, j)),
        out_shape=jax.ShapeDtypeStruct((m, n), BF16),
        scratch_shapes=[pltpu.VMEM((tm, D_MODEL), BF16)],
        compiler_params=_params(("parallel", "arbitrary")),
        name="gate_up",
    )(x1, norm_w, w_gate, w_up)


def _down_kernel(a_ref, w_ref, x_ref, nw_ref, o_ref, x2_ref):
    j = pl.program_id(1)
    x2_ref[j] = x_ref[...] + jnp.dot(a_ref[...], w_ref[...], preferred_element_type=F32)

    @pl.when(j == pl.num_programs(1) - 1)
    def _():
        x2 = jnp.concatenate([x2_ref[t] for t in range(x2_ref.shape[0])], axis=1)
        o_ref[...] = _rmsnorm(x2, nw_ref[...], RMS_EPS)


def _down(a, w_down, x1, norm_w, tm=512, tn=1024):
    m, kdim = a.shape
    return pl.pallas_call(
        _down_kernel,
        grid=(m // tm, D_MODEL // tn),
        in_specs=[
            pl.BlockSpec((tm, kdim), lambda i, j: (i, 0)),
            pl.BlockSpec((kdim, tn), lambda i, j: (0, j)),
            pl.BlockSpec((tm, tn), lambda i, j: (i, j)),
            pl.BlockSpec((1, D_MODEL), lambda i, j: (0, 0)),
        ],
        out_specs=pl.BlockSpec((tm, D_MODEL), lambda i, j: (i, 0)),
        out_shape=jax.ShapeDtypeStruct((m, D_MODEL), F32),
        scratch_shapes=[pltpu.VMEM((D_MODEL // tn, tm, tn), F32)],
        compiler_params=_params(("parallel", "arbitrary")),
        name="down_proj",
    )(a, w_down, x1, norm_w)


def _alibi_slopes_log2(n):
    start = 2.0 ** (-8.0 / n)
    return jnp.asarray([start ** (i + 1) for i in range(n)], dtype=F32) * LOG2E


def _pad_lanes(v):
    return jnp.pad(v.astype(F32), (0, LANES - v.shape[0])).reshape(1, LANES)


def kernel(x, norm_mix_w, w_in, lambda_q1, lambda_k1, lambda_q2, lambda_k2, subln_w, conv_w, conv_b, dt_bias, a_log, d_skip, ssm_norm_w, w_out, norm_ffn_w, w_gate, w_up, w_down, norm_final_w):
    batch, seq, _ = x.shape
    assert w_in.shape[0] == 1, "single-layer block"
    layer = 0
    x2d = x.reshape(batch * seq, D_MODEL)
    col_scale = jnp.concatenate([
        jnp.full((ATT_WIDTH,), ATT_QK_DIM ** -0.5 * LOG2E, F32),
        jnp.ones((PROJ_MAIN - ATT_WIDTH,), F32)]).reshape(1, PROJ_MAIN)
    slopes2 = _alibi_slopes_log2(ATT_HEADS)
    lambda_init = 0.8 - 0.6 * math.exp(-0.3 * layer)
    w_in_t = w_in[layer].T
    w_dt_t = jnp.pad(w_in_t[PROJ_MAIN:], ((0, LANES - SSM_HEADS), (0, 0))).astype(BF16)
    proj, dt_raw = _in_proj(x2d, norm_mix_w[layer].reshape(1, D_MODEL), w_in_t, w_dt_t, col_scale)
    lamv = jnp.stack([lambda_q1[layer], lambda_k1[layer], lambda_q2[layer], lambda_k2[layer]]).astype(F32)
    subw_col = jnp.broadcast_to(subln_w[layer].astype(F32).reshape(ATT_V_DIM, 1), (ATT_V_DIM, LANES))
    att = _attention(proj, slopes2, lamv, subw_col, batch, seq, lambda_init)
    ssm = _ssd(proj, dt_raw, conv_w[layer], conv_b[layer].reshape(1, -1),
               _pad_lanes(dt_bias[layer]), _pad_lanes(a_log[layer]),
               jnp.repeat(d_skip[layer].astype(F32), SSM_HEAD_DIM).reshape(1, SSM_WIDTH),
               ssm_norm_w[layer].reshape(1, SSM_WIDTH), batch, seq)
    w_out2 = w_out[layer].reshape(2, ATT_WIDTH, D_MODEL)
    x1, w_down_b = _out_proj(att, ssm, w_out2, x2d, w_down[layer])
    a = _gate_up(x1, norm_ffn_w[layer].reshape(1, D_MODEL), w_gate[layer], w_up[layer])
    out = _down(a, w_down_b, x1, norm_final_w.reshape(1, D_MODEL))
    return out.reshape(batch, seq, D_MODEL)
```

```python
import functools
import math

import jax
import jax.numpy as jnp
from jax import lax
from jax.experimental import pallas as pl
from jax.experimental.pallas import tpu as pltpu

F32 = jnp.float32
BF16 = jnp.bfloat16

D_MODEL = 2048
ATT_HEADS = 8
ATT_QK_DIM = 128
ATT_V_DIM = 2 * ATT_QK_DIM
ATT_WIDTH = ATT_HEADS * ATT_V_DIM
SSM_HEADS = 32
SSM_HEAD_DIM = 64
SSM_WIDTH = SSM_HEADS * SSM_HEAD_DIM
SSM_GROUPS = 4
SSM_STATE = 128
SSM_CONV = 4
SSM_CHUNK = 128
SSM_BC_WIDTH = 2 * SSM_GROUPS * SSM_STATE
PROJ_MAIN = 2 * ATT_WIDTH + ATT_WIDTH + SSM_WIDTH + SSM_WIDTH + SSM_BC_WIDTH
RMS_EPS = 1e-6
SUB_EPS = 1e-5
LOG2E = 1.4426950408889634
LANES = 128
SUBLANES = 8
ONES_ROWS = 16
BLOCKS_PER_ITER = 2
N_EARLY_CHAINS = 8
HEADS_PER_GROUP = SSM_HEADS // SSM_GROUPS
PAIR_WIDTH = 2 * SSM_HEAD_DIM
NEG_INF = float("-inf")

COL_Q, COL_K, COL_V = 0, ATT_WIDTH, 2 * ATT_WIDTH
COL_Z = 3 * ATT_WIDTH
COL_XS = COL_Z + SSM_WIDTH
COL_BC = COL_XS + SSM_WIDTH

VMEM_LIMIT = 56 * 1024 * 1024


def _params(semantics):
    return pltpu.CompilerParams(dimension_semantics=semantics, vmem_limit_bytes=VMEM_LIMIT)


def _sigmoid(x):
    return 1.0 / (1.0 + jnp.exp(-x))


def _rmsnorm(x, w, eps):
    ms = jnp.mean(x * x, axis=-1, keepdims=True)
    return (x * lax.rsqrt(ms + eps)) * w


def _lane_tile(x, width):
    return x if width == LANES else jnp.concatenate([x] * (width // LANES), axis=1)


_CONTRACT_LAST = (((1,), (1,)), ((), ()))


def _in_proj_kernel(x_ref, nw_ref, wt_ref, wdt_ref, cs_ref, proj_ref, dt_ref, h_ref):
    def project(hb):
        acc = lax.dot_general(hb, wt_ref[...].astype(BF16), _CONTRACT_LAST,
                              preferred_element_type=F32)
        proj_ref[...] = (acc * cs_ref[...]).astype(BF16)

    @pl.when(pl.program_id(1) == 0)
    def _():
        hb = _rmsnorm(x_ref[...], nw_ref[...], RMS_EPS).astype(BF16)
        h_ref[...] = hb
        dt_ref[...] = lax.dot_general(hb, wdt_ref[...], _CONTRACT_LAST,
                                      preferred_element_type=F32)
        project(hb)

    @pl.when(pl.program_id(1) > 0)
    def _():
        project(h_ref[...])


def _in_proj(x2d, norm_w, w_in_t, w_dt_t, col_scale, tm=1024, tn=1024):
    m = x2d.shape[0]
    return pl.pallas_call(
        _in_proj_kernel,
        grid=(m // tm, PROJ_MAIN // tn),
        in_specs=[
            pl.BlockSpec((tm, D_MODEL), lambda i, j: (i, 0)),
            pl.BlockSpec((1, D_MODEL), lambda i, j: (0, 0)),
            pl.BlockSpec((tn, D_MODEL), lambda i, j: (j, 0)),
            pl.BlockSpec((LANES, D_MODEL), lambda i, j: (0, 0)),
            pl.BlockSpec((1, tn), lambda i, j: (0, j)),
        ],
        out_specs=[
            pl.BlockSpec((tm, tn), lambda i, j: (i, j)),
            pl.BlockSpec((tm, LANES), lambda i, j: (i, 0)),
        ],
        out_shape=[
            jax.ShapeDtypeStruct((m, PROJ_MAIN), BF16),
            jax.ShapeDtypeStruct((m, LANES), F32),
        ],
        scratch_shapes=[pltpu.VMEM((tm, D_MODEL), BF16)],
        compiler_params=_params(("parallel", "arbitrary")),
        name="in_proj",
    )(x2d, norm_w, w_in_t, w_dt_t, col_scale)


def _attn_kernel(slope_ref, lamv_ref, subw_ref, q_ref, k_ref, v_ref, o_ref,
                  vt_scr, qt_scr, kaug_scr, st_scr, m_scr, acc_scr,
                  *, tq, tk, qc, tiles, lambda_init):
    h = pl.program_id(1)
    step = pl.program_id(2)
    slope2 = slope_ref[h]
    aug_lane = lax.broadcasted_iota(jnp.int32, (tk, ATT_QK_DIM), 1)

    @pl.when(step == 0)
    def _():
        bias = slope2 * lax.broadcasted_iota(jnp.int32, (tk, ATT_QK_DIM), 0).astype(F32)
        hi = bias.astype(BF16).astype(F32)
        mid = (bias - hi).astype(BF16).astype(F32)
        lo = (bias - hi) - mid
        kaug_scr[...] = jnp.where(aug_lane == 0, hi, jnp.where(
            aug_lane == 1, mid, jnp.where(aug_lane == 2, lo, 0.0))).astype(BF16)

    row_minus_col = (lax.broadcasted_iota(jnp.int32, (tk, qc), 0)
                     - lax.broadcasted_iota(jnp.int32, (tk, qc), 1))
    pick = jnp.where(lax.broadcasted_iota(jnp.int32, (ATT_QK_DIM, tq), 0) < 3, 1.0, 0.0).astype(BF16)
    n_qchunks = tq // qc
    lv = lamv_ref[...]
    lam = (jnp.exp(jnp.sum(lv[0:1] * lv[1:2], axis=1, keepdims=True))
           - jnp.exp(jnp.sum(lv[2:3] * lv[3:4], axis=1, keepdims=True)) + lambda_init)

    def block_chains(d=None):
        out = []
        for u in range(n_qchunks):
            if d is not None and d * tk >= (u + 1) * qc:
                continue
            thr = None if d is None or (d + 1) * tk <= u * qc else u * qc - d * tk
            out += [(u, c, thr) for c in range(2)]
        return out

    def q_tile(t):
        rows = slice(t * tq, (t + 1) * tq)
        qi = step * tiles + t

        qt = q_ref[rows, :].T
        for c in range(2):
            qt_scr[t, c] = jnp.concatenate([qt[c * ATT_QK_DIM:(c + 1) * ATT_QK_DIM, :], pick], axis=0)
        m_scr[t] = jnp.full(m_scr.shape[1:], NEG_INF, F32)
        acc_scr[t] = jnp.zeros(acc_scr.shape[1:], F32)

        def scores(j, c, u):
            k0 = j * tk if isinstance(j, int) else pl.multiple_of(j * tk, tk)
            dims = slice(c * ATT_QK_DIM, (c + 1) * ATT_QK_DIM)
            k_aug = jnp.concatenate([k_ref[pl.ds(k0, tk), dims], kaug_scr[...]], axis=1)
            return jnp.dot(k_aug, qt_scr[t, c, :, u * qc:(u + 1) * qc], preferred_element_type=F32)

        def prefetch_scores(j, chains):
            for slot, (u, c, _) in enumerate(chains[:N_EARLY_CHAINS]):
                st_scr[slot] = scores(j, c, u)

        def chain(j, c, u, thr, slot=None):
            cs = slice(u * qc, (u + 1) * qc)
            st = scores(j, c, u) if slot is None else st_scr[slot]
            if thr is not None:
                st = jnp.where(row_minus_col <= thr, st, NEG_INF)
            off = slope2 * jnp.asarray(j * tk, F32)
            m_old = m_scr[t, c, 0:1, cs]
            m_new = jnp.maximum(m_old, jnp.max(st, axis=0, keepdims=True) + off)
            alpha = jnp.exp2(m_old - m_new)
            pt = jnp.exp2(st - (m_new - off))
            m_scr[t, c, :, cs] = jnp.broadcast_to(m_new, (SUBLANES, qc))
            acc_scr[t, c, :, cs] = alpha * acc_scr[t, c, :, cs] + jnp.dot(
                vt_scr[j], pt.astype(BF16), preferred_element_type=F32)

        def run_block(j, chains, prefetched):
            for i, (u, c, thr) in enumerate(chains):
                chain(j, c, u, thr, i if prefetched and i < N_EARLY_CHAINS else None)

        def body(jj, carry):
            for i in range(BLOCKS_PER_ITER):
                run_block(BLOCKS_PER_ITER * jj + i, block_chains(), i == 0)
            prefetch_scores(BLOCKS_PER_ITER * (jj + 1), block_chains())
            return carry

        n_full = (qi * tq) // tk
        prefetch_scores(0, block_chains())
        for d in range(tq // tk):
            vt = v_ref[pl.ds(pl.multiple_of((n_full + d) * tk, tk), tk), :].T
            vt_scr[n_full + d] = jnp.concatenate([vt, jnp.ones((ONES_ROWS, tk), BF16)], axis=0)
        lax.fori_loop(0, n_full // BLOCKS_PER_ITER, body, 0)
        for d in range(tq // tk):
            run_block(n_full + d, block_chains(d), prefetched=(d == 0))

        inv_l0 = 1.0 / acc_scr[t, 0, ATT_V_DIM:ATT_V_DIM + 1, :]
        inv_l1 = lam / acc_scr[t, 1, ATT_V_DIM:ATT_V_DIM + 1, :]
        ot = (acc_scr[t, 0, :ATT_V_DIM, :] * inv_l0
              - acc_scr[t, 1, :ATT_V_DIM, :] * inv_l1)
        ms = jnp.mean(ot * ot, axis=0, keepdims=True)
        ot = (ot * lax.rsqrt(ms + SUB_EPS)) * _lane_tile(subw_ref[...], tq)
        o_ref[rows, :] = (ot * (1.0 - lambda_init)).astype(BF16).T

    for t in range(tiles):
        q_tile(t)


def _attention(proj, slopes2, lamv, subw_col, batch, seq, lambda_init,
               tq=1024, tk=256, qc=256, tiles=4):
    assert tq % (BLOCKS_PER_ITER * tk) == 0 and tq % qc == 0 and seq % (tiles * tq) == 0
    nsteps = seq // (tiles * tq)
    kern = functools.partial(_attn_kernel, tq=tq, tk=tk, qc=qc, tiles=tiles,
                             lambda_init=lambda_init)
    kb, vb = COL_K // ATT_V_DIM, COL_V // ATT_V_DIM
    return pl.pallas_call(
        kern,
        grid=(batch, ATT_HEADS, nsteps),
        in_specs=[
            pl.BlockSpec(memory_space=pltpu.SMEM),
            pl.BlockSpec((4, ATT_QK_DIM), lambda b, h, i: (0, 0)),
            pl.BlockSpec((ATT_V_DIM, LANES), lambda b, h, i: (0, 0)),
            pl.BlockSpec((tiles * tq, ATT_V_DIM), lambda b, h, i: (b * nsteps + i, h)),
            pl.BlockSpec((seq, ATT_V_DIM), lambda b, h, i: (b, kb + h)),
            pl.BlockSpec((seq, ATT_V_DIM), lambda b, h, i: (b, vb + h)),
        ],
        out_specs=pl.BlockSpec((tiles * tq, ATT_V_DIM), lambda b, h, i: (b * nsteps + i, h)),
        out_shape=jax.ShapeDtypeStruct((batch * seq, ATT_WIDTH), BF16),
        scratch_shapes=[
            pltpu.VMEM((seq // tk, ATT_V_DIM + ONES_ROWS, tk), BF16),
            pltpu.VMEM((tiles, 2, 2 * ATT_QK_DIM, tq), BF16),
            pltpu.VMEM((tk, ATT_QK_DIM), BF16),
            pltpu.VMEM((N_EARLY_CHAINS, tk, qc), F32),
            pltpu.VMEM((tiles, 2, SUBLANES, tq), F32),
            pltpu.VMEM((tiles, 2, ATT_V_DIM + ONES_ROWS, tq), F32),
        ],
        compiler_params=_params(("parallel", "parallel", "arbitrary")),
        name="diff_attention",
    )(slopes2, lamv, subw_col, proj, proj, proj)


def _ssd_kernel(z_ref, xs_ref, bc_ref, dt_ref, cwx_ref, cwbc_ref, cbx_ref, cbbc_ref,
                dtb_ref, alog_ref, dskip_ref, nw_ref, out_ref,
                tailx, tailbc, state, yscr):
    L = SSM_CHUNK

    @pl.when(pl.program_id(1) == 0)
    def _():
        tailx[...] = jnp.zeros(tailx.shape, F32)
        tailbc[...] = jnp.zeros(tailbc.shape, F32)
        state[...] = jnp.zeros(state.shape, F32)

    def conv_silu(u_ref, tail, w_ref, b_ref):
        u = u_ref[...].astype(F32)
        full = jnp.concatenate([tail[...], u], axis=0)
        acc = b_ref[...] + w_ref[SSM_CONV - 1:SSM_CONV, :] * u
        for k in range(SSM_CONV - 1):
            lo = SUBLANES - (SSM_CONV - 1) + k
            acc = acc + w_ref[k:k + 1, :] * full[lo:lo + L]
        tail[...] = u[L - SUBLANES:L]
        return acc * _sigmoid(acc)

    xs = conv_silu(xs_ref, tailx, cwx_ref, cbx_ref)
    bc = conv_silu(bc_ref, tailbc, cwbc_ref, cbbc_ref)
    xs_b = xs.astype(BF16)
    bc_b = bc.astype(BF16)

    dtr = dt_ref[...] + dtb_ref[...]
    dt = jnp.maximum(dtr, 0.0) + jnp.log1p(jnp.exp(-jnp.abs(dtr)))
    acs = dt * (-jnp.exp(alog_ref[...]))
    rowi = lax.broadcasted_iota(jnp.int32, (L, LANES), 0)
    shift = 1
    while shift < L:
        acs = acs + jnp.where(rowi >= shift, pltpu.roll(acs, shift, axis=0), 0.0)
        shift *= 2
    acs = acs * LOG2E
    acs_t = acs.T
    dt_t = dt.T
    exp_a = jnp.exp2(acs)
    w_t = dt_t * jnp.exp2(acs_t[:, L - 1:L] - acs_t)
    cd_row = jnp.exp2(acs[L - 1:L, :])
    src_t = acs_t - jnp.log2(dt_t)

    li = lax.broadcasted_iota(jnp.int32, (L, L), 0)
    si = lax.broadcasted_iota(jnp.int32, (L, L), 1)
    causal = si <= li
    lane = lax.broadcasted_iota(jnp.int32, (L, PAIR_WIDTH), 1)
    lo_half = lane < SSM_HEAD_DIM
    lane_row = lax.broadcasted_iota(jnp.int32, (1, PAIR_WIDTH), 1) < SSM_HEAD_DIM
    zero_b = jnp.zeros((L, PAIR_WIDTH), BF16)

    gn = SSM_GROUPS * SSM_STATE
    for g in range(SSM_GROUPS):
        b_g = bc[:, g * SSM_STATE:(g + 1) * SSM_STATE]
        c_g = bc[:, gn + g * SSM_STATE:gn + (g + 1) * SSM_STATE]
        cb = lax.dot_general(bc_b[:, gn + g * SSM_STATE:gn + (g + 1) * SSM_STATE],
                             bc_b[:, g * SSM_STATE:(g + 1) * SSM_STATE],
                             (((1,), (1,)), ((), ())), preferred_element_type=F32)
        b_gt = b_g.T
        for pr in range(HEADS_PER_GROUP // 2):
            pair = g * (HEADS_PER_GROUP // 2) + pr
            lhs_y, lhs_s = [], []
            for hh in (2 * pair, 2 * pair + 1):
                seg = acs[:, hh:hh + 1] - src_t[hh:hh + 1, :]
                lhs_y.append((cb * jnp.exp2(jnp.where(causal, seg, NEG_INF))).astype(BF16))
                lhs_s.append((b_gt * w_t[hh:hh + 1, :]).astype(BF16))
            for hh in (2 * pair, 2 * pair + 1):
                lhs_y.append((c_g * exp_a[:, hh:hh + 1]).astype(BF16))
            x_p = xs_b[:, pair * PAIR_WIDTH:(pair + 1) * PAIR_WIDTH]
            x_lo = jnp.where(lo_half, x_p, zero_b)
            x_hi = jnp.where(lo_half, zero_b, x_p)
            st = state[pair]
            st_b = st.astype(BF16)
            st_lo = jnp.where(lo_half, st_b, zero_b)
            st_hi = jnp.where(lo_half, zero_b, st_b)
            y = jnp.dot(jnp.concatenate(lhs_y, axis=1),
                        jnp.concatenate([x_lo, x_hi, st_lo, st_hi], axis=0),
                        preferred_element_type=F32)
            yscr[:, pair * PAIR_WIDTH:(pair + 1) * PAIR_WIDTH] = y
            new = jnp.dot(jnp.concatenate(lhs_s, axis=1),
                          jnp.concatenate([x_lo, x_hi], axis=0),
                          preferred_element_type=F32)
            cd = jnp.where(lane_row, cd_row[:, 2 * pair:2 * pair + 1],
                           cd_row[:, 2 * pair + 1:2 * pair + 2])
            state[pair] = st * cd + new

    y = yscr[...] + dskip_ref[...] * xs
    z = z_ref[...].astype(F32)
    y = y * (z * _sigmoid(z))
    gw = SSM_WIDTH // SSM_GROUPS
    for g in range(SSM_GROUPS):
        cols = slice(g * gw, (g + 1) * gw)
        out_ref[:, cols] = _rmsnorm(y[:, cols], nw_ref[:, cols], SUB_EPS).astype(BF16)


def _ssd(proj, dt_raw, conv_w, conv_b, dt_bias, a_log, d_skip_vec, norm_w, batch, seq):
    nc = seq // SSM_CHUNK
    L = SSM_CHUNK
    row = lambda b, c: b * nc + c
    full = lambda shape: pl.BlockSpec(shape, lambda b, c: (0, 0))
    cwx, cwbc = conv_w[:, :SSM_WIDTH], conv_w[:, SSM_WIDTH:]
    cbx, cbbc = conv_b[:, :SSM_WIDTH], conv_b[:, SSM_WIDTH:]
    return pl.pallas_call(
        _ssd_kernel,
        grid=(batch, nc),
        in_specs=[
            pl.BlockSpec((L, SSM_WIDTH), lambda b, c: (row(b, c), COL_Z // SSM_WIDTH)),
            pl.BlockSpec((L, SSM_WIDTH), lambda b, c: (row(b, c), COL_XS // SSM_WIDTH)),
            pl.BlockSpec((L, SSM_BC_WIDTH), lambda b, c: (row(b, c), COL_BC // SSM_BC_WIDTH)),
            pl.BlockSpec((L, LANES), lambda b, c: (row(b, c), 0)),
            full((SSM_CONV, SSM_WIDTH)),
            full((SSM_CONV, SSM_BC_WIDTH)),
            full((1, SSM_WIDTH)),
            full((1, SSM_BC_WIDTH)),
            full((1, LANES)),
            full((1, LANES)),
            full((1, SSM_WIDTH)),
            full((1, SSM_WIDTH)),
        ],
        out_specs=pl.BlockSpec((L, SSM_WIDTH), lambda b, c: (row(b, c), 0)),
        out_shape=jax.ShapeDtypeStruct((batch * seq, SSM_WIDTH), BF16),
        scratch_shapes=[
            pltpu.VMEM((SUBLANES, SSM_WIDTH), F32),
            pltpu.VMEM((SUBLANES, SSM_BC_WIDTH), F32),
            pltpu.VMEM((SSM_HEADS // 2, SSM_STATE, PAIR_WIDTH), F32),
            pltpu.VMEM((L, SSM_WIDTH), F32),
        ],
        compiler_params=_params(("parallel", "arbitrary")),
        name="ssd_mixer",
    )(proj, proj, proj, dt_raw, cwx, cwbc, cbx, cbbc, dt_bias, a_log, d_skip_vec, norm_w)


def _out_proj_kernel(att_ref, ssm_ref, w_ref, x_ref, wd_ref, x1_ref, wdb_ref, wb_ref):
    @pl.when(pl.program_id(1) == 0)
    def _():
        wb_ref[...] = w_ref[...].astype(BF16)

    acc = jnp.dot(att_ref[...], wb_ref[0], preferred_element_type=F32)
    acc = acc + jnp.dot(ssm_ref[...], wb_ref[1], preferred_element_type=F32)
    x1_ref[...] = x_ref[...] + acc
    wdb_ref[...] = wd_ref[...].astype(BF16)


def _out_proj(att, ssm, w_out2, x2d, w_down, tm=512, tn=1024):
    m = att.shape[0]
    ni = m // tm
    n_steps = (D_MODEL // tn) * ni
    d_ff = w_down.shape[0]
    wd_rows = d_ff // n_steps
    assert wd_rows * n_steps == d_ff and wd_rows % ONES_ROWS == 0
    return pl.pallas_call(
        _out_proj_kernel,
        grid=(D_MODEL // tn, ni),
        in_specs=[
            pl.BlockSpec((tm, ATT_WIDTH), lambda j, i: (i, 0)),
            pl.BlockSpec((tm, SSM_WIDTH), lambda j, i: (i, 0)),
            pl.BlockSpec((2, ATT_WIDTH, tn), lambda j, i: (0, 0, j), pipeline_mode=pl.Buffered(1)),
            pl.BlockSpec((tm, tn), lambda j, i: (i, j)),
            pl.BlockSpec((wd_rows, D_MODEL), lambda j, i: (j * ni + i, 0)),
        ],
        out_specs=[
            pl.BlockSpec((tm, tn), lambda j, i: (i, j)),
            pl.BlockSpec((wd_rows, D_MODEL), lambda j, i: (j * ni + i, 0)),
        ],
        out_shape=[
            jax.ShapeDtypeStruct((m, D_MODEL), F32),
            jax.ShapeDtypeStruct(w_down.shape, BF16),
        ],
        scratch_shapes=[pltpu.VMEM((2, ATT_WIDTH, tn), BF16)],
        compiler_params=_params(("parallel", "arbitrary")),
        name="out_proj",
    )(att, ssm, w_out2, x2d, w_down)


def _gate_up_kernel(x_ref, nw_ref, wg_ref, wu_ref, a_ref, h_ref):
    def gated(h):
        g = jnp.dot(h, wg_ref[...].astype(BF16), preferred_element_type=F32)
        u = jnp.dot(h, wu_ref[...].astype(BF16), preferred_element_type=F32)
        a_ref[...] = ((g * _sigmoid(g)) * u).astype(BF16)

    @pl.when(pl.program_id(1) == 0)
    def _():
        h = _rmsnorm(x_ref[...], nw_ref[...], RMS_EPS).astype(BF16)
        h_ref[...] = h
        gated(h)

    @pl.when(pl.program_id(1) > 0)
    def _():
        gated(h_ref[...])


def _gate_up(x1, norm_w, w_gate, w_up, tm=1024, tn=512):
    m = x1.shape[0]
    n = w_gate.shape[1]
    return pl.pallas_call(
        _gate_up_kernel,
        grid=(m // tm, n // tn),
        in_specs=[
            pl.BlockSpec((tm, D_MODEL), lambda i, j: (i, 0)),
            pl.BlockSpec((1, D_MODEL), lambda i, j: (0, 0)),
            pl.BlockSpec((D_MODEL, tn), lambda i, j: (0, j)),
            pl.BlockSpec((D_MODEL, tn), lambda i, j: (0, j)),
        ],
        out_specs=pl.BlockSpec((tm, tn), lambda i, j: (i, j)),
        out_shape=jax.ShapeDtypeStruct((m, n), BF16),
        scratch_shapes=[pltpu.VMEM((tm, D_MODEL), BF16)],
        compiler_params=_params(("parallel", "arbitrary")),
        name="gate_up",
    )(x1, norm_w, w_gate, w_up)


def _down_kernel(a_ref, w_ref, x_ref, nw_ref, o_ref, x2_ref):
    j = pl.program_id(1)
    x2_ref[j] = x_ref[...] + jnp.dot(a_ref[...], w_ref[...], preferred_element_type=F32)

    @pl.when(j == pl.num_programs(1) - 1)
    def _():
        x2 = jnp.concatenate([x2_ref[t] for t in range(x2_ref.shape[0])], axis=1)
        o_ref[...] = _rmsnorm(x2, nw_ref[...], RMS_EPS)


def _down(a, w_down, x1, norm_w, tm=512, tn=1024):
    m, kdim = a.shape
    return pl.pallas_call(
        _down_kernel,
        grid=(m // tm, D_MODEL // tn),
        in_specs=[
            pl.BlockSpec((tm, kdim), lambda i, j: (i, 0)),
            pl.BlockSpec((kdim, tn), lambda i, j: (0, j)),
            pl.BlockSpec((tm, tn), lambda i, j: (i, j)),
            pl.BlockSpec((1, D_MODEL), lambda i, j: (0, 0)),
        ],
        out_specs=pl.BlockSpec((tm, D_MODEL), lambda i, j: (i, 0)),
        out_shape=jax.ShapeDtypeStruct((m, D_MODEL), F32),
        scratch_shapes=[pltpu.VMEM((D_MODEL // tn, tm, tn), F32)],
        compiler_params=_params(("parallel", "arbitrary")),
        name="down_proj",
    )(a, w_down, x1, norm_w)


def _alibi_slopes_log2(n):
    start = 2.0 ** (-8.0 / n)
    return jnp.asarray([start ** (i + 1) for i in range(n)], dtype=F32) * LOG2E


def _pad_lanes(v):
    return jnp.pad(v.astype(F32), (0, LANES - v.shape[0])).reshape(1, LANES)


def kernel(x, norm_mix_w, w_in, lambda_q1, lambda_k1, lambda_q2, lambda_k2, subln_w, conv_w, conv_b, dt_bias, a_log, d_skip, ssm_norm_w, w_out, norm_ffn_w, w_gate, w_up, w_down, norm_final_w):
    batch, seq, _ = x.shape
    assert w_in.shape[0] == 1, "single-layer block"
    layer = 0
    x2d = x.reshape(batch * seq, D_MODEL)
    col_scale = jnp.concatenate([
        jnp.full((ATT_WIDTH,), ATT_QK_DIM ** -0.5 * LOG2E, F32),
        jnp.ones((PROJ_MAIN - ATT_WIDTH,), F32)]).reshape(1, PROJ_MAIN)
    slopes2 = _alibi_slopes_log2(ATT_HEADS)
    lambda_init = 0.8 - 0.6 * math.exp(-0.3 * layer)
    w_in_t = w_in[layer].T
    w_dt_t = jnp.pad(w_in_t[PROJ_MAIN:], ((0, LANES - SSM_HEADS), (0, 0))).astype(BF16)
    proj, dt_raw = _in_proj(x2d, norm_mix_w[layer].reshape(1, D_MODEL), w_in_t, w_dt_t, col_scale)
    lamv = jnp.stack([lambda_q1[layer], lambda_k1[layer], lambda_q2[layer], lambda_k2[layer]]).astype(F32)
    subw_col = jnp.broadcast_to(subln_w[layer].astype(F32).reshape(ATT_V_DIM, 1), (ATT_V_DIM, LANES))
    att = _attention(proj, slopes2, lamv, subw_col, batch, seq, lambda_init)
    ssm = _ssd(proj, dt_raw, conv_w[layer], conv_b[layer].reshape(1, -1),
               _pad_lanes(dt_bias[layer]), _pad_lanes(a_log[layer]),
               jnp.repeat(d_skip[layer].astype(F32), SSM_HEAD_DIM).reshape(1, SSM_WIDTH),
               ssm_norm_w[layer].reshape(1, SSM_WIDTH), batch, seq)
    w_out2 = w_out[layer].reshape(2, ATT_WIDTH, D_MODEL)
    x1, w_down_b = _out_proj(att, ssm, w_out2, x2d, w_down[layer])
    a = _gate_up(x1, norm_ffn_w[layer].reshape(1, D_MODEL), w_gate[layer], w_up[layer])
    out = _down(a, w_down_b, x1, norm_final_w.reshape(1, D_MODEL))
    return out.reshape(batch, seq, D_MODEL)
```

```python
import functools
import math

import jax
import jax.numpy as jnp
from jax import lax
from jax.experimental import pallas as pl
from jax.experimental.pallas import tpu as pltpu

F32 = jnp.float32
BF16 = jnp.bfloat16

D_MODEL = 2048
ATT_HEADS = 8
ATT_QK_DIM = 128
ATT_V_DIM = 2 * ATT_QK_DIM
ATT_WIDTH = ATT_HEADS * ATT_V_DIM
SSM_HEADS = 32
SSM_HEAD_DIM = 64
SSM_WIDTH = SSM_HEADS * SSM_HEAD_DIM
SSM_GROUPS = 4
SSM_STATE = 128
SSM_CONV = 4
SSM_CHUNK = 128
SSM_BC_WIDTH = 2 * SSM_GROUPS * SSM_STATE
PROJ_MAIN = 2 * ATT_WIDTH + ATT_WIDTH + SSM_WIDTH + SSM_WIDTH + SSM_BC_WIDTH
RMS_EPS = 1e-6
SUB_EPS = 1e-5
LOG2E = 1.4426950408889634
LANES = 128
SUBLANES = 8
ONES_ROWS = 16
BLOCKS_PER_ITER = 2
EARLY_BLOCKS = 1
HEADS_PER_GROUP = SSM_HEADS // SSM_GROUPS
PAIR_WIDTH = 2 * SSM_HEAD_DIM
NEG_INF = float("-inf")

COL_Q, COL_K, COL_V = 0, ATT_WIDTH, 2 * ATT_WIDTH
COL_Z = 3 * ATT_WIDTH
COL_XS = COL_Z + SSM_WIDTH
COL_BC = COL_XS + SSM_WIDTH

VMEM_LIMIT = 56 * 1024 * 1024


def _params(semantics):
    return pltpu.CompilerParams(dimension_semantics=semantics, vmem_limit_bytes=VMEM_LIMIT)


def _sigmoid(x):
    return 1.0 / (1.0 + jnp.exp(-x))


def _rmsnorm(x, w, eps):
    ms = jnp.mean(x * x, axis=-1, keepdims=True)
    return (x * lax.rsqrt(ms + eps)) * w


def _lane_tile(x, width):
    return x if width == LANES else jnp.concatenate([x] * (width // LANES), axis=1)


_CONTRACT_LAST = (((1,), (1,)), ((), ()))


def _in_proj_kernel(x_ref, nw_ref, wt_ref, wdt_ref, cs_ref, proj_ref, dt_ref, h_ref):
    def project(hb):
        acc = lax.dot_general(hb, wt_ref[...].astype(BF16), _CONTRACT_LAST,
                              preferred_element_type=F32)
        proj_ref[...] = (acc * cs_ref[...]).astype(BF16)

    @pl.when(pl.program_id(1) == 0)
    def _():
        hb = _rmsnorm(x_ref[...], nw_ref[...], RMS_EPS).astype(BF16)
        h_ref[...] = hb
        dt_ref[...] = lax.dot_general(hb, wdt_ref[...], _CONTRACT_LAST,
                                      preferred_element_type=F32)
        project(hb)

    @pl.when(pl.program_id(1) > 0)
    def _():
        project(h_ref[...])


def _in_proj(x2d, norm_w, w_in_t, w_dt_t, col_scale, tm=1024, tn=1024):
    m = x2d.shape[0]
    return pl.pallas_call(
        _in_proj_kernel,
        grid=(m // tm, PROJ_MAIN // tn),
        in_specs=[
            pl.BlockSpec((tm, D_MODEL), lambda i, j: (i, 0)),
            pl.BlockSpec((1, D_MODEL), lambda i, j: (0, 0)),
            pl.BlockSpec((tn, D_MODEL), lambda i, j: (j, 0)),
            pl.BlockSpec((LANES, D_MODEL), lambda i, j: (0, 0)),
            pl.BlockSpec((1, tn), lambda i, j: (0, j)),
        ],
        out_specs=[
            pl.BlockSpec((tm, tn), lambda i, j: (i, j)),
            pl.BlockSpec((tm, LANES), lambda i, j: (i, 0)),
        ],
        out_shape=[
            jax.ShapeDtypeStruct((m, PROJ_MAIN), BF16),
            jax.ShapeDtypeStruct((m, LANES), F32),
        ],
        scratch_shapes=[pltpu.VMEM((tm, D_MODEL), BF16)],
        compiler_params=_params(("parallel", "arbitrary")),
        name="in_proj",
    )(x2d, norm_w, w_in_t, w_dt_t, col_scale)


def _attn_kernel(slope_ref, lamv_ref, subw_ref, q_ref, k_ref, v_ref, o_ref,
                  vt_scr, qt_scr, kaug_scr, st_scr, m_scr, acc_scr,
                  *, tq, tk, qc, tiles, lambda_init):
    h = pl.program_id(1)
    step = pl.program_id(2)
    slope2 = slope_ref[h]
    aug_lane = lax.broadcasted_iota(jnp.int32, (tk, ATT_QK_DIM), 1)

    @pl.when(step == 0)
    def _():
        bias = slope2 * lax.broadcasted_iota(jnp.int32, (tk, ATT_QK_DIM), 0).astype(F32)
        hi = bias.astype(BF16).astype(F32)
        mid = (bias - hi).astype(BF16).astype(F32)
        lo = (bias - hi) - mid
        kaug_scr[...] = jnp.where(aug_lane == 0, hi, jnp.where(
            aug_lane == 1, mid, jnp.where(aug_lane == 2, lo, 0.0))).astype(BF16)

    row_minus_col = (lax.broadcasted_iota(jnp.int32, (tk, qc), 0)
                     - lax.broadcasted_iota(jnp.int32, (tk, qc), 1))
    pick = jnp.where(lax.broadcasted_iota(jnp.int32, (ATT_QK_DIM, tq), 0) < 3, 1.0, 0.0).astype(BF16)
    n_qchunks = tq // qc
    lv = lamv_ref[...]
    lam = (jnp.exp(jnp.sum(lv[0:1] * lv[1:2], axis=1, keepdims=True))
           - jnp.exp(jnp.sum(lv[2:3] * lv[3:4], axis=1, keepdims=True)) + lambda_init)

    def block_chains(d=None):
        out = []
        for u in range(n_qchunks):
            if d is not None and d * tk >= (u + 1) * qc:
                continue
            thr = None if d is None or (d + 1) * tk <= u * qc else u * qc - d * tk
            out += [(u, c, thr) for c in range(2)]
        return out

    def q_tile(t):
        rows = slice(t * tq, (t + 1) * tq)
        qi = step * tiles + t

        qt = q_ref[rows, :].T
        for c in range(2):
            qt_scr[t, c] = jnp.concatenate([qt[c * ATT_QK_DIM:(c + 1) * ATT_QK_DIM, :], pick], axis=0)
        m_scr[t] = jnp.full(m_scr.shape[1:], NEG_INF, F32)
        acc_scr[t] = jnp.zeros(acc_scr.shape[1:], F32)

        def scores(j, c, u):
            k0 = j * tk if isinstance(j, int) else pl.multiple_of(j * tk, tk)
            dims = slice(c * ATT_QK_DIM, (c + 1) * ATT_QK_DIM)
            k_aug = jnp.concatenate([k_ref[pl.ds(k0, tk), dims], kaug_scr[...]], axis=1)
            return jnp.dot(k_aug, qt_scr[t, c, :, u * qc:(u + 1) * qc], preferred_element_type=F32)

        def early_slot(blk, u, c):
            return (blk * n_qchunks + u) * 2 + c

        def prefetch_scores(j):
            for blk in range(EARLY_BLOCKS):
                for u, c, _ in block_chains():
                    st_scr[early_slot(blk, u, c)] = scores(j + blk, c, u)

        def chain(j, c, u, thr, slot=None):
            cs = slice(u * qc, (u + 1) * qc)
            st = scores(j, c, u) if slot is None else st_scr[slot]
            if thr is not None:
                st = jnp.where(row_minus_col <= thr, st, NEG_INF)
            off = slope2 * jnp.asarray(j * tk, F32)
            m_old = m_scr[t, c, 0:1, cs]
            m_new = jnp.maximum(m_old, jnp.max(st, axis=0, keepdims=True) + off)
            alpha = jnp.exp2(m_old - m_new)
            pt = jnp.exp2(st - (m_new - off))
            m_scr[t, c, :, cs] = jnp.broadcast_to(m_new, (SUBLANES, qc))
            acc_scr[t, c, :, cs] = alpha * acc_scr[t, c, :, cs] + jnp.dot(
                vt_scr[j], pt.astype(BF16), preferred_element_type=F32)

        def run_block(j, chains, early_blk):
            for u, c, thr in chains:
                chain(j, c, u, thr, None if early_blk is None else early_slot(early_blk, u, c))

        def body(jj, carry):
            for i in range(BLOCKS_PER_ITER):
                run_block(BLOCKS_PER_ITER * jj + i, block_chains(), i if i < EARLY_BLOCKS else None)
            prefetch_scores(BLOCKS_PER_ITER * (jj + 1))
            return carry

        n_full = (qi * tq) // tk
        prefetch_scores(0)
        for d in range(tq // tk):
            vt = v_ref[pl.ds(pl.multiple_of((n_full + d) * tk, tk), tk), :].T
            vt_scr[n_full + d] = jnp.concatenate([vt, jnp.ones((ONES_ROWS, tk), BF16)], axis=0)
        lax.fori_loop(0, n_full // BLOCKS_PER_ITER, body, 0)
        for d in range(tq // tk):
            run_block(n_full + d, block_chains(d), d if d < EARLY_BLOCKS else None)

        inv_l0 = 1.0 / acc_scr[t, 0, ATT_V_DIM:ATT_V_DIM + 1, :]
        inv_l1 = lam / acc_scr[t, 1, ATT_V_DIM:ATT_V_DIM + 1, :]
        ot = (acc_scr[t, 0, :ATT_V_DIM, :] * inv_l0
              - acc_scr[t, 1, :ATT_V_DIM, :] * inv_l1)
        ms = jnp.mean(ot * ot, axis=0, keepdims=True)
        ot = (ot * lax.rsqrt(ms + SUB_EPS)) * _lane_tile(subw_ref[...], tq)
        o_ref[rows, :] = (ot * (1.0 - lambda_init)).astype(BF16).T

    for t in range(tiles):
        q_tile(t)


def _attention(proj, slopes2, lamv, subw_col, batch, seq, lambda_init,
               tq=1024, tk=256, qc=256, tiles=4):
    assert tq % (BLOCKS_PER_ITER * tk) == 0 and tq % qc == 0 and seq % (tiles * tq) == 0
    nsteps = seq // (tiles * tq)
    kern = functools.partial(_attn_kernel, tq=tq, tk=tk, qc=qc, tiles=tiles,
                             lambda_init=lambda_init)
    kb, vb = COL_K // ATT_V_DIM, COL_V // ATT_V_DIM
    return pl.pallas_call(
        kern,
        grid=(batch, ATT_HEADS, nsteps),
        in_specs=[
            pl.BlockSpec(memory_space=pltpu.SMEM),
            pl.BlockSpec((4, ATT_QK_DIM), lambda b, h, i: (0, 0)),
            pl.BlockSpec((ATT_V_DIM, LANES), lambda b, h, i: (0, 0)),
            pl.BlockSpec((tiles * tq, ATT_V_DIM), lambda b, h, i: (b * nsteps + i, h)),
            pl.BlockSpec((seq, ATT_V_DIM), lambda b, h, i: (b, kb + h)),
            pl.BlockSpec((seq, ATT_V_DIM), lambda b, h, i: (b, vb + h)),
        ],
        out_specs=pl.BlockSpec((tiles * tq, ATT_V_DIM), lambda b, h, i: (b * nsteps + i, h)),
        out_shape=jax.ShapeDtypeStruct((batch * seq, ATT_WIDTH), BF16),
        scratch_shapes=[
            pltpu.VMEM((seq // tk, ATT_V_DIM + ONES_ROWS, tk), BF16),
            pltpu.VMEM((tiles, 2, 2 * ATT_QK_DIM, tq), BF16),
            pltpu.VMEM((tk, ATT_QK_DIM), BF16),
            pltpu.VMEM((EARLY_BLOCKS * (tq // qc) * 2, tk, qc), F32),
            pltpu.VMEM((tiles, 2, SUBLANES, tq), F32),
            pltpu.VMEM((tiles, 2, ATT_V_DIM + ONES_ROWS, tq), F32),
        ],
        compiler_params=_params(("parallel", "parallel", "arbitrary")),
        name="diff_attention",
    )(slopes2, lamv, subw_col, proj, proj, proj)


def _ssd_kernel(z_ref, xs_ref, bc_ref, dt_ref, cwx_ref, cwbc_ref, cbx_ref, cbbc_ref,
                dtb_ref, alog_ref, dskip_ref, nw_ref, out_ref,
                tailx, tailbc, state, yscr):
    @pl.when(pl.program_id(1) == 0)
    def _():
        tailx[...] = jnp.zeros(tailx.shape, F32)
        tailbc[...] = jnp.zeros(tailbc.shape, F32)
        state[...] = jnp.zeros(state.shape, F32)

    for s in range(yscr.shape[0]):
        rows = pl.ds(s * SSM_CHUNK, SSM_CHUNK)
        _ssd_chunk(z_ref.at[rows], xs_ref.at[rows], bc_ref.at[rows], dt_ref.at[rows],
                   cwx_ref, cwbc_ref, cbx_ref, cbbc_ref, dtb_ref, alog_ref, dskip_ref, nw_ref,
                   out_ref.at[rows], tailx, tailbc, state, yscr.at[s])


def _ssd_chunk(z_ref, xs_ref, bc_ref, dt_ref, cwx_ref, cwbc_ref, cbx_ref, cbbc_ref,
               dtb_ref, alog_ref, dskip_ref, nw_ref, out_ref,
               tailx, tailbc, state, yscr):
    L = SSM_CHUNK

    def conv_silu(u_ref, tail, w_ref, b_ref):
        u = u_ref[...].astype(F32)
        full = jnp.concatenate([tail[...], u], axis=0)
        acc = b_ref[...] + w_ref[SSM_CONV - 1:SSM_CONV, :] * u
        for k in range(SSM_CONV - 1):
            lo = SUBLANES - (SSM_CONV - 1) + k
            acc = acc + w_ref[k:k + 1, :] * full[lo:lo + L]
        tail[...] = u[L - SUBLANES:L]
        return acc * _sigmoid(acc)

    xs = conv_silu(xs_ref, tailx, cwx_ref, cbx_ref)
    bc = conv_silu(bc_ref, tailbc, cwbc_ref, cbbc_ref)
    xs_b = xs.astype(BF16)
    bc_b = bc.astype(BF16)

    dtr = dt_ref[...] + dtb_ref[...]
    dt = jnp.maximum(dtr, 0.0) + jnp.log1p(jnp.exp(-jnp.abs(dtr)))
    acs = dt * (-jnp.exp(alog_ref[...]))
    rowi = lax.broadcasted_iota(jnp.int32, (L, LANES), 0)
    shift = 1
    while shift < L:
        acs = acs + jnp.where(rowi >= shift, pltpu.roll(acs, shift, axis=0), 0.0)
        shift *= 2
    acs = acs * LOG2E
    acs_t = acs.T
    dt_t = dt.T
    exp_a = jnp.exp2(acs)
    w_t = dt_t * jnp.exp2(acs_t[:, L - 1:L] - acs_t)
    cd_row = jnp.exp2(acs[L - 1:L, :])
    src_t = acs_t - jnp.log2(dt_t)

    li = lax.broadcasted_iota(jnp.int32, (L, L), 0)
    si = lax.broadcasted_iota(jnp.int32, (L, L), 1)
    causal = si <= li
    lane = lax.broadcasted_iota(jnp.int32, (L, PAIR_WIDTH), 1)
    lo_half = lane < SSM_HEAD_DIM
    lane_row = lax.broadcasted_iota(jnp.int32, (1, PAIR_WIDTH), 1) < SSM_HEAD_DIM
    zero_b = jnp.zeros((L, PAIR_WIDTH), BF16)

    gn = SSM_GROUPS * SSM_STATE
    for g in range(SSM_GROUPS):
        b_g = bc[:, g * SSM_STATE:(g + 1) * SSM_STATE]
        c_g = bc[:, gn + g * SSM_STATE:gn + (g + 1) * SSM_STATE]
        cb = lax.dot_general(bc_b[:, gn + g * SSM_STATE:gn + (g + 1) * SSM_STATE],
                             bc_b[:, g * SSM_STATE:(g + 1) * SSM_STATE],
                             (((1,), (1,)), ((), ())), preferred_element_type=F32)
        b_gt = b_g.T
        for pr in range(HEADS_PER_GROUP // 2):
            pair = g * (HEADS_PER_GROUP // 2) + pr
            lhs_y, lhs_s = [], []
            for hh in (2 * pair, 2 * pair + 1):
                seg = acs[:, hh:hh + 1] - src_t[hh:hh + 1, :]
                lhs_y.append((cb * jnp.exp2(jnp.where(causal, seg, NEG_INF))).astype(BF16))
                lhs_s.append((b_gt * w_t[hh:hh + 1, :]).astype(BF16))
            for hh in (2 * pair, 2 * pair + 1):
                lhs_y.append((c_g * exp_a[:, hh:hh + 1]).astype(BF16))
            x_p = xs_b[:, pair * PAIR_WIDTH:(pair + 1) * PAIR_WIDTH]
            x_lo = jnp.where(lo_half, x_p, zero_b)
            x_hi = jnp.where(lo_half, zero_b, x_p)
            st = state[pair]
            st_b = st.astype(BF16)
            st_lo = jnp.where(lo_half, st_b, zero_b)
            st_hi = jnp.where(lo_half, zero_b, st_b)
            y = jnp.dot(jnp.concatenate(lhs_y, axis=1),
                        jnp.concatenate([x_lo, x_hi, st_lo, st_hi], axis=0),
                        preferred_element_type=F32)
            yscr[:, pair * PAIR_WIDTH:(pair + 1) * PAIR_WIDTH] = y
            new = jnp.dot(jnp.concatenate(lhs_s, axis=1),
                          jnp.concatenate([x_lo, x_hi], axis=0),
                          preferred_element_type=F32)
            cd = jnp.where(lane_row, cd_row[:, 2 * pair:2 * pair + 1],
                           cd_row[:, 2 * pair + 1:2 * pair + 2])
            state[pair] = st * cd + new

    y = yscr[...] + dskip_ref[...] * xs
    z = z_ref[...].astype(F32)
    y = y * (z * _sigmoid(z))
    gw = SSM_WIDTH // SSM_GROUPS
    for g in range(SSM_GROUPS):
        cols = slice(g * gw, (g + 1) * gw)
        out_ref[:, cols] = _rmsnorm(y[:, cols], nw_ref[:, cols], SUB_EPS).astype(BF16)


def _ssd(proj, dt_raw, conv_w, conv_b, dt_bias, a_log, d_skip_vec, norm_w, batch, seq, chunks=4):
    L = chunks * SSM_CHUNK
    nc = seq // L
    assert nc * L == seq
    row = lambda b, c: b * nc + c
    full = lambda shape: pl.BlockSpec(shape, lambda b, c: (0, 0))
    cwx, cwbc = conv_w[:, :SSM_WIDTH], conv_w[:, SSM_WIDTH:]
    cbx, cbbc = conv_b[:, :SSM_WIDTH], conv_b[:, SSM_WIDTH:]
    return pl.pallas_call(
        _ssd_kernel,
        grid=(batch, nc),
        in_specs=[
            pl.BlockSpec((L, SSM_WIDTH), lambda b, c: (row(b, c), COL_Z // SSM_WIDTH)),
            pl.BlockSpec((L, SSM_WIDTH), lambda b, c: (row(b, c), COL_XS // SSM_WIDTH)),
            pl.BlockSpec((L, SSM_BC_WIDTH), lambda b, c: (row(b, c), COL_BC // SSM_BC_WIDTH)),
            pl.BlockSpec((L, LANES), lambda b, c: (row(b, c), 0)),
            full((SSM_CONV, SSM_WIDTH)),
            full((SSM_CONV, SSM_BC_WIDTH)),
            full((1, SSM_WIDTH)),
            full((1, SSM_BC_WIDTH)),
            full((1, LANES)),
            full((1, LANES)),
            full((1, SSM_WIDTH)),
            full((1, SSM_WIDTH)),
        ],
        out_specs=pl.BlockSpec((L, SSM_WIDTH), lambda b, c: (row(b, c), 0)),
        out_shape=jax.ShapeDtypeStruct((batch * seq, SSM_WIDTH), BF16),
        scratch_shapes=[
            pltpu.VMEM((SUBLANES, SSM_WIDTH), F32),
            pltpu.VMEM((SUBLANES, SSM_BC_WIDTH), F32),
            pltpu.VMEM((SSM_HEADS // 2, SSM_STATE, PAIR_WIDTH), F32),
            pltpu.VMEM((chunks, SSM_CHUNK, SSM_WIDTH), F32),
        ],
        compiler_params=_params(("parallel", "arbitrary")),
        name="ssd_mixer",
    )(proj, proj, proj, dt_raw, cwx, cwbc, cbx, cbbc, dt_bias, a_log, d_skip_vec, norm_w)


def _out_proj_kernel(att_ref, ssm_ref, w_ref, x_ref, wd_ref, x1_ref, wdb_ref, wb_ref):
    @pl.when(pl.program_id(1) == 0)
    def _():
        wb_ref[...] = w_ref[...].astype(BF16)

    acc = jnp.dot(att_ref[...], wb_ref[0], preferred_element_type=F32)
    acc = acc + jnp.dot(ssm_ref[...], wb_ref[1], preferred_element_type=F32)
    x1_ref[...] = x_ref[...] + acc
    wdb_ref[...] = wd_ref[...].astype(BF16)


def _out_proj(att, ssm, w_out2, x2d, w_down, tm=512, tn=1024):
    m = att.shape[0]
    ni = m // tm
    n_steps = (D_MODEL // tn) * ni
    d_ff = w_down.shape[0]
    wd_rows = d_ff // n_steps
    assert wd_rows * n_steps == d_ff and wd_rows % ONES_ROWS == 0
    return pl.pallas_call(
        _out_proj_kernel,
        grid=(D_MODEL // tn, ni),
        in_specs=[
            pl.BlockSpec((tm, ATT_WIDTH), lambda j, i: (i, 0)),
            pl.BlockSpec((tm, SSM_WIDTH), lambda j, i: (i, 0)),
            pl.BlockSpec((2, ATT_WIDTH, tn), lambda j, i: (0, 0, j), pipeline_mode=pl.Buffered(1)),
            pl.BlockSpec((tm, tn), lambda j, i: (i, j)),
            pl.BlockSpec((wd_rows, D_MODEL), lambda j, i: (j * ni + i, 0)),
        ],
        out_specs=[
            pl.BlockSpec((tm, tn), lambda j, i: (i, j)),
            pl.BlockSpec((wd_rows, D_MODEL), lambda j, i: (j * ni + i, 0)),
        ],
        out_shape=[
            jax.ShapeDtypeStruct((m, D_MODEL), F32),
            jax.ShapeDtypeStruct(w_down.shape, BF16),
        ],
        scratch_shapes=[pltpu.VMEM((2, ATT_WIDTH, tn), BF16)],
        compiler_params=_params(("parallel", "arbitrary")),
        name="out_proj",
    )(att, ssm, w_out2, x2d, w_down)


def _gate_up_kernel(x_ref, nw_ref, wg_ref, wu_ref, a_ref, h_ref):
    def gated(h):
        g = jnp.dot(h, wg_ref[...].astype(BF16), preferred_element_type=F32)
        u = jnp.dot(h, wu_ref[...].astype(BF16), preferred_element_type=F32)
        a_ref[...] = ((g * _sigmoid(g)) * u).astype(BF16)

    @pl.when(pl.program_id(1) == 0)
    def _():
        h = _rmsnorm(x_ref[...], nw_ref[...], RMS_EPS).astype(BF16)
        h_ref[...] = h
        gated(h)

    @pl.when(pl.program_id(1) > 0)
    def _():
        gated(h_ref[...])


def _gate_up(x1, norm_w, w_gate, w_up, tm=1024, tn=512):
    m = x1.shape[0]
    n = w_gate.shape[1]
    return pl.pallas_call(
        _gate_up_kernel,
        grid=(m // tm, n // tn),
        in_specs=[
            pl.BlockSpec((tm, D_MODEL), lambda i, j: (i, 0)),
            pl.BlockSpec((1, D_MODEL), lambda i, j: (0, 0)),
            pl.BlockSpec((D_MODEL, tn), lambda i, j: (0, j)),
            pl.BlockSpec((D_MODEL, tn), lambda i, j: (0, j)),
        ],
        out_specs=pl.BlockSpec((tm, tn), lambda i, j: (i, j)),
        out_shape=jax.ShapeDtypeStruct((m, n), BF16),
        scratch_shapes=[pltpu.VMEM((tm, D_MODEL), BF16)],
        compiler_params=_params(("parallel", "arbitrary")),
        name="gate_up",
    )(x1, norm_w, w_gate, w_up)


def _down_kernel(a_ref, w_ref, x_ref, nw_ref, o_ref, x2_ref):
    j = pl.program_id(1)
    x2_ref[j] = x_ref[...] + jnp.dot(a_ref[...], w_ref[...], preferred_element_type=F32)

    @pl.when(j == pl.num_programs(1) - 1)
    def _():
        x2 = jnp.concatenate([x2_ref[t] for t in range(x2_ref.shape[0])], axis=1)
        o_ref[...] = _rmsnorm(x2, nw_ref[...], RMS_EPS)


def _down(a, w_down, x1, norm_w, tm=512, tn=1024):
    m, kdim = a.shape
    return pl.pallas_call(
        _down_kernel,
        grid=(m // tm, D_MODEL // tn),
        in_specs=[
            pl.BlockSpec((tm, kdim), lambda i, j: (i, 0)),
            pl.BlockSpec((kdim, tn), lambda i, j: (0, j)),
            pl.BlockSpec((tm, tn), lambda i, j: (i, j)),
            pl.BlockSpec((1, D_MODEL), lambda i, j: (0, 0)),
        ],
        out_specs=pl.BlockSpec((tm, D_MODEL), lambda i, j: (i, 0)),
        out_shape=jax.ShapeDtypeStruct((m, D_MODEL), F32),
        scratch_shapes=[pltpu.VMEM((D_MODEL // tn, tm, tn), F32)],
        compiler_params=_params(("parallel", "arbitrary")),
        name="down_proj",
    )(a, w_down, x1, norm_w)


def _alibi_slopes_log2(n):
    start = 2.0 ** (-8.0 / n)
    return jnp.asarray([start ** (i + 1) for i in range(n)], dtype=F32) * LOG2E


def _pad_lanes(v):
    return jnp.pad(v.astype(F32), (0, LANES - v.shape[0])).reshape(1, LANES)


def kernel(x, norm_mix_w, w_in, lambda_q1, lambda_k1, lambda_q2, lambda_k2, subln_w, conv_w, conv_b, dt_bias, a_log, d_skip, ssm_norm_w, w_out, norm_ffn_w, w_gate, w_up, w_down, norm_final_w):
    batch, seq, _ = x.shape
    assert w_in.shape[0] == 1, "single-layer block"
    layer = 0
    x2d = x.reshape(batch * seq, D_MODEL)
    col_scale = jnp.concatenate([
        jnp.full((ATT_WIDTH,), ATT_QK_DIM ** -0.5 * LOG2E, F32),
        jnp.ones((PROJ_MAIN - ATT_WIDTH,), F32)]).reshape(1, PROJ_MAIN)
    slopes2 = _alibi_slopes_log2(ATT_HEADS)
    lambda_init = 0.8 - 0.6 * math.exp(-0.3 * layer)
    w_in_t = w_in[layer].T
    w_dt_t = jnp.pad(w_in_t[PROJ_MAIN:], ((0, LANES - SSM_HEADS), (0, 0))).astype(BF16)
    proj, dt_raw = _in_proj(x2d, norm_mix_w[layer].reshape(1, D_MODEL), w_in_t, w_dt_t, col_scale)
    lamv = jnp.stack([lambda_q1[layer], lambda_k1[layer], lambda_q2[layer], lambda_k2[layer]]).astype(F32)
    subw_col = jnp.broadcast_to(subln_w[layer].astype(F32).reshape(ATT_V_DIM, 1), (ATT_V_DIM, LANES))
    att = _attention(proj, slopes2, lamv, subw_col, batch, seq, lambda_init)
    ssm = _ssd(proj, dt_raw, conv_w[layer], conv_b[layer].reshape(1, -1),
               _pad_lanes(dt_bias[layer]), _pad_lanes(a_log[layer]),
               jnp.repeat(d_skip[layer].astype(F32), SSM_HEAD_DIM).reshape(1, SSM_WIDTH),
               ssm_norm_w[layer].reshape(1, SSM_WIDTH), batch, seq)
    w_out2 = w_out[layer].reshape(2, ATT_WIDTH, D_MODEL)
    x1, w_down_b = _out_proj(att, ssm, w_out2, x2d, w_down[layer])
    a = _gate_up(x1, norm_ffn_w[layer].reshape(1, D_MODEL), w_gate[layer], w_up[layer])
    out = _down(a, w_down_b, x1, norm_final_w.reshape(1, D_MODEL))
    return out.reshape(batch, seq, D_MODEL)
```

```python
import functools
import math

import jax
import jax.numpy as jnp
from jax import lax
from jax.experimental import pallas as pl
from jax.experimental.pallas import tpu as pltpu

F32 = jnp.float32
BF16 = jnp.bfloat16

D_MODEL = 2048
ATT_HEADS = 8
ATT_QK_DIM = 128
ATT_V_DIM = 2 * ATT_QK_DIM
ATT_WIDTH = ATT_HEADS * ATT_V_DIM
SSM_HEADS = 32
SSM_HEAD_DIM = 64
SSM_WIDTH = SSM_HEADS * SSM_HEAD_DIM
SSM_GROUPS = 4
SSM_STATE = 128
SSM_CONV = 4
SSM_CHUNK = 128
SSM_BC_WIDTH = 2 * SSM_GROUPS * SSM_STATE
PROJ_MAIN = 2 * ATT_WIDTH + ATT_WIDTH + SSM_WIDTH + SSM_WIDTH + SSM_BC_WIDTH
RMS_EPS = 1e-6
SUB_EPS = 1e-5
LOG2E = 1.4426950408889634
LANES = 128
SUBLANES = 8
ONES_ROWS = 16
BLOCKS_PER_ITER = 2
EARLY_BLOCKS = 1
HEADS_PER_GROUP = SSM_HEADS // SSM_GROUPS
PAIR_WIDTH = 2 * SSM_HEAD_DIM
NEG_INF = float("-inf")

COL_Q, COL_K, COL_V = 0, ATT_WIDTH, 2 * ATT_WIDTH
COL_Z = 3 * ATT_WIDTH
COL_XS = COL_Z + SSM_WIDTH
COL_BC = COL_XS + SSM_WIDTH

VMEM_LIMIT = 56 * 1024 * 1024


def _params(semantics):
    return pltpu.CompilerParams(dimension_semantics=semantics, vmem_limit_bytes=VMEM_LIMIT)


def _sigmoid(x):
    return 1.0 / (1.0 + jnp.exp(-x))


def _rmsnorm(x, w, eps):
    ms = jnp.mean(x * x, axis=-1, keepdims=True)
    return (x * lax.rsqrt(ms + eps)) * w


def _lane_tile(x, width):
    return x if width == LANES else jnp.concatenate([x] * (width // LANES), axis=1)


_CONTRACT_LAST = (((1,), (1,)), ((), ()))


def _in_proj_kernel(x_ref, nw_ref, wt_ref, wdt_ref, cs_ref, proj_ref, dt_ref, h_ref):
    def project(hb):
        acc = lax.dot_general(hb, wt_ref[...].astype(BF16), _CONTRACT_LAST,
                              preferred_element_type=F32)
        proj_ref[...] = (acc * cs_ref[...]).astype(BF16)

    @pl.when(pl.program_id(1) == 0)
    def _():
        hb = _rmsnorm(x_ref[...], nw_ref[...], RMS_EPS).astype(BF16)
        h_ref[...] = hb
        dt_ref[...] = lax.dot_general(hb, wdt_ref[...], _CONTRACT_LAST,
                                      preferred_element_type=F32)
        project(hb)

    @pl.when(pl.program_id(1) > 0)
    def _():
        project(h_ref[...])


def _in_proj(x2d, norm_w, w_in_t, w_dt_t, col_scale, tm=1024, tn=1024):
    m = x2d.shape[0]
    return pl.pallas_call(
        _in_proj_kernel,
        grid=(m // tm, PROJ_MAIN // tn),
        in_specs=[
            pl.BlockSpec((tm, D_MODEL), lambda i, j: (i, 0)),
            pl.BlockSpec((1, D_MODEL), lambda i, j: (0, 0)),
            pl.BlockSpec((tn, D_MODEL), lambda i, j: (j, 0)),
            pl.BlockSpec((LANES, D_MODEL), lambda i, j: (0, 0)),
            pl.BlockSpec((1, tn), lambda i, j: (0, j)),
        ],
        out_specs=[
            pl.BlockSpec((tm, tn), lambda i, j: (i, j)),
            pl.BlockSpec((tm, LANES), lambda i, j: (i, 0)),
        ],
        out_shape=[
            jax.ShapeDtypeStruct((m, PROJ_MAIN), BF16),
            jax.ShapeDtypeStruct((m, LANES), F32),
        ],
        scratch_shapes=[pltpu.VMEM((tm, D_MODEL), BF16)],
        compiler_params=_params(("parallel", "arbitrary")),
        name="in_proj",
    )(x2d, norm_w, w_in_t, w_dt_t, col_scale)


def _attn_kernel(slope_ref, lamv_ref, subw_ref, q_ref, k_ref, v_ref, wg_ref, wu_ref,
                 o_ref, wgb_ref, wub_ref,
                 vt_scr, qt_scr, kaug_scr, st_scr, m_scr, acc_scr,
                 *, tq, tk, qc, tiles, lambda_init):
    h = pl.program_id(1)
    step = pl.program_id(2)
    slope2 = slope_ref[h]
    aug_lane = lax.broadcasted_iota(jnp.int32, (tk, ATT_QK_DIM), 1)

    @pl.when(step == 0)
    def _():
        bias = slope2 * lax.broadcasted_iota(jnp.int32, (tk, ATT_QK_DIM), 0).astype(F32)
        hi = bias.astype(BF16).astype(F32)
        mid = (bias - hi).astype(BF16).astype(F32)
        lo = (bias - hi) - mid
        kaug_scr[...] = jnp.where(aug_lane == 0, hi, jnp.where(
            aug_lane == 1, mid, jnp.where(aug_lane == 2, lo, 0.0))).astype(BF16)

    row_minus_col = (lax.broadcasted_iota(jnp.int32, (tk, qc), 0)
                     - lax.broadcasted_iota(jnp.int32, (tk, qc), 1))
    pick = jnp.where(lax.broadcasted_iota(jnp.int32, (ATT_QK_DIM, tq), 0) < 3, 1.0, 0.0).astype(BF16)
    n_qchunks = tq // qc
    lv = lamv_ref[...]
    lam = (jnp.exp(jnp.sum(lv[0:1] * lv[1:2], axis=1, keepdims=True))
           - jnp.exp(jnp.sum(lv[2:3] * lv[3:4], axis=1, keepdims=True)) + lambda_init)

    def block_chains(d=None):
        out = []
        for u in range(n_qchunks):
            if d is not None and d * tk >= (u + 1) * qc:
                continue
            thr = None if d is None or (d + 1) * tk <= u * qc else u * qc - d * tk
            out += [(u, c, thr) for c in range(2)]
        return out

    def q_tile(t):
        rows = slice(t * tq, (t + 1) * tq)
        qi = step * tiles + t

        qt = q_ref[rows, :].T
        for c in range(2):
            qt_scr[t, c] = jnp.concatenate([qt[c * ATT_QK_DIM:(c + 1) * ATT_QK_DIM, :], pick], axis=0)
        m_scr[t] = jnp.full(m_scr.shape[1:], NEG_INF, F32)
        acc_scr[t] = jnp.zeros(acc_scr.shape[1:], F32)

        def scores(j, c, u):
            k0 = j * tk if isinstance(j, int) else pl.multiple_of(j * tk, tk)
            dims = slice(c * ATT_QK_DIM, (c + 1) * ATT_QK_DIM)
            k_aug = jnp.concatenate([k_ref[pl.ds(k0, tk), dims], kaug_scr[...]], axis=1)
            return jnp.dot(k_aug, qt_scr[t, c, :, u * qc:(u + 1) * qc], preferred_element_type=F32)

        def early_slot(blk, u, c):
            return (blk * n_qchunks + u) * 2 + c

        def prefetch_scores(j):
            for blk in range(EARLY_BLOCKS):
                for u, c, _ in block_chains():
                    st_scr[early_slot(blk, u, c)] = scores(j + blk, c, u)

        def chain(j, c, u, thr, slot=None):
            cs = slice(u * qc, (u + 1) * qc)
            st = scores(j, c, u) if slot is None else st_scr[slot]
            if thr is not None:
                st = jnp.where(row_minus_col <= thr, st, NEG_INF)
            off = slope2 * jnp.asarray(j * tk, F32)
            m_old = m_scr[t, c, 0:1, cs]
            m_new = jnp.maximum(m_old, jnp.max(st, axis=0, keepdims=True) + off)
            alpha = jnp.exp2(m_old - m_new)
            pt = jnp.exp2(st - (m_new - off))
            m_scr[t, c, :, cs] = jnp.broadcast_to(m_new, (SUBLANES, qc))
            acc_scr[t, c, :, cs] = alpha * acc_scr[t, c, :, cs] + jnp.dot(
                vt_scr[j], pt.astype(BF16), preferred_element_type=F32)

        def run_block(j, chains, early_blk):
            for u, c, thr in chains:
                chain(j, c, u, thr, None if early_blk is None else early_slot(early_blk, u, c))

        def body(jj, carry):
            for i in range(BLOCKS_PER_ITER):
                run_block(BLOCKS_PER_ITER * jj + i, block_chains(), i if i < EARLY_BLOCKS else None)
            prefetch_scores(BLOCKS_PER_ITER * (jj + 1))
            return carry

        n_full = (qi * tq) // tk
        prefetch_scores(0)
        for d in range(tq // tk):
            vt = v_ref[pl.ds(pl.multiple_of((n_full + d) * tk, tk), tk), :].T
            vt_scr[n_full + d] = jnp.concatenate([vt, jnp.ones((ONES_ROWS, tk), BF16)], axis=0)
        lax.fori_loop(0, n_full // BLOCKS_PER_ITER, body, 0)
        for d in range(tq // tk):
            run_block(n_full + d, block_chains(d), d if d < EARLY_BLOCKS else None)

        inv_l0 = 1.0 / acc_scr[t, 0, ATT_V_DIM:ATT_V_DIM + 1, :]
        inv_l1 = lam / acc_scr[t, 1, ATT_V_DIM:ATT_V_DIM + 1, :]
        ot = (acc_scr[t, 0, :ATT_V_DIM, :] * inv_l0
              - acc_scr[t, 1, :ATT_V_DIM, :] * inv_l1)
        ms = jnp.mean(ot * ot, axis=0, keepdims=True)
        ot = (ot * lax.rsqrt(ms + SUB_EPS)) * _lane_tile(subw_ref[...], tq)
        o_ref[rows, :] = (ot * (1.0 - lambda_init)).astype(BF16).T

    for t in range(tiles):
        q_tile(t)

    wgb_ref[...] = wg_ref[...].astype(BF16)
    wub_ref[...] = wu_ref[...].astype(BF16)


def _attention(proj, w_gate, w_up, slopes2, lamv, subw_col, batch, seq, lambda_init,
               tq=1024, tk=256, qc=256, tiles=4):
    assert tq % (BLOCKS_PER_ITER * tk) == 0 and tq % qc == 0 and seq % (tiles * tq) == 0
    nsteps = seq // (tiles * tq)
    kern = functools.partial(_attn_kernel, tq=tq, tk=tk, qc=qc, tiles=tiles,
                             lambda_init=lambda_init)
    kb, vb = COL_K // ATT_V_DIM, COL_V // ATT_V_DIM
    n_steps = batch * ATT_HEADS * nsteps
    w_rows = w_gate.shape[0] // n_steps
    assert w_rows * n_steps == w_gate.shape[0] and w_rows % ONES_ROWS == 0
    w_spec = pl.BlockSpec((w_rows, w_gate.shape[1]),
                          lambda b, h, i: ((b * ATT_HEADS + h) * nsteps + i, 0))
    return pl.pallas_call(
        kern,
        grid=(batch, ATT_HEADS, nsteps),
        in_specs=[
            pl.BlockSpec(memory_space=pltpu.SMEM),
            pl.BlockSpec((4, ATT_QK_DIM), lambda b, h, i: (0, 0)),
            pl.BlockSpec((ATT_V_DIM, LANES), lambda b, h, i: (0, 0)),
            pl.BlockSpec((tiles * tq, ATT_V_DIM), lambda b, h, i: (b * nsteps + i, h)),
            pl.BlockSpec((seq, ATT_V_DIM), lambda b, h, i: (b, kb + h)),
            pl.BlockSpec((seq, ATT_V_DIM), lambda b, h, i: (b, vb + h)),
            w_spec,
            w_spec,
        ],
        out_specs=[
            pl.BlockSpec((tiles * tq, ATT_V_DIM), lambda b, h, i: (b * nsteps + i, h)),
            w_spec,
            w_spec,
        ],
        out_shape=[
            jax.ShapeDtypeStruct((batch * seq, ATT_WIDTH), BF16),
            jax.ShapeDtypeStruct(w_gate.shape, BF16),
            jax.ShapeDtypeStruct(w_up.shape, BF16),
        ],
        scratch_shapes=[
            pltpu.VMEM((seq // tk, ATT_V_DIM + ONES_ROWS, tk), BF16),
            pltpu.VMEM((tiles, 2, 2 * ATT_QK_DIM, tq), BF16),
            pltpu.VMEM((tk, ATT_QK_DIM), BF16),
            pltpu.VMEM((EARLY_BLOCKS * (tq // qc) * 2, tk, qc), F32),
            pltpu.VMEM((tiles, 2, SUBLANES, tq), F32),
            pltpu.VMEM((tiles, 2, ATT_V_DIM + ONES_ROWS, tq), F32),
        ],
        compiler_params=_params(("parallel", "parallel", "arbitrary")),
        name="diff_attention",
    )(slopes2, lamv, subw_col, proj, proj, proj, w_gate, w_up)


def _ssd_kernel(z_ref, xs_ref, bc_ref, dt_ref, cwx_ref, cwbc_ref, cbx_ref, cbbc_ref,
                dtb_ref, alog_ref, dskip_ref, nw_ref, out_ref,
                tailx, tailbc, state, yscr):
    @pl.when(pl.program_id(1) == 0)
    def _():
        tailx[...] = jnp.zeros(tailx.shape, F32)
        tailbc[...] = jnp.zeros(tailbc.shape, F32)
        state[...] = jnp.zeros(state.shape, F32)

    for s in range(yscr.shape[0]):
        rows = pl.ds(s * SSM_CHUNK, SSM_CHUNK)
        _ssd_chunk(z_ref.at[rows], xs_ref.at[rows], bc_ref.at[rows], dt_ref.at[rows],
                   cwx_ref, cwbc_ref, cbx_ref, cbbc_ref, dtb_ref, alog_ref, dskip_ref, nw_ref,
                   out_ref.at[rows], tailx, tailbc, state, yscr.at[s])


def _ssd_chunk(z_ref, xs_ref, bc_ref, dt_ref, cwx_ref, cwbc_ref, cbx_ref, cbbc_ref,
               dtb_ref, alog_ref, dskip_ref, nw_ref, out_ref,
               tailx, tailbc, state, yscr):
    L = SSM_CHUNK

    def conv_silu(u_ref, tail, w_ref, b_ref):
        u = u_ref[...].astype(F32)
        full = jnp.concatenate([tail[...], u], axis=0)
        acc = b_ref[...] + w_ref[SSM_CONV - 1:SSM_CONV, :] * u
        for k in range(SSM_CONV - 1):
            lo = SUBLANES - (SSM_CONV - 1) + k
            acc = acc + w_ref[k:k + 1, :] * full[lo:lo + L]
        tail[...] = u[L - SUBLANES:L]
        return acc * _sigmoid(acc)

    xs = conv_silu(xs_ref, tailx, cwx_ref, cbx_ref)
    bc = conv_silu(bc_ref, tailbc, cwbc_ref, cbbc_ref)
    xs_b = xs.astype(BF16)
    bc_b = bc.astype(BF16)

    dtr = dt_ref[...] + dtb_ref[...]
    dt = jnp.maximum(dtr, 0.0) + jnp.log1p(jnp.exp(-jnp.abs(dtr)))
    acs = dt * (-jnp.exp(alog_ref[...]))
    rowi = lax.broadcasted_iota(jnp.int32, (L, LANES), 0)
    shift = 1
    while shift < L:
        acs = acs + jnp.where(rowi >= shift, pltpu.roll(acs, shift, axis=0), 0.0)
        shift *= 2
    acs = acs * LOG2E
    acs_t = acs.T
    dt_t = dt.T
    exp_a = jnp.exp2(acs)
    w_t = dt_t * jnp.exp2(acs_t[:, L - 1:L] - acs_t)
    cd_row = jnp.exp2(acs[L - 1:L, :])
    src_t = acs_t - jnp.log2(dt_t)

    li = lax.broadcasted_iota(jnp.int32, (L, L), 0)
    si = lax.broadcasted_iota(jnp.int32, (L, L), 1)
    causal = si <= li
    lane = lax.broadcasted_iota(jnp.int32, (L, PAIR_WIDTH), 1)
    lo_half = lane < SSM_HEAD_DIM
    lane_row = lax.broadcasted_iota(jnp.int32, (1, PAIR_WIDTH), 1) < SSM_HEAD_DIM
    zero_b = jnp.zeros((L, PAIR_WIDTH), BF16)

    gn = SSM_GROUPS * SSM_STATE
    for g in range(SSM_GROUPS):
        b_g = bc[:, g * SSM_STATE:(g + 1) * SSM_STATE]
        c_g = bc[:, gn + g * SSM_STATE:gn + (g + 1) * SSM_STATE]
        cb = lax.dot_general(bc_b[:, gn + g * SSM_STATE:gn + (g + 1) * SSM_STATE],
                             bc_b[:, g * SSM_STATE:(g + 1) * SSM_STATE],
                             (((1,), (1,)), ((), ())), preferred_element_type=F32)
        b_gt = b_g.T
        for pr in range(HEADS_PER_GROUP // 2):
            pair = g * (HEADS_PER_GROUP // 2) + pr
            lhs_y, lhs_s = [], []
            for hh in (2 * pair, 2 * pair + 1):
                seg = acs[:, hh:hh + 1] - src_t[hh:hh + 1, :]
                lhs_y.append((cb * jnp.exp2(jnp.where(causal, seg, NEG_INF))).astype(BF16))
                lhs_s.append((b_gt * w_t[hh:hh + 1, :]).astype(BF16))
            for hh in (2 * pair, 2 * pair + 1):
                lhs_y.append((c_g * exp_a[:, hh:hh + 1]).astype(BF16))
            x_p = xs_b[:, pair * PAIR_WIDTH:(pair + 1) * PAIR_WIDTH]
            x_lo = jnp.where(lo_half, x_p, zero_b)
            x_hi = jnp.where(lo_half, zero_b, x_p)
            st = state[pair]
            st_b = st.astype(BF16)
            st_lo = jnp.where(lo_half, st_b, zero_b)
            st_hi = jnp.where(lo_half, zero_b, st_b)
            y = jnp.dot(jnp.concatenate(lhs_y, axis=1),
                        jnp.concatenate([x_lo, x_hi, st_lo, st_hi], axis=0),
                        preferred_element_type=F32)
            yscr[:, pair * PAIR_WIDTH:(pair + 1) * PAIR_WIDTH] = y
            new = jnp.dot(jnp.concatenate(lhs_s, axis=1),
                          jnp.concatenate([x_lo, x_hi], axis=0),
                          preferred_element_type=F32)
            cd = jnp.where(lane_row, cd_row[:, 2 * pair:2 * pair + 1],
                           cd_row[:, 2 * pair + 1:2 * pair + 2])
            state[pair] = st * cd + new

    y = yscr[...] + dskip_ref[...] * xs
    z = z_ref[...].astype(F32)
    y = y * (z * _sigmoid(z))
    gw = SSM_WIDTH // SSM_GROUPS
    for g in range(SSM_GROUPS):
        cols = slice(g * gw, (g + 1) * gw)
        out_ref[:, cols] = _rmsnorm(y[:, cols], nw_ref[:, cols], SUB_EPS).astype(BF16)


def _ssd(proj, dt_raw, conv_w, conv_b, dt_bias, a_log, d_skip_vec, norm_w, batch, seq, chunks=4):
    L = chunks * SSM_CHUNK
    nc = seq // L
    assert nc * L == seq
    row = lambda b, c: b * nc + c
    full = lambda shape: pl.BlockSpec(shape, lambda b, c: (0, 0))
    cwx, cwbc = conv_w[:, :SSM_WIDTH], conv_w[:, SSM_WIDTH:]
    cbx, cbbc = conv_b[:, :SSM_WIDTH], conv_b[:, SSM_WIDTH:]
    return pl.pallas_call(
        _ssd_kernel,
        grid=(batch, nc),
        in_specs=[
            pl.BlockSpec((L, SSM_WIDTH), lambda b, c: (row(b, c), COL_Z // SSM_WIDTH)),
            pl.BlockSpec((L, SSM_WIDTH), lambda b, c: (row(b, c), COL_XS // SSM_WIDTH)),
            pl.BlockSpec((L, SSM_BC_WIDTH), lambda b, c: (row(b, c), COL_BC // SSM_BC_WIDTH)),
            pl.BlockSpec((L, LANES), lambda b, c: (row(b, c), 0)),
            full((SSM_CONV, SSM_WIDTH)),
            full((SSM_CONV, SSM_BC_WIDTH)),
            full((1, SSM_WIDTH)),
            full((1, SSM_BC_WIDTH)),
            full((1, LANES)),
            full((1, LANES)),
            full((1, SSM_WIDTH)),
            full((1, SSM_WIDTH)),
        ],
        out_specs=pl.BlockSpec((L, SSM_WIDTH), lambda b, c: (row(b, c), 0)),
        out_shape=jax.ShapeDtypeStruct((batch * seq, SSM_WIDTH), BF16),
        scratch_shapes=[
            pltpu.VMEM((SUBLANES, SSM_WIDTH), F32),
            pltpu.VMEM((SUBLANES, SSM_BC_WIDTH), F32),
            pltpu.VMEM((SSM_HEADS // 2, SSM_STATE, PAIR_WIDTH), F32),
            pltpu.VMEM((chunks, SSM_CHUNK, SSM_WIDTH), F32),
        ],
        compiler_params=_params(("parallel", "arbitrary")),
        name="ssd_mixer",
    )(proj, proj, proj, dt_raw, cwx, cwbc, cbx, cbbc, dt_bias, a_log, d_skip_vec, norm_w)


def _out_proj_kernel(att_ref, ssm_ref, w_ref, x_ref, wd_ref, x1_ref, wdb_ref, wb_ref):
    @pl.when(pl.program_id(1) == 0)
    def _():
        wb_ref[...] = w_ref[...].astype(BF16)

    acc = jnp.dot(att_ref[...], wb_ref[0], preferred_element_type=F32)
    acc = acc + jnp.dot(ssm_ref[...], wb_ref[1], preferred_element_type=F32)
    x1_ref[...] = x_ref[...] + acc
    wdb_ref[...] = wd_ref[...].astype(BF16)


def _out_proj(att, ssm, w_out2, x2d, w_down, tm=512, tn=1024):
    m = att.shape[0]
    ni = m // tm
    n_steps = (D_MODEL // tn) * ni
    d_ff = w_down.shape[0]
    wd_rows = d_ff // n_steps
    assert wd_rows * n_steps == d_ff and wd_rows % ONES_ROWS == 0
    return pl.pallas_call(
        _out_proj_kernel,
        grid=(D_MODEL // tn, ni),
        in_specs=[
            pl.BlockSpec((tm, ATT_WIDTH), lambda j, i: (i, 0)),
            pl.BlockSpec((tm, SSM_WIDTH), lambda j, i: (i, 0)),
            pl.BlockSpec((2, ATT_WIDTH, tn), lambda j, i: (0, 0, j), pipeline_mode=pl.Buffered(1)),
            pl.BlockSpec((tm, tn), lambda j, i: (i, j)),
            pl.BlockSpec((wd_rows, D_MODEL), lambda j, i: (j * ni + i, 0)),
        ],
        out_specs=[
            pl.BlockSpec((tm, tn), lambda j, i: (i, j)),
            pl.BlockSpec((wd_rows, D_MODEL), lambda j, i: (j * ni + i, 0)),
        ],
        out_shape=[
            jax.ShapeDtypeStruct((m, D_MODEL), F32),
            jax.ShapeDtypeStruct(w_down.shape, BF16),
        ],
        scratch_shapes=[pltpu.VMEM((2, ATT_WIDTH, tn), BF16)],
        compiler_params=_params(("parallel", "arbitrary")),
        name="out_proj",
    )(att, ssm, w_out2, x2d, w_down)


def _gate_up_kernel(x_ref, nw_ref, wg_ref, wu_ref, a_ref, h_ref):
    def gated(h):
        g = jnp.dot(h, wg_ref[...], preferred_element_type=F32)
        u = jnp.dot(h, wu_ref[...], preferred_element_type=F32)
        a_ref[...] = ((g * _sigmoid(g)) * u).astype(BF16)

    @pl.when(pl.program_id(1) == 0)
    def _():
        h = _rmsnorm(x_ref[...], nw_ref[...], RMS_EPS).astype(BF16)
        h_ref[...] = h
        gated(h)

    @pl.when(pl.program_id(1) > 0)
    def _():
        gated(h_ref[...])


def _gate_up(x1, norm_w, w_gate, w_up, tm=1024, tn=512):
    m = x1.shape[0]
    n = w_gate.shape[1]
    return pl.pallas_call(
        _gate_up_kernel,
        grid=(m // tm, n // tn),
        in_specs=[
            pl.BlockSpec((tm, D_MODEL), lambda i, j: (i, 0)),
            pl.BlockSpec((1, D_MODEL), lambda i, j: (0, 0)),
            pl.BlockSpec((D_MODEL, tn), lambda i, j: (0, j)),
            pl.BlockSpec((D_MODEL, tn), lambda i, j: (0, j)),
        ],
        out_specs=pl.BlockSpec((tm, tn), lambda i, j: (i, j)),
        out_shape=jax.ShapeDtypeStruct((m, n), BF16),
        scratch_shapes=[pltpu.VMEM((tm, D_MODEL), BF16)],
        compiler_params=_params(("parallel", "arbitrary")),
        name="gate_up",
    )(x1, norm_w, w_gate, w_up)


def _down_kernel(a_ref, w_ref, x_ref, nw_ref, o_ref, x2_ref):
    j = pl.program_id(1)
    x2_ref[j] = x_ref[...] + jnp.dot(a_ref[...], w_ref[...], preferred_element_type=F32)

    @pl.when(j == pl.num_programs(1) - 1)
    def _():
        x2 = jnp.concatenate([x2_ref[t] for t in range(x2_ref.shape[0])], axis=1)
        o_ref[...] = _rmsnorm(x2, nw_ref[...], RMS_EPS)


def _down(a, w_down, x1, norm_w, tm=512, tn=1024):
    m, kdim = a.shape
    return pl.pallas_call(
        _down_kernel,
        grid=(m // tm, D_MODEL // tn),
        in_specs=[
            pl.BlockSpec((tm, kdim), lambda i, j: (i, 0)),
            pl.BlockSpec((kdim, tn), lambda i, j: (0, j)),
            pl.BlockSpec((tm, tn), lambda i, j: (i, j)),
            pl.BlockSpec((1, D_MODEL), lambda i, j: (0, 0)),
        ],
        out_specs=pl.BlockSpec((tm, D_MODEL), lambda i, j: (i, 0)),
        out_shape=jax.ShapeDtypeStruct((m, D_MODEL), F32),
        scratch_shapes=[pltpu.VMEM((D_MODEL // tn, tm, tn), F32)],
        compiler_params=_params(("parallel", "arbitrary")),
        name="down_proj",
    )(a, w_down, x1, norm_w)


def _alibi_slopes_log2(n):
    start = 2.0 ** (-8.0 / n)
    return jnp.asarray([start ** (i + 1) for i in range(n)], dtype=F32) * LOG2E


def _pad_lanes(v):
    return jnp.pad(v.astype(F32), (0, LANES - v.shape[0])).reshape(1, LANES)


def kernel(x, norm_mix_w, w_in, lambda_q1, lambda_k1, lambda_q2, lambda_k2, subln_w, conv_w, conv_b, dt_bias, a_log, d_skip, ssm_norm_w, w_out, norm_ffn_w, w_gate, w_up, w_down, norm_final_w):
    batch, seq, _ = x.shape
    assert w_in.shape[0] == 1, "single-layer block"
    layer = 0
    x2d = x.reshape(batch * seq, D_MODEL)
    col_scale = jnp.concatenate([
        jnp.full((ATT_WIDTH,), ATT_QK_DIM ** -0.5 * LOG2E, F32),
        jnp.ones((PROJ_MAIN - ATT_WIDTH,), F32)]).reshape(1, PROJ_MAIN)
    slopes2 = _alibi_slopes_log2(ATT_HEADS)
    lambda_init = 0.8 - 0.6 * math.exp(-0.3 * layer)
    w_in_t = w_in[layer].T
    w_dt_t = jnp.pad(w_in_t[PROJ_MAIN:], ((0, LANES - SSM_HEADS), (0, 0))).astype(BF16)
    proj, dt_raw = _in_proj(x2d, norm_mix_w[layer].reshape(1, D_MODEL), w_in_t, w_dt_t, col_scale)
    lamv = jnp.stack([lambda_q1[layer], lambda_k1[layer], lambda_q2[layer], lambda_k2[layer]]).astype(F32)
    subw_col = jnp.broadcast_to(subln_w[layer].astype(F32).reshape(ATT_V_DIM, 1), (ATT_V_DIM, LANES))
    att, w_gate_b, w_up_b = _attention(proj, w_gate[layer], w_up[layer], slopes2, lamv, subw_col,
                                       batch, seq, lambda_init)
    ssm = _ssd(proj, dt_raw, conv_w[layer], conv_b[layer].reshape(1, -1),
               _pad_lanes(dt_bias[layer]), _pad_lanes(a_log[layer]),
               jnp.repeat(d_skip[layer].astype(F32), SSM_HEAD_DIM).reshape(1, SSM_WIDTH),
               ssm_norm_w[layer].reshape(1, SSM_WIDTH), batch, seq)
    w_out2 = w_out[layer].reshape(2, ATT_WIDTH, D_MODEL)
    x1, w_down_b = _out_proj(att, ssm, w_out2, x2d, w_down[layer])
    a = _gate_up(x1, norm_ffn_w[layer].reshape(1, D_MODEL), w_gate_b, w_up_b)
    out = _down(a, w_down_b, x1, norm_final_w.reshape(1, D_MODEL))
    return out.reshape(batch, seq, D_MODEL)
```

```python
import functools
import math

import jax
import jax.numpy as jnp
from jax import lax
from jax.experimental import pallas as pl
from jax.experimental.pallas import tpu as pltpu

F32 = jnp.float32
BF16 = jnp.bfloat16

D_MODEL = 2048
ATT_HEADS = 8
ATT_QK_DIM = 128
ATT_V_DIM = 2 * ATT_QK_DIM
ATT_WIDTH = ATT_HEADS * ATT_V_DIM
SSM_HEADS = 32
SSM_HEAD_DIM = 64
SSM_WIDTH = SSM_HEADS * SSM_HEAD_DIM
SSM_GROUPS = 4
SSM_STATE = 128
SSM_CONV = 4
SSM_CHUNK = 128
SSM_BC_WIDTH = 2 * SSM_GROUPS * SSM_STATE
PROJ_MAIN = 2 * ATT_WIDTH + ATT_WIDTH + SSM_WIDTH + SSM_WIDTH + SSM_BC_WIDTH
RMS_EPS = 1e-6
SUB_EPS = 1e-5
LOG2E = 1.4426950408889634
LANES = 128
SUBLANES = 8
ONES_ROWS = 16
BLOCKS_PER_ITER = 4
EARLY_BLOCKS = 1
HEADS_PER_GROUP = SSM_HEADS // SSM_GROUPS
PAIR_WIDTH = 2 * SSM_HEAD_DIM
NEG_INF = float("-inf")

COL_Q, COL_K, COL_V = 0, ATT_WIDTH, 2 * ATT_WIDTH
COL_Z = 3 * ATT_WIDTH
COL_XS = COL_Z + SSM_WIDTH
COL_BC = COL_XS + SSM_WIDTH

VMEM_LIMIT = 56 * 1024 * 1024


def _params(semantics):
    return pltpu.CompilerParams(dimension_semantics=semantics, vmem_limit_bytes=VMEM_LIMIT)


def _sigmoid(x):
    return 1.0 / (1.0 + jnp.exp(-x))


def _rmsnorm(x, w, eps):
    ms = jnp.mean(x * x, axis=-1, keepdims=True)
    return (x * lax.rsqrt(ms + eps)) * w


def _lane_tile(x, width):
    return x if width == LANES else jnp.concatenate([x] * (width // LANES), axis=1)


_CONTRACT_LAST = (((1,), (1,)), ((), ()))


def _in_proj_kernel(x_ref, nw_ref, wt_ref, wdt_ref, cs_ref, proj_ref, dt_ref, h_ref):
    def project(hb):
        acc = lax.dot_general(hb, wt_ref[...].astype(BF16), _CONTRACT_LAST,
                              preferred_element_type=F32)
        proj_ref[...] = (acc * cs_ref[...]).astype(BF16)

    @pl.when(pl.program_id(1) == 0)
    def _():
        hb = _rmsnorm(x_ref[...], nw_ref[...], RMS_EPS).astype(BF16)
        h_ref[...] = hb
        dt_ref[...] = lax.dot_general(hb, wdt_ref[...], _CONTRACT_LAST,
                                      preferred_element_type=F32)
        project(hb)

    @pl.when(pl.program_id(1) > 0)
    def _():
        project(h_ref[...])


def _in_proj(x2d, norm_w, w_in_t, w_dt_t, col_scale, tm=1024, tn=1024):
    m = x2d.shape[0]
    return pl.pallas_call(
        _in_proj_kernel,
        grid=(m // tm, PROJ_MAIN // tn),
        in_specs=[
            pl.BlockSpec((tm, D_MODEL), lambda i, j: (i, 0)),
            pl.BlockSpec((1, D_MODEL), lambda i, j: (0, 0)),
            pl.BlockSpec((tn, D_MODEL), lambda i, j: (j, 0)),
            pl.BlockSpec((LANES, D_MODEL), lambda i, j: (0, 0)),
            pl.BlockSpec((1, tn), lambda i, j: (0, j)),
        ],
        out_specs=[
            pl.BlockSpec((tm, tn), lambda i, j: (i, j)),
            pl.BlockSpec((tm, LANES), lambda i, j: (i, 0)),
        ],
        out_shape=[
            jax.ShapeDtypeStruct((m, PROJ_MAIN), BF16),
            jax.ShapeDtypeStruct((m, LANES), F32),
        ],
        scratch_shapes=[pltpu.VMEM((tm, D_MODEL), BF16)],
        compiler_params=_params(("parallel", "arbitrary")),
        name="in_proj",
    )(x2d, norm_w, w_in_t, w_dt_t, col_scale)


def _attn_kernel(slope_ref, lamv_ref, subw_ref, q_ref, k_ref, v_ref, o_ref,
                 vt_scr, qt_scr, kaug_scr, st_scr, m_scr, acc_scr,
                 *, tq, tk, qc, tiles, lambda_init):
    h = pl.program_id(1)
    step = pl.program_id(2)
    slope2 = slope_ref[h]
    aug_lane = lax.broadcasted_iota(jnp.int32, (tk, ATT_QK_DIM), 1)

    @pl.when(step == 0)
    def _():
        bias = slope2 * lax.broadcasted_iota(jnp.int32, (tk, ATT_QK_DIM), 0).astype(F32)
        hi = bias.astype(BF16).astype(F32)
        mid = (bias - hi).astype(BF16).astype(F32)
        lo = (bias - hi) - mid
        kaug_scr[...] = jnp.where(aug_lane == 0, hi, jnp.where(
            aug_lane == 1, mid, jnp.where(aug_lane == 2, lo, 0.0))).astype(BF16)

    row_minus_col = (lax.broadcasted_iota(jnp.int32, (tk, qc), 0)
                     - lax.broadcasted_iota(jnp.int32, (tk, qc), 1))
    pick = jnp.where(lax.broadcasted_iota(jnp.int32, (ATT_QK_DIM, tq), 0) < 3, 1.0, 0.0).astype(BF16)
    n_qchunks = tq // qc
    lv = lamv_ref[...]
    lam = (jnp.exp(jnp.sum(lv[0:1] * lv[1:2], axis=1, keepdims=True))
           - jnp.exp(jnp.sum(lv[2:3] * lv[3:4], axis=1, keepdims=True)) + lambda_init)

    def block_chains(d=None):
        out = []
        for u in range(n_qchunks):
            if d is not None and d * tk >= (u + 1) * qc:
                continue
            thr = None if d is None or (d + 1) * tk <= u * qc else u * qc - d * tk
            out += [(u, c, thr) for c in range(2)]
        return out

    def q_tile(t):
        rows = slice(t * tq, (t + 1) * tq)
        qi = step * tiles + t

        qt = q_ref[rows, :].T
        for c in range(2):
            qt_scr[t, c] = jnp.concatenate([qt[c * ATT_QK_DIM:(c + 1) * ATT_QK_DIM, :], pick], axis=0)
        m_scr[t] = jnp.full(m_scr.shape[1:], NEG_INF, F32)
        acc_scr[t] = jnp.zeros(acc_scr.shape[1:], F32)

        def scores(j, c, u):
            k0 = j * tk if isinstance(j, int) else pl.multiple_of(j * tk, tk)
            dims = slice(c * ATT_QK_DIM, (c + 1) * ATT_QK_DIM)
            k_aug = jnp.concatenate([k_ref[pl.ds(k0, tk), dims], kaug_scr[...]], axis=1)
            return jnp.dot(k_aug, qt_scr[t, c, :, u * qc:(u + 1) * qc], preferred_element_type=F32)

        def early_slot(blk, u, c):
            return (blk * n_qchunks + u) * 2 + c

        def prefetch_scores(j):
            for blk in range(EARLY_BLOCKS):
                for u, c, _ in block_chains():
                    st_scr[early_slot(blk, u, c)] = scores(j + blk, c, u)

        def chain(j, c, u, thr, slot=None, st=None):
            cs = slice(u * qc, (u + 1) * qc)
            if st is None:
                st = scores(j, c, u) if slot is None else st_scr[slot]
            if thr is not None:
                st = jnp.where(row_minus_col <= thr, st, NEG_INF)
            off = slope2 * jnp.asarray(j * tk, F32)
            m_old = m_scr[t, c, 0:1, cs]
            m_new = jnp.maximum(m_old, jnp.max(st, axis=0, keepdims=True) + off)
            alpha = jnp.exp2(m_old - m_new)
            pt = jnp.exp2(st - (m_new - off))
            m_scr[t, c, :, cs] = jnp.broadcast_to(m_new, (SUBLANES, qc))
            acc_scr[t, c, :, cs] = alpha * acc_scr[t, c, :, cs] + jnp.dot(
                vt_scr[j], pt.astype(BF16), preferred_element_type=F32)

        def run_block(j, chains, early_blk):
            for u, c, thr in chains:
                chain(j, c, u, thr, None if early_blk is None else early_slot(early_blk, u, c))

        def body(jj, carry):
            chains = block_chains()
            st_cur = None
            for i in range(BLOCKS_PER_ITER):
                j = BLOCKS_PER_ITER * jj + i
                st_nxt = []
                for k, (u, c, _) in enumerate(chains):
                    st_nxt.append(scores(j + 1, c, u))
                    if st_cur is None:
                        chain(j, c, u, None, early_slot(0, u, c))
                    else:
                        chain(j, c, u, None, st=st_cur[k])
                    if i == BLOCKS_PER_ITER - 1:
                        st_scr[early_slot(0, u, c)] = st_nxt[k]
                st_cur = st_nxt
            return carry

        n_full = (qi * tq) // tk
        prefetch_scores(0)
        for d in range(tq // tk):
            vt = v_ref[pl.ds(pl.multiple_of((n_full + d) * tk, tk), tk), :].T
            vt_scr[n_full + d] = jnp.concatenate([vt, jnp.ones((ONES_ROWS, tk), BF16)], axis=0)
        lax.fori_loop(0, n_full // BLOCKS_PER_ITER, body, 0)
        diag = [block_chains(d) for d in range(tq // tk)]
        st_cur = None
        for d, chains in enumerate(diag):
            nxt = diag[d + 1] if d + 1 < len(diag) else []
            st_nxt = []
            for k, (u, c, thr) in enumerate(chains):
                if k < len(nxt):
                    st_nxt.append(scores(n_full + d + 1, nxt[k][1], nxt[k][0]))
                if st_cur is None:
                    chain(n_full + d, c, u, thr, early_slot(0, u, c))
                else:
                    chain(n_full + d, c, u, thr, st=st_cur[k])
            st_cur = st_nxt

        inv_l0 = 1.0 / acc_scr[t, 0, ATT_V_DIM:ATT_V_DIM + 1, :]
        inv_l1 = lam / acc_scr[t, 1, ATT_V_DIM:ATT_V_DIM + 1, :]
        ot = (acc_scr[t, 0, :ATT_V_DIM, :] * inv_l0
              - acc_scr[t, 1, :ATT_V_DIM, :] * inv_l1)
        ms = jnp.mean(ot * ot, axis=0, keepdims=True)
        ot = (ot * lax.rsqrt(ms + SUB_EPS)) * _lane_tile(subw_ref[...], tq)
        o_ref[rows, :] = (ot * (1.0 - lambda_init)).astype(BF16).T

    for t in range(tiles):
        q_tile(t)


def _attention(proj, slopes2, lamv, subw_col, batch, seq, lambda_init,
               tq=1024, tk=256, qc=256, tiles=4):
    assert tq % (BLOCKS_PER_ITER * tk) == 0 and tq % qc == 0 and seq % (tiles * tq) == 0
    nsteps = seq // (tiles * tq)
    kern = functools.partial(_attn_kernel, tq=tq, tk=tk, qc=qc, tiles=tiles,
                             lambda_init=lambda_init)
    kb, vb = COL_K // ATT_V_DIM, COL_V // ATT_V_DIM
    return pl.pallas_call(
        kern,
        grid=(batch, ATT_HEADS, nsteps),
        in_specs=[
            pl.BlockSpec(memory_space=pltpu.SMEM),
            pl.BlockSpec((4, ATT_QK_DIM), lambda b, h, i: (0, 0)),
            pl.BlockSpec((ATT_V_DIM, LANES), lambda b, h, i: (0, 0)),
            pl.BlockSpec((tiles * tq, ATT_V_DIM), lambda b, h, i: (b * nsteps + i, h)),
            pl.BlockSpec((seq, ATT_V_DIM), lambda b, h, i: (b, kb + h)),
            pl.BlockSpec((seq, ATT_V_DIM), lambda b, h, i: (b, vb + h)),
        ],
        out_specs=pl.BlockSpec((tiles * tq, ATT_V_DIM), lambda b, h, i: (b * nsteps + i, h)),
        out_shape=jax.ShapeDtypeStruct((batch * seq, ATT_WIDTH), BF16),
        scratch_shapes=[
            pltpu.VMEM((seq // tk, ATT_V_DIM + ONES_ROWS, tk), BF16),
            pltpu.VMEM((tiles, 2, 2 * ATT_QK_DIM, tq), BF16),
            pltpu.VMEM((tk, ATT_QK_DIM), BF16),
            pltpu.VMEM((EARLY_BLOCKS * (tq // qc) * 2, tk, qc), F32),
            pltpu.VMEM((tiles, 2, SUBLANES, tq), F32),
            pltpu.VMEM((tiles, 2, ATT_V_DIM + ONES_ROWS, tq), F32),
        ],
        compiler_params=_params(("parallel", "parallel", "arbitrary")),
        name="diff_attention",
    )(slopes2, lamv, subw_col, proj, proj, proj)


def _ssd_kernel(z_ref, xs_ref, bc_ref, dt_ref, cwx_ref, cwbc_ref, cbx_ref, cbbc_ref,
                dtb_ref, alog_ref, dskip_ref, nw_ref, out_ref,
                tailx, tailbc, state, yscr):
    @pl.when(pl.program_id(1) == 0)
    def _():
        tailx[...] = jnp.zeros(tailx.shape, F32)
        tailbc[...] = jnp.zeros(tailbc.shape, F32)
        state[...] = jnp.zeros(state.shape, F32)

    for s in range(yscr.shape[0]):
        rows = pl.ds(s * SSM_CHUNK, SSM_CHUNK)
        _ssd_chunk(z_ref.at[rows], xs_ref.at[rows], bc_ref.at[rows], dt_ref.at[rows],
                   cwx_ref, cwbc_ref, cbx_ref, cbbc_ref, dtb_ref, alog_ref, dskip_ref, nw_ref,
                   out_ref.at[rows], tailx, tailbc, state, yscr.at[s])


def _ssd_chunk(z_ref, xs_ref, bc_ref, dt_ref, cwx_ref, cwbc_ref, cbx_ref, cbbc_ref,
               dtb_ref, alog_ref, dskip_ref, nw_ref, out_ref,
               tailx, tailbc, state, yscr):
    L = SSM_CHUNK

    def conv_silu(u_ref, tail, w_ref, b_ref):
        u = u_ref[...].astype(F32)
        full = jnp.concatenate([tail[...], u], axis=0)
        acc = b_ref[...] + w_ref[SSM_CONV - 1:SSM_CONV, :] * u
        for k in range(SSM_CONV - 1):
            lo = SUBLANES - (SSM_CONV - 1) + k
            acc = acc + w_ref[k:k + 1, :] * full[lo:lo + L]
        tail[...] = u[L - SUBLANES:L]
        return acc * _sigmoid(acc)

    xs = conv_silu(xs_ref, tailx, cwx_ref, cbx_ref)
    bc = conv_silu(bc_ref, tailbc, cwbc_ref, cbbc_ref)
    xs_b = xs.astype(BF16)
    bc_b = bc.astype(BF16)

    dtr = dt_ref[...] + dtb_ref[...]
    dt = jnp.maximum(dtr, 0.0) + jnp.log1p(jnp.exp(-jnp.abs(dtr)))
    acs = dt * (-jnp.exp(alog_ref[...]))
    rowi = lax.broadcasted_iota(jnp.int32, (L, LANES), 0)
    shift = 1
    while shift < L:
        acs = acs + jnp.where(rowi >= shift, pltpu.roll(acs, shift, axis=0), 0.0)
        shift *= 2
    acs = acs * LOG2E
    acs_t = acs.T
    dt_t = dt.T
    exp_a = jnp.exp2(acs)
    w_t = dt_t * jnp.exp2(acs_t[:, L - 1:L] - acs_t)
    cd_row = jnp.exp2(acs[L - 1:L, :])
    src_t = acs_t - jnp.log2(dt_t)

    li = lax.broadcasted_iota(jnp.int32, (L, L), 0)
    si = lax.broadcasted_iota(jnp.int32, (L, L), 1)
    causal = si <= li
    lane = lax.broadcasted_iota(jnp.int32, (L, PAIR_WIDTH), 1)
    lo_half = lane < SSM_HEAD_DIM
    lane_row = lax.broadcasted_iota(jnp.int32, (1, PAIR_WIDTH), 1) < SSM_HEAD_DIM
    zero_b = jnp.zeros((L, PAIR_WIDTH), BF16)

    gn = SSM_GROUPS * SSM_STATE
    for g in range(SSM_GROUPS):
        b_g = bc[:, g * SSM_STATE:(g + 1) * SSM_STATE]
        c_g = bc[:, gn + g * SSM_STATE:gn + (g + 1) * SSM_STATE]
        cb = lax.dot_general(bc_b[:, gn + g * SSM_STATE:gn + (g + 1) * SSM_STATE],
                             bc_b[:, g * SSM_STATE:(g + 1) * SSM_STATE],
                             (((1,), (1,)), ((), ())), preferred_element_type=F32)
        b_gt = b_g.T
        for pr in range(HEADS_PER_GROUP // 2):
            pair = g * (HEADS_PER_GROUP // 2) + pr
            lhs_y, lhs_s = [], []
            for hh in (2 * pair, 2 * pair + 1):
                seg = acs[:, hh:hh + 1] - src_t[hh:hh + 1, :]
                lhs_y.append((cb * jnp.exp2(jnp.where(causal, seg, NEG_INF))).astype(BF16))
                lhs_s.append((b_gt * w_t[hh:hh + 1, :]).astype(BF16))
            for hh in (2 * pair, 2 * pair + 1):
                lhs_y.append((c_g * exp_a[:, hh:hh + 1]).astype(BF16))
            x_p = xs_b[:, pair * PAIR_WIDTH:(pair + 1) * PAIR_WIDTH]
            x_lo = jnp.where(lo_half, x_p, zero_b)
            x_hi = jnp.where(lo_half, zero_b, x_p)
            st = state[pair]
            st_b = st.astype(BF16)
            st_lo = jnp.where(lo_half, st_b, zero_b)
            st_hi = jnp.where(lo_half, zero_b, st_b)
            y = jnp.dot(jnp.concatenate(lhs_y, axis=1),
                        jnp.concatenate([x_lo, x_hi, st_lo, st_hi], axis=0),
                        preferred_element_type=F32)
            yscr[:, pair * PAIR_WIDTH:(pair + 1) * PAIR_WIDTH] = y
            new = jnp.dot(jnp.concatenate(lhs_s, axis=1),
                          jnp.concatenate([x_lo, x_hi], axis=0),
                          preferred_element_type=F32)
            cd = jnp.where(lane_row, cd_row[:, 2 * pair:2 * pair + 1],
                           cd_row[:, 2 * pair + 1:2 * pair + 2])
            state[pair] = st * cd + new

    y = yscr[...] + dskip_ref[...] * xs
    z = z_ref[...].astype(F32)
    y = y * (z * _sigmoid(z))
    gw = SSM_WIDTH // SSM_GROUPS
    for g in range(SSM_GROUPS):
        cols = slice(g * gw, (g + 1) * gw)
        out_ref[:, cols] = _rmsnorm(y[:, cols], nw_ref[:, cols], SUB_EPS).astype(BF16)


def _ssd(proj, dt_raw, conv_w, conv_b, dt_bias, a_log, d_skip_vec, norm_w, batch, seq, chunks=4):
    L = chunks * SSM_CHUNK
    nc = seq // L
    assert nc * L == seq
    row = lambda b, c: b * nc + c
    full = lambda shape: pl.BlockSpec(shape, lambda b, c: (0, 0))
    cwx, cwbc = conv_w[:, :SSM_WIDTH], conv_w[:, SSM_WIDTH:]
    cbx, cbbc = conv_b[:, :SSM_WIDTH], conv_b[:, SSM_WIDTH:]
    return pl.pallas_call(
        _ssd_kernel,
        grid=(batch, nc),
        in_specs=[
            pl.BlockSpec((L, SSM_WIDTH), lambda b, c: (row(b, c), COL_Z // SSM_WIDTH)),
            pl.BlockSpec((L, SSM_WIDTH), lambda b, c: (row(b, c), COL_XS // SSM_WIDTH)),
            pl.BlockSpec((L, SSM_BC_WIDTH), lambda b, c: (row(b, c), COL_BC // SSM_BC_WIDTH)),
            pl.BlockSpec((L, LANES), lambda b, c: (row(b, c), 0)),
            full((SSM_CONV, SSM_WIDTH)),
            full((SSM_CONV, SSM_BC_WIDTH)),
            full((1, SSM_WIDTH)),
            full((1, SSM_BC_WIDTH)),
            full((1, LANES)),
            full((1, LANES)),
            full((1, SSM_WIDTH)),
            full((1, SSM_WIDTH)),
        ],
        out_specs=pl.BlockSpec((L, SSM_WIDTH), lambda b, c: (row(b, c), 0)),
        out_shape=jax.ShapeDtypeStruct((batch * seq, SSM_WIDTH), BF16),
        scratch_shapes=[
            pltpu.VMEM((SUBLANES, SSM_WIDTH), F32),
            pltpu.VMEM((SUBLANES, SSM_BC_WIDTH), F32),
            pltpu.VMEM((SSM_HEADS // 2, SSM_STATE, PAIR_WIDTH), F32),
            pltpu.VMEM((chunks, SSM_CHUNK, SSM_WIDTH), F32),
        ],
        compiler_params=_params(("parallel", "arbitrary")),
        name="ssd_mixer",
    )(proj, proj, proj, dt_raw, cwx, cwbc, cbx, cbbc, dt_bias, a_log, d_skip_vec, norm_w)


def _out_proj_kernel(att_ref, ssm_ref, w_ref, x_ref, wd_ref, x1_ref, wdb_ref, wb_ref):
    @pl.when(pl.program_id(1) == 0)
    def _():
        wb_ref[...] = w_ref[...].astype(BF16)

    acc = jnp.dot(att_ref[...], wb_ref[0], preferred_element_type=F32)
    acc = acc + jnp.dot(ssm_ref[...], wb_ref[1], preferred_element_type=F32)
    x1_ref[...] = x_ref[...] + acc
    wdb_ref[...] = wd_ref[...].astype(BF16)


def _out_proj(att, ssm, w_out2, x2d, w_down, tm=512, tn=1024):
    m = att.shape[0]
    ni = m // tm
    n_steps = (D_MODEL // tn) * ni
    d_ff = w_down.shape[0]
    wd_rows = d_ff // n_steps
    assert wd_rows * n_steps == d_ff and wd_rows % ONES_ROWS == 0
    return pl.pallas_call(
        _out_proj_kernel,
        grid=(D_MODEL // tn, ni),
        in_specs=[
            pl.BlockSpec((tm, ATT_WIDTH), lambda j, i: (i, 0)),
            pl.BlockSpec((tm, SSM_WIDTH), lambda j, i: (i, 0)),
            pl.BlockSpec((2, ATT_WIDTH, tn), lambda j, i: (0, 0, j), pipeline_mode=pl.Buffered(1)),
            pl.BlockSpec((tm, tn), lambda j, i: (i, j)),
            pl.BlockSpec((wd_rows, D_MODEL), lambda j, i: (j * ni + i, 0)),
        ],
        out_specs=[
            pl.BlockSpec((tm, tn), lambda j, i: (i, j)),
            pl.BlockSpec((wd_rows, D_MODEL), lambda j, i: (j * ni + i, 0)),
        ],
        out_shape=[
            jax.ShapeDtypeStruct((m, D_MODEL), F32),
            jax.ShapeDtypeStruct(w_down.shape, BF16),
        ],
        scratch_shapes=[pltpu.VMEM((2, ATT_WIDTH, tn), BF16)],
        compiler_params=_params(("parallel", "arbitrary")),
        name="out_proj",
    )(att, ssm, w_out2, x2d, w_down)


def _gate_up_kernel(x_ref, nw_ref, wg_ref, wu_ref, a_ref, h_ref):
    def gated(h):
        g = jnp.dot(h, wg_ref[...].astype(BF16), preferred_element_type=F32)
        u = jnp.dot(h, wu_ref[...].astype(BF16), preferred_element_type=F32)
        a_ref[...] = ((g * _sigmoid(g)) * u).astype(BF16)

    @pl.when(pl.program_id(1) == 0)
    def _():
        h = _rmsnorm(x_ref[...], nw_ref[...], RMS_EPS).astype(BF16)
        h_ref[...] = h
        gated(h)

    @pl.when(pl.program_id(1) > 0)
    def _():
        gated(h_ref[...])


def _gate_up(x1, norm_w, w_gate, w_up, tm=1024, tn=512):
    m = x1.shape[0]
    n = w_gate.shape[1]
    return pl.pallas_call(
        _gate_up_kernel,
        grid=(m // tm, n // tn),
        in_specs=[
            pl.BlockSpec((tm, D_MODEL), lambda i, j: (i, 0)),
            pl.BlockSpec((1, D_MODEL), lambda i, j: (0, 0)),
            pl.BlockSpec((D_MODEL, tn), lambda i, j: (0, j)),
            pl.BlockSpec((D_MODEL, tn), lambda i, j: (0, j)),
        ],
        out_specs=pl.BlockSpec((tm, tn), lambda i, j: (i, j)),
        out_shape=jax.ShapeDtypeStruct((m, n), BF16),
        scratch_shapes=[pltpu.VMEM((tm, D_MODEL), BF16)],
        compiler_params=_params(("parallel", "arbitrary")),
        name="gate_up",
    )(x1, norm_w, w_gate, w_up)


def _down_kernel(a_ref, w_ref, x_ref, nw_ref, o_ref, x2_ref):
    j = pl.program_id(1)
    x2_ref[j] = x_ref[...] + jnp.dot(a_ref[...], w_ref[...], preferred_element_type=F32)

    @pl.when(j == pl.num_programs(1) - 1)
    def _():
        x2 = jnp.concatenate([x2_ref[t] for t in range(x2_ref.shape[0])], axis=1)
        o_ref[...] = _rmsnorm(x2, nw_ref[...], RMS_EPS)


def _down(a, w_down, x1, norm_w, tm=512, tn=1024):
    m, kdim = a.shape
    return pl.pallas_call(
        _down_kernel,
        grid=(m // tm, D_MODEL // tn),
        in_specs=[
            pl.BlockSpec((tm, kdim), lambda i, j: (i, 0)),
            pl.BlockSpec((kdim, tn), lambda i, j: (0, j)),
            pl.BlockSpec((tm, tn), lambda i, j: (i, j)),
            pl.BlockSpec((1, D_MODEL), lambda i, j: (0, 0)),
        ],
        out_specs=pl.BlockSpec((tm, D_MODEL), lambda i, j: (i, 0)),
        out_shape=jax.ShapeDtypeStruct((m, D_MODEL), F32),
        scratch_shapes=[pltpu.VMEM((D_MODEL // tn, tm, tn), F32)],
        compiler_params=_params(("parallel", "arbitrary")),
        name="down_proj",
    )(a, w_down, x1, norm_w)


def _alibi_slopes_log2(n):
    start = 2.0 ** (-8.0 / n)
    return jnp.asarray([start ** (i + 1) for i in range(n)], dtype=F32) * LOG2E


def _pad_lanes(v):
    return jnp.pad(v.astype(F32), (0, LANES - v.shape[0])).reshape(1, LANES)


def kernel(x, norm_mix_w, w_in, lambda_q1, lambda_k1, lambda_q2, lambda_k2, subln_w, conv_w, conv_b, dt_bias, a_log, d_skip, ssm_norm_w, w_out, norm_ffn_w, w_gate, w_up, w_down, norm_final_w):
    batch, seq, _ = x.shape
    assert w_in.shape[0] == 1, "single-layer block"
    layer = 0
    x2d = x.reshape(batch * seq, D_MODEL)
    col_scale = jnp.concatenate([
        jnp.full((ATT_WIDTH,), ATT_QK_DIM ** -0.5 * LOG2E, F32),
        jnp.ones((PROJ_MAIN - ATT_WIDTH,), F32)]).reshape(1, PROJ_MAIN)
    slopes2 = _alibi_slopes_log2(ATT_HEADS)
    lambda_init = 0.8 - 0.6 * math.exp(-0.3 * layer)
    w_in_t = w_in[layer].T
    w_dt_t = jnp.pad(w_in_t[PROJ_MAIN:], ((0, LANES - SSM_HEADS), (0, 0))).astype(BF16)
    proj, dt_raw = _in_proj(x2d, norm_mix_w[layer].reshape(1, D_MODEL), w_in_t, w_dt_t, col_scale)
    lamv = jnp.stack([lambda_q1[layer], lambda_k1[layer], lambda_q2[layer], lambda_k2[layer]]).astype(F32)
    subw_col = jnp.broadcast_to(subln_w[layer].astype(F32).reshape(ATT_V_DIM, 1), (ATT_V_DIM, LANES))
    att = _attention(proj, slopes2, lamv, subw_col, batch, seq, lambda_init)
    ssm = _ssd(proj, dt_raw, conv_w[layer], conv_b[layer].reshape(1, -1),
               _pad_lanes(dt_bias[layer]), _pad_lanes(a_log[layer]),
               jnp.repeat(d_skip[layer].astype(F32), SSM_HEAD_DIM).reshape(1, SSM_WIDTH),
               ssm_norm_w[layer].reshape(1, SSM_WIDTH), batch, seq)
    w_out2 = w_out[layer].reshape(2, ATT_WIDTH, D_MODEL)
    x1, w_down_b = _out_proj(att, ssm, w_out2, x2d, w_down[layer])
    a = _gate_up(x1, norm_ffn_w[layer].reshape(1, D_MODEL), w_gate[layer], w_up[layer])
    out = _down(a, w_down_b, x1, norm_final_w.reshape(1, D_MODEL))
    return out.reshape(batch, seq, D_MODEL)
```

```python
import functools
import math

import jax
import jax.numpy as jnp
from jax import lax
from jax.experimental import pallas as pl
from jax.experimental.pallas import tpu as pltpu

F32 = jnp.float32
BF16 = jnp.bfloat16

D_MODEL = 2048
ATT_HEADS = 8
ATT_QK_DIM = 128
ATT_V_DIM = 2 * ATT_QK_DIM
ATT_WIDTH = ATT_HEADS * ATT_V_DIM
SSM_HEADS = 32
SSM_HEAD_DIM = 64
SSM_WIDTH = SSM_HEADS * SSM_HEAD_DIM
SSM_GROUPS = 4
SSM_STATE = 128
SSM_CONV = 4
SSM_CHUNK = 128
SSM_BC_WIDTH = 2 * SSM_GROUPS * SSM_STATE
PROJ_MAIN = 2 * ATT_WIDTH + ATT_WIDTH + SSM_WIDTH + SSM_WIDTH + SSM_BC_WIDTH
RMS_EPS = 1e-6
SUB_EPS = 1e-5
LOG2E = 1.4426950408889634
LANES = 128
SUBLANES = 8
ONES_ROWS = 16
BLOCKS_PER_ITER = 4
EARLY_BLOCKS = 1
HEADS_PER_GROUP = SSM_HEADS // SSM_GROUPS
PAIR_WIDTH = 2 * SSM_HEAD_DIM
NEG_INF = float("-inf")

COL_Q, COL_K, COL_V = 0, ATT_WIDTH, 2 * ATT_WIDTH
COL_Z = 3 * ATT_WIDTH
COL_XS = COL_Z + SSM_WIDTH
COL_BC = COL_XS + SSM_WIDTH

VMEM_LIMIT = 56 * 1024 * 1024


def _params(semantics):
    return pltpu.CompilerParams(dimension_semantics=semantics, vmem_limit_bytes=VMEM_LIMIT)


def _sigmoid(x):
    return 1.0 / (1.0 + jnp.exp(-x))


def _rmsnorm(x, w, eps):
    ms = jnp.mean(x * x, axis=-1, keepdims=True)
    return (x * lax.rsqrt(ms + eps)) * w


def _lane_tile(x, width):
    return x if width == LANES else jnp.concatenate([x] * (width // LANES), axis=1)


_CONTRACT_LAST = (((1,), (1,)), ((), ()))


def _in_proj_kernel(x_ref, nw_ref, wt_ref, wdt_ref, cs_ref, proj_ref, dt_ref, h_ref):
    def project(hb):
        acc = lax.dot_general(hb, wt_ref[...].astype(BF16), _CONTRACT_LAST,
                              preferred_element_type=F32)
        proj_ref[...] = (acc * cs_ref[...]).astype(BF16)

    @pl.when(pl.program_id(1) == 0)
    def _():
        hb = _rmsnorm(x_ref[...], nw_ref[...], RMS_EPS).astype(BF16)
        h_ref[...] = hb
        dt_ref[...] = lax.dot_general(hb, wdt_ref[...], _CONTRACT_LAST,
                                      preferred_element_type=F32)
        project(hb)

    @pl.when(pl.program_id(1) > 0)
    def _():
        project(h_ref[...])


def _in_proj(x2d, norm_w, w_in_t, w_dt_t, col_scale, tm=1024, tn=1024):
    m = x2d.shape[0]
    return pl.pallas_call(
        _in_proj_kernel,
        grid=(m // tm, PROJ_MAIN // tn),
        in_specs=[
            pl.BlockSpec((tm, D_MODEL), lambda i, j: (i, 0)),
            pl.BlockSpec((1, D_MODEL), lambda i, j: (0, 0)),
            pl.BlockSpec((tn, D_MODEL), lambda i, j: (j, 0)),
            pl.BlockSpec((LANES, D_MODEL), lambda i, j: (0, 0)),
            pl.BlockSpec((1, tn), lambda i, j: (0, j)),
        ],
        out_specs=[
            pl.BlockSpec((tm, tn), lambda i, j: (i, j)),
            pl.BlockSpec((tm, LANES), lambda i, j: (i, 0)),
        ],
        out_shape=[
            jax.ShapeDtypeStruct((m, PROJ_MAIN), BF16),
            jax.ShapeDtypeStruct((m, LANES), F32),
        ],
        scratch_shapes=[pltpu.VMEM((tm, D_MODEL), BF16)],
        compiler_params=_params(("parallel", "arbitrary")),
        name="in_proj",
    )(x2d, norm_w, w_in_t, w_dt_t, col_scale)


def _attn_kernel(slope_ref, lamv_ref, subw_ref, q_ref, k_ref, v_ref, o_ref,
                 vt_scr, qt_scr, kaug_scr, st_scr, m_scr, acc_scr,
                 *, tq, tk, qc, tiles, whole_head, lambda_init):
    h = pl.program_id(1)
    step = pl.program_id(2)
    slope2 = slope_ref[h]
    aug_lane = lax.broadcasted_iota(jnp.int32, (tk, ATT_QK_DIM), 1)

    @pl.when(step == 0)
    def _():
        bias = slope2 * lax.broadcasted_iota(jnp.int32, (tk, ATT_QK_DIM), 0).astype(F32)
        hi = bias.astype(BF16).astype(F32)
        mid = (bias - hi).astype(BF16).astype(F32)
        lo = (bias - hi) - mid
        kaug_scr[...] = jnp.where(aug_lane == 0, hi, jnp.where(
            aug_lane == 1, mid, jnp.where(aug_lane == 2, lo, 0.0))).astype(BF16)

    row_minus_col = (lax.broadcasted_iota(jnp.int32, (tk, qc), 0)
                     - lax.broadcasted_iota(jnp.int32, (tk, qc), 1))
    pick = jnp.where(lax.broadcasted_iota(jnp.int32, (ATT_QK_DIM, tq), 0) < 3, 1.0, 0.0).astype(BF16)
    n_qchunks = tq // qc
    lv = lamv_ref[...]
    lam = (jnp.exp(jnp.sum(lv[0:1] * lv[1:2], axis=1, keepdims=True))
           - jnp.exp(jnp.sum(lv[2:3] * lv[3:4], axis=1, keepdims=True)) + lambda_init)

    def block_chains(d=None):
        out = []
        for u in range(n_qchunks):
            if d is not None and d * tk >= (u + 1) * qc:
                continue
            thr = None if d is None or (d + 1) * tk <= u * qc else u * qc - d * tk
            out += [(u, c, thr) for c in range(2)]
        return out

    def q_tile(t):
        rows = slice(t * tq, (t + 1) * tq)
        qi = t if whole_head else step * tiles + t

        qt = q_ref[rows, :].T
        for c in range(2):
            qt_scr[t, c] = jnp.concatenate([qt[c * ATT_QK_DIM:(c + 1) * ATT_QK_DIM, :], pick], axis=0)
        m_scr[t] = jnp.full(m_scr.shape[1:], NEG_INF, F32)
        acc_scr[t] = jnp.zeros(acc_scr.shape[1:], F32)

        def scores(j, c, u):
            k0 = j * tk if isinstance(j, int) else pl.multiple_of(j * tk, tk)
            dims = slice(c * ATT_QK_DIM, (c + 1) * ATT_QK_DIM)
            k_aug = jnp.concatenate([k_ref[pl.ds(k0, tk), dims], kaug_scr[...]], axis=1)
            return jnp.dot(k_aug, qt_scr[t, c, :, u * qc:(u + 1) * qc], preferred_element_type=F32)

        def early_slot(blk, u, c):
            return (blk * n_qchunks + u) * 2 + c

        def prefetch_scores(j):
            for blk in range(EARLY_BLOCKS):
                for u, c, _ in block_chains():
                    st_scr[early_slot(blk, u, c)] = scores(j + blk, c, u)

        def chain(j, c, u, thr, slot=None, st=None):
            cs = slice(u * qc, (u + 1) * qc)
            if st is None:
                st = scores(j, c, u) if slot is None else st_scr[slot]
            if thr is not None:
                st = jnp.where(row_minus_col <= thr, st, NEG_INF)
            off = slope2 * jnp.asarray(j * tk, F32)
            m_old = m_scr[t, c, 0:1, cs]
            m_new = jnp.maximum(m_old, jnp.max(st, axis=0, keepdims=True) + off)
            alpha = jnp.exp2(m_old - m_new)
            pt = jnp.exp2(st - (m_new - off))
            m_scr[t, c, :, cs] = jnp.broadcast_to(m_new, (SUBLANES, qc))
            acc_scr[t, c, :, cs] = alpha * acc_scr[t, c, :, cs] + jnp.dot(
                vt_scr[j], pt.astype(BF16), preferred_element_type=F32)

        def run_block(j, chains, early_blk):
            for u, c, thr in chains:
                chain(j, c, u, thr, None if early_blk is None else early_slot(early_blk, u, c))

        def body(jj, carry):
            chains = block_chains()
            st_cur = None
            for i in range(BLOCKS_PER_ITER):
                j = BLOCKS_PER_ITER * jj + i
                st_nxt = []
                for k, (u, c, _) in enumerate(chains):
                    st_nxt.append(scores(j + 1, c, u))
                    if st_cur is None:
                        chain(j, c, u, None, early_slot(0, u, c))
                    else:
                        chain(j, c, u, None, st=st_cur[k])
                    if i == BLOCKS_PER_ITER - 1:
                        st_scr[early_slot(0, u, c)] = st_nxt[k]
                st_cur = st_nxt
            return carry

        n_full = (qi * tq) // tk
        prefetch_scores(0)
        for d in range(tq // tk):
            vt = v_ref[pl.ds(pl.multiple_of((n_full + d) * tk, tk), tk), :].T
            vt_scr[n_full + d] = jnp.concatenate([vt, jnp.ones((ONES_ROWS, tk), BF16)], axis=0)
        yield
        if isinstance(n_full, int):
            for jj in range(n_full // BLOCKS_PER_ITER):
                body(jj, 0)
        else:
            lax.fori_loop(0, n_full // BLOCKS_PER_ITER, body, 0)
        diag = [block_chains(d) for d in range(tq // tk)]
        st_cur = None
        for d, chains in enumerate(diag):
            nxt = diag[d + 1] if d + 1 < len(diag) else []
            st_nxt = []
            for k, (u, c, thr) in enumerate(chains):
                if k < len(nxt):
                    st_nxt.append(scores(n_full + d + 1, nxt[k][1], nxt[k][0]))
                if st_cur is None:
                    chain(n_full + d, c, u, thr, early_slot(0, u, c))
                else:
                    chain(n_full + d, c, u, thr, st=st_cur[k])
            st_cur = st_nxt
        yield

        inv_l0 = 1.0 / acc_scr[t, 0, ATT_V_DIM:ATT_V_DIM + 1, :]
        inv_l1 = lam / acc_scr[t, 1, ATT_V_DIM:ATT_V_DIM + 1, :]
        ot = (acc_scr[t, 0, :ATT_V_DIM, :] * inv_l0
              - acc_scr[t, 1, :ATT_V_DIM, :] * inv_l1)
        ms = jnp.mean(ot * ot, axis=0, keepdims=True)
        ot = (ot * lax.rsqrt(ms + SUB_EPS)) * _lane_tile(subw_ref[...], tq)
        o_ref[rows, :] = (ot * (1.0 - lambda_init)).astype(BF16).T

    phases = [q_tile(t) for t in range(tiles)]
    next(phases[0])
    for t in range(tiles):
        next(phases[t])
        if t + 1 < tiles:
            next(phases[t + 1])
        next(phases[t], None)


def _attention(proj, slopes2, lamv, subw_col, batch, seq, lambda_init,
               tq=1024, tk=256, qc=256, tiles=4):
    assert tq % (BLOCKS_PER_ITER * tk) == 0 and tq % qc == 0 and seq % (tiles * tq) == 0
    nsteps = seq // (tiles * tq)
    kern = functools.partial(_attn_kernel, tq=tq, tk=tk, qc=qc, tiles=tiles,
                             whole_head=(nsteps == 1), lambda_init=lambda_init)
    kb, vb = COL_K // ATT_V_DIM, COL_V // ATT_V_DIM
    return pl.pallas_call(
        kern,
        grid=(batch, ATT_HEADS, nsteps),
        in_specs=[
            pl.BlockSpec(memory_space=pltpu.SMEM),
            pl.BlockSpec((4, ATT_QK_DIM), lambda b, h, i: (0, 0)),
            pl.BlockSpec((ATT_V_DIM, LANES), lambda b, h, i: (0, 0)),
            pl.BlockSpec((tiles * tq, ATT_V_DIM), lambda b, h, i: (b * nsteps + i, h)),
            pl.BlockSpec((seq, ATT_V_DIM), lambda b, h, i: (b, kb + h)),
            pl.BlockSpec((seq, ATT_V_DIM), lambda b, h, i: (b, vb + h)),
        ],
        out_specs=pl.BlockSpec((tiles * tq, ATT_V_DIM), lambda b, h, i: (b * nsteps + i, h)),
        out_shape=jax.ShapeDtypeStruct((batch * seq, ATT_WIDTH), BF16),
        scratch_shapes=[
            pltpu.VMEM((seq // tk, ATT_V_DIM + ONES_ROWS, tk), BF16),
            pltpu.VMEM((tiles, 2, 2 * ATT_QK_DIM, tq), BF16),
            pltpu.VMEM((tk, ATT_QK_DIM), BF16),
            pltpu.VMEM((EARLY_BLOCKS * (tq // qc) * 2, tk, qc), F32),
            pltpu.VMEM((tiles, 2, SUBLANES, tq), F32),
            pltpu.VMEM((tiles, 2, ATT_V_DIM + ONES_ROWS, tq), F32),
        ],
        compiler_params=_params(("parallel", "parallel", "arbitrary")),
        name="diff_attention",
    )(slopes2, lamv, subw_col, proj, proj, proj)


def _ssd_kernel(z_ref, xs_ref, bc_ref, dt_ref, cwx_ref, cwbc_ref, cbx_ref, cbbc_ref,
                dtb_ref, alog_ref, dskip_ref, nw_ref, out_ref,
                tailx, tailbc, state, yscr):
    @pl.when(pl.program_id(1) == 0)
    def _():
        tailx[...] = jnp.zeros(tailx.shape, F32)
        tailbc[...] = jnp.zeros(tailbc.shape, F32)
        state[...] = jnp.zeros(state.shape, F32)

    for s in range(yscr.shape[0]):
        rows = pl.ds(s * SSM_CHUNK, SSM_CHUNK)
        _ssd_chunk(z_ref.at[rows], xs_ref.at[rows], bc_ref.at[rows], dt_ref.at[rows],
                   cwx_ref, cwbc_ref, cbx_ref, cbbc_ref, dtb_ref, alog_ref, dskip_ref, nw_ref,
                   out_ref.at[rows], tailx, tailbc, state, yscr.at[s])


def _ssd_chunk(z_ref, xs_ref, bc_ref, dt_ref, cwx_ref, cwbc_ref, cbx_ref, cbbc_ref,
               dtb_ref, alog_ref, dskip_ref, nw_ref, out_ref,
               tailx, tailbc, state, yscr):
    L = SSM_CHUNK

    def conv_silu(u_ref, tail, w_ref, b_ref):
        u = u_ref[...].astype(F32)
        full = jnp.concatenate([tail[...], u], axis=0)
        acc = b_ref[...] + w_ref[SSM_CONV - 1:SSM_CONV, :] * u
        for k in range(SSM_CONV - 1):
            lo = SUBLANES - (SSM_CONV - 1) + k
            acc = acc + w_ref[k:k + 1, :] * full[lo:lo + L]
        tail[...] = u[L - SUBLANES:L]
        return acc * _sigmoid(acc)

    xs = conv_silu(xs_ref, tailx, cwx_ref, cbx_ref)
    bc = conv_silu(bc_ref, tailbc, cwbc_ref, cbbc_ref)
    xs_b = xs.astype(BF16)
    bc_b = bc.astype(BF16)

    dtr = dt_ref[...] + dtb_ref[...]
    dt = jnp.maximum(dtr, 0.0) + jnp.log1p(jnp.exp(-jnp.abs(dtr)))
    acs = dt * (-jnp.exp(alog_ref[...]))
    rowi = lax.broadcasted_iota(jnp.int32, (L, LANES), 0)
    shift = 1
    while shift < L:
        acs = acs + jnp.where(rowi >= shift, pltpu.roll(acs, shift, axis=0), 0.0)
        shift *= 2
    acs = acs * LOG2E
    acs_t = acs.T
    dt_t = dt.T
    exp_a = jnp.exp2(acs)
    w_t = dt_t * jnp.exp2(acs_t[:, L - 1:L] - acs_t)
    cd_row = jnp.exp2(acs[L - 1:L, :])
    src_t = acs_t - jnp.log2(dt_t)

    li = lax.broadcasted_iota(jnp.int32, (L, L), 0)
    si = lax.broadcasted_iota(jnp.int32, (L, L), 1)
    causal = si <= li
    lane = lax.broadcasted_iota(jnp.int32, (L, PAIR_WIDTH), 1)
    lo_half = lane < SSM_HEAD_DIM
    lane_row = lax.broadcasted_iota(jnp.int32, (1, PAIR_WIDTH), 1) < SSM_HEAD_DIM
    zero_b = jnp.zeros((L, PAIR_WIDTH), BF16)

    gn = SSM_GROUPS * SSM_STATE
    for g in range(SSM_GROUPS):
        b_g = bc[:, g * SSM_STATE:(g + 1) * SSM_STATE]
        c_g = bc[:, gn + g * SSM_STATE:gn + (g + 1) * SSM_STATE]
        cb = lax.dot_general(bc_b[:, gn + g * SSM_STATE:gn + (g + 1) * SSM_STATE],
                             bc_b[:, g * SSM_STATE:(g + 1) * SSM_STATE],
                             (((1,), (1,)), ((), ())), preferred_element_type=F32)
        b_gt = b_g.T
        for pr in range(HEADS_PER_GROUP // 2):
            pair = g * (HEADS_PER_GROUP // 2) + pr
            lhs_y, lhs_s = [], []
            for hh in (2 * pair, 2 * pair + 1):
                seg = acs[:, hh:hh + 1] - src_t[hh:hh + 1, :]
                lhs_y.append((cb * jnp.exp2(jnp.where(causal, seg, NEG_INF))).astype(BF16))
                lhs_s.append((b_gt * w_t[hh:hh + 1, :]).astype(BF16))
            for hh in (2 * pair, 2 * pair + 1):
                lhs_y.append((c_g * exp_a[:, hh:hh + 1]).astype(BF16))
            x_p = xs_b[:, pair * PAIR_WIDTH:(pair + 1) * PAIR_WIDTH]
            x_lo = jnp.where(lo_half, x_p, zero_b)
            x_hi = jnp.where(lo_half, zero_b, x_p)
            st = state[pair]
            st_b = st.astype(BF16)
            st_lo = jnp.where(lo_half, st_b, zero_b)
            st_hi = jnp.where(lo_half, zero_b, st_b)
            y = jnp.dot(jnp.concatenate(lhs_y, axis=1),
                        jnp.concatenate([x_lo, x_hi, st_lo, st_hi], axis=0),
                        preferred_element_type=F32)
            yscr[:, pair * PAIR_WIDTH:(pair + 1) * PAIR_WIDTH] = y
            new = jnp.dot(jnp.concatenate(lhs_s, axis=1),
                          jnp.concatenate([x_lo, x_hi], axis=0),
                          preferred_element_type=F32)
            cd = jnp.where(lane_row, cd_row[:, 2 * pair:2 * pair + 1],
                           cd_row[:, 2 * pair + 1:2 * pair + 2])
            state[pair] = st * cd + new

    y = yscr[...] + dskip_ref[...] * xs
    z = z_ref[...].astype(F32)
    y = y * (z * _sigmoid(z))
    gw = SSM_WIDTH // SSM_GROUPS
    for g in range(SSM_GROUPS):
        cols = slice(g * gw, (g + 1) * gw)
        out_ref[:, cols] = _rmsnorm(y[:, cols], nw_ref[:, cols], SUB_EPS).astype(BF16)


def _ssd(proj, dt_raw, conv_w, conv_b, dt_bias, a_log, d_skip_vec, norm_w, batch, seq, chunks=4):
    L = chunks * SSM_CHUNK
    nc = seq // L
    assert nc * L == seq
    row = lambda b, c: b * nc + c
    full = lambda shape: pl.BlockSpec(shape, lambda b, c: (0, 0))
    cwx, cwbc = conv_w[:, :SSM_WIDTH], conv_w[:, SSM_WIDTH:]
    cbx, cbbc = conv_b[:, :SSM_WIDTH], conv_b[:, SSM_WIDTH:]
    return pl.pallas_call(
        _ssd_kernel,
        grid=(batch, nc),
        in_specs=[
            pl.BlockSpec((L, SSM_WIDTH), lambda b, c: (row(b, c), COL_Z // SSM_WIDTH)),
            pl.BlockSpec((L, SSM_WIDTH), lambda b, c: (row(b, c), COL_XS // SSM_WIDTH)),
            pl.BlockSpec((L, SSM_BC_WIDTH), lambda b, c: (row(b, c), COL_BC // SSM_BC_WIDTH)),
            pl.BlockSpec((L, LANES), lambda b, c: (row(b, c), 0)),
            full((SSM_CONV, SSM_WIDTH)),
            full((SSM_CONV, SSM_BC_WIDTH)),
            full((1, SSM_WIDTH)),
            full((1, SSM_BC_WIDTH)),
            full((1, LANES)),
            full((1, LANES)),
            full((1, SSM_WIDTH)),
            full((1, SSM_WIDTH)),
        ],
        out_specs=pl.BlockSpec((L, SSM_WIDTH), lambda b, c: (row(b, c), 0)),
        out_shape=jax.ShapeDtypeStruct((batch * seq, SSM_WIDTH), BF16),
        scratch_shapes=[
            pltpu.VMEM((SUBLANES, SSM_WIDTH), F32),
            pltpu.VMEM((SUBLANES, SSM_BC_WIDTH), F32),
            pltpu.VMEM((SSM_HEADS // 2, SSM_STATE, PAIR_WIDTH), F32),
            pltpu.VMEM((chunks, SSM_CHUNK, SSM_WIDTH), F32),
        ],
        compiler_params=_params(("parallel", "arbitrary")),
        name="ssd_mixer",
    )(proj, proj, proj, dt_raw, cwx, cwbc, cbx, cbbc, dt_bias, a_log, d_skip_vec, norm_w)


def _out_proj_kernel(att_ref, ssm_ref, w_ref, x_ref, wd_ref, x1_ref, wdb_ref, wb_ref):
    @pl.when(pl.program_id(1) == 0)
    def _():
        wb_ref[...] = w_ref[...].astype(BF16)

    acc = jnp.dot(att_ref[...], wb_ref[0], preferred_element_type=F32)
    acc = acc + jnp.dot(ssm_ref[...], wb_ref[1], preferred_element_type=F32)
    x1_ref[...] = x_ref[...] + acc
    wdb_ref[...] = wd_ref[...].astype(BF16)


def _out_proj(att, ssm, w_out2, x2d, w_down, tm=512, tn=1024):
    m = att.shape[0]
    ni = m // tm
    n_steps = (D_MODEL // tn) * ni
    d_ff = w_down.shape[0]
    wd_rows = d_ff // n_steps
    assert wd_rows * n_steps == d_ff and wd_rows % ONES_ROWS == 0
    return pl.pallas_call(
        _out_proj_kernel,
        grid=(D_MODEL // tn, ni),
        in_specs=[
            pl.BlockSpec((tm, ATT_WIDTH), lambda j, i: (i, 0)),
            pl.BlockSpec((tm, SSM_WIDTH), lambda j, i: (i, 0)),
            pl.BlockSpec((2, ATT_WIDTH, tn), lambda j, i: (0, 0, j), pipeline_mode=pl.Buffered(1)),
            pl.BlockSpec((tm, tn), lambda j, i: (i, j)),
            pl.BlockSpec((wd_rows, D_MODEL), lambda j, i: (j * ni + i, 0)),
        ],
        out_specs=[
            pl.BlockSpec((tm, tn), lambda j, i: (i, j)),
            pl.BlockSpec((wd_rows, D_MODEL), lambda j, i: (j * ni + i, 0)),
        ],
        out_shape=[
            jax.ShapeDtypeStruct((m, D_MODEL), F32),
            jax.ShapeDtypeStruct(w_down.shape, BF16),
        ],
        scratch_shapes=[pltpu.VMEM((2, ATT_WIDTH, tn), BF16)],
        compiler_params=_params(("parallel", "arbitrary")),
        name="out_proj",
    )(att, ssm, w_out2, x2d, w_down)


def _gate_up_kernel(x_ref, nw_ref, wg_ref, wu_ref, a_ref, h_ref):
    def gated(h):
        g = jnp.dot(h, wg_ref[...].astype(BF16), preferred_element_type=F32)
        u = jnp.dot(h, wu_ref[...].astype(BF16), preferred_element_type=F32)
        a_ref[...] = ((g * _sigmoid(g)) * u).astype(BF16)

    @pl.when(pl.program_id(1) == 0)
    def _():
        h = _rmsnorm(x_ref[...], nw_ref[...], RMS_EPS).astype(BF16)
        h_ref[...] = h
        gated(h)

    @pl.when(pl.program_id(1) > 0)
    def _():
        gated(h_ref[...])


def _gate_up(x1, norm_w, w_gate, w_up, tm=1024, tn=512):
    m = x1.shape[0]
    n = w_gate.shape[1]
    return pl.pallas_call(
        _gate_up_kernel,
        grid=(m // tm, n // tn),
        in_specs=[
            pl.BlockSpec((tm, D_MODEL), lambda i, j: (i, 0)),
            pl.BlockSpec((1, D_MODEL), lambda i, j: (0, 0)),
            pl.BlockSpec((D_MODEL, tn), lambda i, j: (0, j)),
            pl.BlockSpec((D_MODEL, tn), lambda i, j: (0, j)),
        ],
        out_specs=pl.BlockSpec((tm, tn), lambda i, j: (i, j)),
        out_shape=jax.ShapeDtypeStruct((m, n), BF16),
        scratch_shapes=[pltpu.VMEM((tm, D_MODEL), BF16)],
        compiler_params=_params(("parallel", "arbitrary")),
        name="gate_up",
    )(x1, norm_w, w_gate, w_up)


def _down_kernel(a_ref, w_ref, x_ref, nw_ref, o_ref, x2_ref):
    j = pl.program_id(1)
    x2_ref[j] = x_ref[...] + jnp.dot(a_ref[...], w_ref[...], preferred_element_type=F32)

    @pl.when(j == pl.num_programs(1) - 1)
    def _():
        x2 = jnp.concatenate([x2_ref[t] for t in range(x2_ref.shape[0])], axis=1)
        o_ref[...] = _rmsnorm(x2, nw_ref[...], RMS_EPS)


def _down(a, w_down, x1, norm_w, tm=512, tn=1024):
    m, kdim = a.shape
    return pl.pallas_call(
        _down_kernel,
        grid=(m // tm, D_MODEL // tn),
        in_specs=[
            pl.BlockSpec((tm, kdim), lambda i, j: (i, 0)),
            pl.BlockSpec((kdim, tn), lambda i, j: (0, j)),
            pl.BlockSpec((tm, tn), lambda i, j: (i, j)),
            pl.BlockSpec((1, D_MODEL), lambda i, j: (0, 0)),
        ],
        out_specs=pl.BlockSpec((tm, D_MODEL), lambda i, j: (i, 0)),
        out_shape=jax.ShapeDtypeStruct((m, D_MODEL), F32),
        scratch_shapes=[pltpu.VMEM((D_MODEL // tn, tm, tn), F32)],
        compiler_params=_params(("parallel", "arbitrary")),
        name="down_proj",
    )(a, w_down, x1, norm_w)


def _alibi_slopes_log2(n):
    start = 2.0 ** (-8.0 / n)
    return jnp.asarray([start ** (i + 1) for i in range(n)], dtype=F32) * LOG2E


def _pad_lanes(v):
    return jnp.pad(v.astype(F32), (0, LANES - v.shape[0])).reshape(1, LANES)


def kernel(x, norm_mix_w, w_in, lambda_q1, lambda_k1, lambda_q2, lambda_k2, subln_w, conv_w, conv_b, dt_bias, a_log, d_skip, ssm_norm_w, w_out, norm_ffn_w, w_gate, w_up, w_down, norm_final_w):
    batch, seq, _ = x.shape
    assert w_in.shape[0] == 1, "single-layer block"
    layer = 0
    x2d = x.reshape(batch * seq, D_MODEL)
    col_scale = jnp.concatenate([
        jnp.full((ATT_WIDTH,), ATT_QK_DIM ** -0.5 * LOG2E, F32),
        jnp.ones((PROJ_MAIN - ATT_WIDTH,), F32)]).reshape(1, PROJ_MAIN)
    slopes2 = _alibi_slopes_log2(ATT_HEADS)
    lambda_init = 0.8 - 0.6 * math.exp(-0.3 * layer)
    w_in_t = w_in[layer].T
    w_dt_t = jnp.pad(w_in_t[PROJ_MAIN:], ((0, LANES - SSM_HEADS), (0, 0))).astype(BF16)
    proj, dt_raw = _in_proj(x2d, norm_mix_w[layer].reshape(1, D_MODEL), w_in_t, w_dt_t, col_scale)
    lamv = jnp.stack([lambda_q1[layer], lambda_k1[layer], lambda_q2[layer], lambda_k2[layer]]).astype(F32)
    subw_col = jnp.broadcast_to(subln_w[layer].astype(F32).reshape(ATT_V_DIM, 1), (ATT_V_DIM, LANES))
    att = _attention(proj, slopes2, lamv, subw_col, batch, seq, lambda_init)
    ssm = _ssd(proj, dt_raw, conv_w[layer], conv_b[layer].reshape(1, -1),
               _pad_lanes(dt_bias[layer]), _pad_lanes(a_log[layer]),
               jnp.repeat(d_skip[layer].astype(F32), SSM_HEAD_DIM).reshape(1, SSM_WIDTH),
               ssm_norm_w[layer].reshape(1, SSM_WIDTH), batch, seq)
    w_out2 = w_out[layer].reshape(2, ATT_WIDTH, D_MODEL)
    x1, w_down_b = _out_proj(att, ssm, w_out2, x2d, w_down[layer])
    a = _gate_up(x1, norm_ffn_w[layer].reshape(1, D_MODEL), w_gate[layer], w_up[layer])
    out = _down(a, w_down_b, x1, norm_final_w.reshape(1, D_MODEL))
    return out.reshape(batch, seq, D_MODEL)
```

```python
import functools
import math

import jax
import jax.numpy as jnp
from jax import lax
from jax.experimental import pallas as pl
from jax.experimental.pallas import tpu as pltpu

F32 = jnp.float32
BF16 = jnp.bfloat16

D_MODEL = 2048
ATT_HEADS = 8
ATT_QK_DIM = 128
ATT_V_DIM = 2 * ATT_QK_DIM
ATT_WIDTH = ATT_HEADS * ATT_V_DIM
SSM_HEADS = 32
SSM_HEAD_DIM = 64
SSM_WIDTH = SSM_HEADS * SSM_HEAD_DIM
SSM_GROUPS = 4
SSM_STATE = 128
SSM_CONV = 4
SSM_CHUNK = 128
SSM_BC_WIDTH = 2 * SSM_GROUPS * SSM_STATE
PROJ_MAIN = 2 * ATT_WIDTH + ATT_WIDTH + SSM_WIDTH + SSM_WIDTH + SSM_BC_WIDTH
RMS_EPS = 1e-6
SUB_EPS = 1e-5
LOG2E = 1.4426950408889634
LANES = 128
SUBLANES = 8
ONES_ROWS = 16
LOOKAHEAD = 4
HEADS_PER_GROUP = SSM_HEADS // SSM_GROUPS
PAIR_WIDTH = 2 * SSM_HEAD_DIM
NEG_INF = float("-inf")

COL_Q, COL_K, COL_V = 0, ATT_WIDTH, 2 * ATT_WIDTH
COL_Z = 3 * ATT_WIDTH
COL_XS = COL_Z + SSM_WIDTH
COL_BC = COL_XS + SSM_WIDTH

VMEM_LIMIT = 56 * 1024 * 1024


def _params(semantics):
    return pltpu.CompilerParams(dimension_semantics=semantics, vmem_limit_bytes=VMEM_LIMIT)


def _sigmoid(x):
    return 1.0 / (1.0 + jnp.exp(-x))


def _rmsnorm(x, w, eps):
    ms = jnp.mean(x * x, axis=-1, keepdims=True)
    return (x * lax.rsqrt(ms + eps)) * w


def _lane_tile(x, width):
    return x if width == LANES else jnp.concatenate([x] * (width // LANES), axis=1)


_CONTRACT_LAST = (((1,), (1,)), ((), ()))


def _in_proj_kernel(x_ref, nw_ref, wt_ref, wdt_ref, cs_ref, proj_ref, dt_ref, h_ref):
    def project(hb):
        acc = lax.dot_general(hb, wt_ref[...].astype(BF16), _CONTRACT_LAST,
                              preferred_element_type=F32)
        proj_ref[...] = (acc * cs_ref[...]).astype(BF16)

    @pl.when(pl.program_id(1) == 0)
    def _():
        hb = _rmsnorm(x_ref[...], nw_ref[...], RMS_EPS).astype(BF16)
        h_ref[...] = hb
        dt_ref[...] = lax.dot_general(hb, wdt_ref[...], _CONTRACT_LAST,
                                      preferred_element_type=F32)
        project(hb)

    @pl.when(pl.program_id(1) > 0)
    def _():
        project(h_ref[...])


def _in_proj(x2d, norm_w, w_in_t, w_dt_t, col_scale, tm=1024, tn=1024):
    m = x2d.shape[0]
    return pl.pallas_call(
        _in_proj_kernel,
        grid=(m // tm, PROJ_MAIN // tn),
        in_specs=[
            pl.BlockSpec((tm, D_MODEL), lambda i, j: (i, 0)),
            pl.BlockSpec((1, D_MODEL), lambda i, j: (0, 0)),
            pl.BlockSpec((tn, D_MODEL), lambda i, j: (j, 0)),
            pl.BlockSpec((LANES, D_MODEL), lambda i, j: (0, 0)),
            pl.BlockSpec((1, tn), lambda i, j: (0, j)),
        ],
        out_specs=[
            pl.BlockSpec((tm, tn), lambda i, j: (i, j)),
            pl.BlockSpec((tm, LANES), lambda i, j: (i, 0)),
        ],
        out_shape=[
            jax.ShapeDtypeStruct((m, PROJ_MAIN), BF16),
            jax.ShapeDtypeStruct((m, LANES), F32),
        ],
        scratch_shapes=[pltpu.VMEM((tm, D_MODEL), BF16)],
        compiler_params=_params(("parallel", "arbitrary")),
        name="in_proj",
    )(x2d, norm_w, w_in_t, w_dt_t, col_scale)


def _attn_kernel(slope_ref, lamv_ref, subw_ref, q_ref, k_ref, v_ref, o_ref,
                 vt_scr, qt_scr, kaug_scr, m_scr, acc_scr, *, tq, tk, qc, lambda_init):
    slope2 = slope_ref[pl.program_id(1)]
    aug_lane = lax.broadcasted_iota(jnp.int32, (tk, ATT_QK_DIM), 1)

    bias = slope2 * lax.broadcasted_iota(jnp.int32, (tk, ATT_QK_DIM), 0).astype(F32)
    hi = bias.astype(BF16).astype(F32)
    mid = (bias - hi).astype(BF16).astype(F32)
    lo = (bias - hi) - mid
    kaug_scr[...] = jnp.where(aug_lane == 0, hi, jnp.where(
        aug_lane == 1, mid, jnp.where(aug_lane == 2, lo, 0.0))).astype(BF16)

    row_minus_col = (lax.broadcasted_iota(jnp.int32, (tk, qc), 0)
                     - lax.broadcasted_iota(jnp.int32, (tk, qc), 1))
    pick = jnp.where(lax.broadcasted_iota(jnp.int32, (ATT_QK_DIM, tq), 0) < 3, 1.0, 0.0).astype(BF16)
    n_qchunks = tq // qc
    lv = lamv_ref[...]
    lam = (jnp.exp(jnp.sum(lv[0:1] * lv[1:2], axis=1, keepdims=True))
           - jnp.exp(jnp.sum(lv[2:3] * lv[3:4], axis=1, keepdims=True)) + lambda_init)

    def block_chains(d=None):
        out = []
        for u in range(n_qchunks):
            if d is not None and d * tk >= (u + 1) * qc:
                continue
            thr = None if d is None or (d + 1) * tk <= u * qc else u * qc - d * tk
            out += [(u, c, thr) for c in range(2)]
        return out

    def q_tile(t):
        rows = slice(t * tq, (t + 1) * tq)

        qt = q_ref[rows, :].T
        for c in range(2):
            qt_scr[t, c] = jnp.concatenate([qt[c * ATT_QK_DIM:(c + 1) * ATT_QK_DIM, :], pick], axis=0)
        m_scr[t] = jnp.full(m_scr.shape[1:], NEG_INF, F32)
        acc_scr[t] = jnp.zeros(acc_scr.shape[1:], F32)

        def scores(j, c, u):
            dims = slice(c * ATT_QK_DIM, (c + 1) * ATT_QK_DIM)
            k_aug = jnp.concatenate([k_ref[j * tk:(j + 1) * tk, dims], kaug_scr[...]], axis=1)
            return jnp.dot(k_aug, qt_scr[t, c, :, u * qc:(u + 1) * qc], preferred_element_type=F32)

        def chain(j, c, u, thr, st):
            cs = slice(u * qc, (u + 1) * qc)
            if thr is not None:
                st = jnp.where(row_minus_col <= thr, st, NEG_INF)
            off = slope2 * float(j * tk)
            m_old = m_scr[t, c, 0:1, cs]
            m_new = jnp.maximum(m_old, jnp.max(st, axis=0, keepdims=True) + off)
            alpha = jnp.exp2(m_old - m_new)
            pt = jnp.exp2(st - (m_new - off))
            m_scr[t, c, :, cs] = jnp.broadcast_to(m_new, (SUBLANES, qc))
            acc_scr[t, c, :, cs] = alpha * acc_scr[t, c, :, cs] + jnp.dot(
                vt_scr[j], pt.astype(BF16), preferred_element_type=F32)

        n_full = (t * tq) // tk
        for d in range(tq // tk):
            vt = v_ref[(n_full + d) * tk:(n_full + d + 1) * tk, :].T
            vt_scr[n_full + d] = jnp.concatenate([vt, jnp.ones((ONES_ROWS, tk), BF16)], axis=0)
        tasks = [(j, u, c, None) for j in range(n_full) for u, c, _ in block_chains()]
        tasks += [(n_full + d, u, c, thr) for d in range(tq // tk) for u, c, thr in block_chains(d)]
        sts = [scores(j, c, u) for j, u, c, _ in tasks[:LOOKAHEAD]]
        for i, (j, u, c, thr) in enumerate(tasks):
            if i + LOOKAHEAD < len(tasks):
                jn, un, cn, _ = tasks[i + LOOKAHEAD]
                sts.append(scores(jn, cn, un))
            chain(j, c, u, thr, sts[i])
            sts[i] = None

        inv_l0 = 1.0 / acc_scr[t, 0, ATT_V_DIM:ATT_V_DIM + 1, :]
        inv_l1 = lam / acc_scr[t, 1, ATT_V_DIM:ATT_V_DIM + 1, :]
        ot = (acc_scr[t, 0, :ATT_V_DIM, :] * inv_l0
              - acc_scr[t, 1, :ATT_V_DIM, :] * inv_l1)
        ms = jnp.mean(ot * ot, axis=0, keepdims=True)
        ot = (ot * lax.rsqrt(ms + SUB_EPS)) * _lane_tile(subw_ref[...], tq)
        o_ref[rows, :] = (ot * (1.0 - lambda_init)).astype(BF16).T

    for t in range(q_ref.shape[0] // tq):
        q_tile(t)


def _attention(proj, slopes2, lamv, subw_col, batch, seq, lambda_init, tq=1024, tk=256, qc=256):
    assert tq % tk == 0 and tq % qc == 0 and seq % tq == 0
    tiles = seq // tq
    kern = functools.partial(_attn_kernel, tq=tq, tk=tk, qc=qc, lambda_init=lambda_init)
    kb, vb = COL_K // ATT_V_DIM, COL_V // ATT_V_DIM
    return pl.pallas_call(
        kern,
        grid=(batch, ATT_HEADS),
        in_specs=[
            pl.BlockSpec(memory_space=pltpu.SMEM),
            pl.BlockSpec((4, ATT_QK_DIM), lambda b, h: (0, 0)),
            pl.BlockSpec((ATT_V_DIM, LANES), lambda b, h: (0, 0)),
            pl.BlockSpec((seq, ATT_V_DIM), lambda b, h: (b, h)),
            pl.BlockSpec((seq, ATT_V_DIM), lambda b, h: (b, kb + h)),
            pl.BlockSpec((seq, ATT_V_DIM), lambda b, h: (b, vb + h)),
        ],
        out_specs=pl.BlockSpec((seq, ATT_V_DIM), lambda b, h: (b, h)),
        out_shape=jax.ShapeDtypeStruct((batch * seq, ATT_WIDTH), BF16),
        scratch_shapes=[
            pltpu.VMEM((seq // tk, ATT_V_DIM + ONES_ROWS, tk), BF16),
            pltpu.VMEM((tiles, 2, 2 * ATT_QK_DIM, tq), BF16),
            pltpu.VMEM((tk, ATT_QK_DIM), BF16),
            pltpu.VMEM((tiles, 2, SUBLANES, tq), F32),
            pltpu.VMEM((tiles, 2, ATT_V_DIM + ONES_ROWS, tq), F32),
        ],
        compiler_params=_params(("parallel", "parallel")),
        name="diff_attention",
    )(slopes2, lamv, subw_col, proj, proj, proj)


def _ssd_kernel(z_ref, xs_ref, bc_ref, dt_ref, cwx_ref, cwbc_ref, cbx_ref, cbbc_ref,
                dtb_ref, alog_ref, dskip_ref, nw_ref, out_ref,
                tailx, tailbc, state, yscr):
    @pl.when(pl.program_id(1) == 0)
    def _():
        tailx[...] = jnp.zeros(tailx.shape, F32)
        tailbc[...] = jnp.zeros(tailbc.shape, F32)
        state[...] = jnp.zeros(state.shape, F32)

    for s in range(yscr.shape[0]):
        rows = pl.ds(s * SSM_CHUNK, SSM_CHUNK)
        _ssd_chunk(z_ref.at[rows], xs_ref.at[rows], bc_ref.at[rows], dt_ref.at[rows],
                   cwx_ref, cwbc_ref, cbx_ref, cbbc_ref, dtb_ref, alog_ref, dskip_ref, nw_ref,
                   out_ref.at[rows], tailx, tailbc, state, yscr.at[s])


def _ssd_chunk(z_ref, xs_ref, bc_ref, dt_ref, cwx_ref, cwbc_ref, cbx_ref, cbbc_ref,
               dtb_ref, alog_ref, dskip_ref, nw_ref, out_ref,
               tailx, tailbc, state, yscr):
    L = SSM_CHUNK

    def conv_silu(u_ref, tail, w_ref, b_ref):
        u = u_ref[...].astype(F32)
        full = jnp.concatenate([tail[...], u], axis=0)
        acc = b_ref[...] + w_ref[SSM_CONV - 1:SSM_CONV, :] * u
        for k in range(SSM_CONV - 1):
            lo = SUBLANES - (SSM_CONV - 1) + k
            acc = acc + w_ref[k:k + 1, :] * full[lo:lo + L]
        tail[...] = u[L - SUBLANES:L]
        return acc * _sigmoid(acc)

    xs = conv_silu(xs_ref, tailx, cwx_ref, cbx_ref)
    bc = conv_silu(bc_ref, tailbc, cwbc_ref, cbbc_ref)
    xs_b = xs.astype(BF16)
    bc_b = bc.astype(BF16)

    dtr = dt_ref[...] + dtb_ref[...]
    dt = jnp.maximum(dtr, 0.0) + jnp.log1p(jnp.exp(-jnp.abs(dtr)))
    acs = dt * (-jnp.exp(alog_ref[...]))
    rowi = lax.broadcasted_iota(jnp.int32, (L, LANES), 0)
    shift = 1
    while shift < L:
        acs = acs + jnp.where(rowi >= shift, pltpu.roll(acs, shift, axis=0), 0.0)
        shift *= 2
    acs = acs * LOG2E
    acs_t = acs.T
    dt_t = dt.T
    exp_a = jnp.exp2(acs)
    w_t = dt_t * jnp.exp2(acs_t[:, L - 1:L] - acs_t)
    cd_row = jnp.exp2(acs[L - 1:L, :])
    src_t = acs_t - jnp.log2(dt_t)

    li = lax.broadcasted_iota(jnp.int32, (L, L), 0)
    si = lax.broadcasted_iota(jnp.int32, (L, L), 1)
    causal = si <= li
    lane = lax.broadcasted_iota(jnp.int32, (L, PAIR_WIDTH), 1)
    lo_half = lane < SSM_HEAD_DIM
    lane_row = lax.broadcasted_iota(jnp.int32, (1, PAIR_WIDTH), 1) < SSM_HEAD_DIM
    zero_b = jnp.zeros((L, PAIR_WIDTH), BF16)

    gn = SSM_GROUPS * SSM_STATE
    for g in range(SSM_GROUPS):
        b_g = bc[:, g * SSM_STATE:(g + 1) * SSM_STATE]
        c_g = bc[:, gn + g * SSM_STATE:gn + (g + 1) * SSM_STATE]
        cb = lax.dot_general(bc_b[:, gn + g * SSM_STATE:gn + (g + 1) * SSM_STATE],
                             bc_b[:, g * SSM_STATE:(g + 1) * SSM_STATE],
                             (((1,), (1,)), ((), ())), preferred_element_type=F32)
        b_gt = b_g.T
        for pr in range(HEADS_PER_GROUP // 2):
            pair = g * (HEADS_PER_GROUP // 2) + pr
            lhs_y, lhs_s = [], []
            for hh in (2 * pair, 2 * pair + 1):
                seg = acs[:, hh:hh + 1] - src_t[hh:hh + 1, :]
                lhs_y.append((cb * jnp.exp2(jnp.where(causal, seg, NEG_INF))).astype(BF16))
                lhs_s.append((b_gt * w_t[hh:hh + 1, :]).astype(BF16))
            for hh in (2 * pair, 2 * pair + 1):
                lhs_y.append((c_g * exp_a[:, hh:hh + 1]).astype(BF16))
            x_p = xs_b[:, pair * PAIR_WIDTH:(pair + 1) * PAIR_WIDTH]
            x_lo = jnp.where(lo_half, x_p, zero_b)
            x_hi = jnp.where(lo_half, zero_b, x_p)
            st = state[pair]
            st_b = st.astype(BF16)
            st_lo = jnp.where(lo_half, st_b, zero_b)
            st_hi = jnp.where(lo_half, zero_b, st_b)
            y = jnp.dot(jnp.concatenate(lhs_y, axis=1),
                        jnp.concatenate([x_lo, x_hi, st_lo, st_hi], axis=0),
                        preferred_element_type=F32)
            yscr[:, pair * PAIR_WIDTH:(pair + 1) * PAIR_WIDTH] = y
            new = jnp.dot(jnp.concatenate(lhs_s, axis=1),
                          jnp.concatenate([x_lo, x_hi], axis=0),
                          preferred_element_type=F32)
            cd = jnp.where(lane_row, cd_row[:, 2 * pair:2 * pair + 1],
                           cd_row[:, 2 * pair + 1:2 * pair + 2])
            state[pair] = st * cd + new

    y = yscr[...] + dskip_ref[...] * xs
    z = z_ref[...].astype(F32)
    y = y * (z * _sigmoid(z))
    gw = SSM_WIDTH // SSM_GROUPS
    for g in range(SSM_GROUPS):
        cols = slice(g * gw, (g + 1) * gw)
        out_ref[:, cols] = _rmsnorm(y[:, cols], nw_ref[:, cols], SUB_EPS).astype(BF16)


def _ssd(proj, dt_raw, conv_w, conv_b, dt_bias, a_log, d_skip_vec, norm_w, batch, seq, chunks=4):
    L = chunks * SSM_CHUNK
    nc = seq // L
    assert nc * L == seq
    row = lambda b, c: b * nc + c
    full = lambda shape: pl.BlockSpec(shape, lambda b, c: (0, 0))
    cwx, cwbc = conv_w[:, :SSM_WIDTH], conv_w[:, SSM_WIDTH:]
    cbx, cbbc = conv_b[:, :SSM_WIDTH], conv_b[:, SSM_WIDTH:]
    return pl.pallas_call(
        _ssd_kernel,
        grid=(batch, nc),
        in_specs=[
            pl.BlockSpec((L, SSM_WIDTH), lambda b, c: (row(b, c), COL_Z // SSM_WIDTH)),
            pl.BlockSpec((L, SSM_WIDTH), lambda b, c: (row(b, c), COL_XS // SSM_WIDTH)),
            pl.BlockSpec((L, SSM_BC_WIDTH), lambda b, c: (row(b, c), COL_BC // SSM_BC_WIDTH)),
            pl.BlockSpec((L, LANES), lambda b, c: (row(b, c), 0)),
            full((SSM_CONV, SSM_WIDTH)),
            full((SSM_CONV, SSM_BC_WIDTH)),
            full((1, SSM_WIDTH)),
            full((1, SSM_BC_WIDTH)),
            full((1, LANES)),
            full((1, LANES)),
            full((1, SSM_WIDTH)),
            full((1, SSM_WIDTH)),
        ],
        out_specs=pl.BlockSpec((L, SSM_WIDTH), lambda b, c: (row(b, c), 0)),
        out_shape=jax.ShapeDtypeStruct((batch * seq, SSM_WIDTH), BF16),
        scratch_shapes=[
            pltpu.VMEM((SUBLANES, SSM_WIDTH), F32),
            pltpu.VMEM((SUBLANES, SSM_BC_WIDTH), F32),
            pltpu.VMEM((SSM_HEADS // 2, SSM_STATE, PAIR_WIDTH), F32),
            pltpu.VMEM((chunks, SSM_CHUNK, SSM_WIDTH), F32),
        ],
        compiler_params=_params(("parallel", "arbitrary")),
        name="ssd_mixer",
    )(proj, proj, proj, dt_raw, cwx, cwbc, cbx, cbbc, dt_bias, a_log, d_skip_vec, norm_w)


def _out_proj_kernel(att_ref, ssm_ref, w_ref, x_ref, wd_ref, x1_ref, wdb_ref, wb_ref):
    @pl.when(pl.program_id(1) == 0)
    def _():
        wb_ref[...] = w_ref[...].astype(BF16)

    acc = jnp.dot(att_ref[...], wb_ref[0], preferred_element_type=F32)
    acc = acc + jnp.dot(ssm_ref[...], wb_ref[1], preferred_element_type=F32)
    x1_ref[...] = x_ref[...] + acc
    wdb_ref[...] = wd_ref[...].astype(BF16)


def _out_proj(att, ssm, w_out2, x2d, w_down, tm=512, tn=1024):
    m = att.shape[0]
    ni = m // tm
    n_steps = (D_MODEL // tn) * ni
    d_ff = w_down.shape[0]
    wd_rows = d_ff // n_steps
    assert wd_rows * n_steps == d_ff and wd_rows % ONES_ROWS == 0
    return pl.pallas_call(
        _out_proj_kernel,
        grid=(D_MODEL // tn, ni),
        in_specs=[
            pl.BlockSpec((tm, ATT_WIDTH), lambda j, i: (i, 0)),
            pl.BlockSpec((tm, SSM_WIDTH), lambda j, i: (i, 0)),
            pl.BlockSpec((2, ATT_WIDTH, tn), lambda j, i: (0, 0, j), pipeline_mode=pl.Buffered(1)),
            pl.BlockSpec((tm, tn), lambda j, i: (i, j)),
            pl.BlockSpec((wd_rows, D_MODEL), lambda j, i: (j * ni + i, 0)),
        ],
        out_specs=[
            pl.BlockSpec((tm, tn), lambda j, i: (i, j)),
            pl.BlockSpec((wd_rows, D_MODEL), lambda j, i: (j * ni + i, 0)),
        ],
        out_shape=[
            jax.ShapeDtypeStruct((m, D_MODEL), F32),
            jax.ShapeDtypeStruct(w_down.shape, BF16),
        ],
        scratch_shapes=[pltpu.VMEM((2, ATT_WIDTH, tn), BF16)],
        compiler_params=_params(("parallel", "arbitrary")),
        name="out_proj",
    )(att, ssm, w_out2, x2d, w_down)


def _gate_up_kernel(x_ref, nw_ref, wg_ref, wu_ref, a_ref, h_ref):
    def gated(h):
        g = jnp.dot(h, wg_ref[...].astype(BF16), preferred_element_type=F32)
        u = jnp.dot(h, wu_ref[...].astype(BF16), preferred_element_type=F32)
        a_ref[...] = ((g * _sigmoid(g)) * u).astype(BF16)

    @pl.when(pl.program_id(1) == 0)
    def _():
        h = _rmsnorm(x_ref[...], nw_ref[...], RMS_EPS).astype(BF16)
        h_ref[...] = h
        gated(h)

    @pl.when(pl.program_id(1) > 0)
    def _():
        gated(h_ref[...])


def _gate_up(x1, norm_w, w_gate, w_up, tm=1024, tn=512):
    m = x1.shape[0]
    n = w_gate.shape[1]
    return pl.pallas_call(
        _gate_up_kernel,
        grid=(m // tm, n // tn),
        in_specs=[
            pl.BlockSpec((tm, D_MODEL), lambda i, j: (i, 0)),
            pl.BlockSpec((1, D_MODEL), lambda i, j: (0, 0)),
            pl.BlockSpec((D_MODEL, tn), lambda i, j: (0, j)),
            pl.BlockSpec((D_MODEL, tn), lambda i, j: (0, j)),
        ],
        out_specs=pl.BlockSpec((tm, tn), lambda i, j: (i, j)),
        out_shape=jax.ShapeDtypeStruct((m, n), BF16),
        scratch_shapes=[pltpu.VMEM((tm, D_MODEL), BF16)],
        compiler_params=_params(("parallel", "arbitrary")),
        name="gate_up",
    )(x1, norm_w, w_gate, w_up)


def _down_kernel(a_ref, w_ref, x_ref, nw_ref, o_ref, x2_ref):
    j = pl.program_id(1)
    x2_ref[j] = x_ref[...] + jnp.dot(a_ref[...], w_ref[...], preferred_element_type=F32)

    @pl.when(j == pl.num_programs(1) - 1)
    def _():
        x2 = jnp.concatenate([x2_ref[t] for t in range(x2_ref.shape[0])], axis=1)
        o_ref[...] = _rmsnorm(x2, nw_ref[...], RMS_EPS)


def _down(a, w_down, x1, norm_w, tm=512, tn=1024):
    m, kdim = a.shape
    return pl.pallas_call(
        _down_kernel,
        grid=(m // tm, D_MODEL // tn),
        in_specs=[
            pl.BlockSpec((tm, kdim), lambda i, j: (i, 0)),
            pl.BlockSpec((kdim, tn), lambda i, j: (0, j)),
            pl.BlockSpec((tm, tn), lambda i, j: (i, j)),
            pl.BlockSpec((1, D_MODEL), lambda i, j: (0, 0)),
        ],
        out_specs=pl.BlockSpec((tm, D_MODEL), lambda i, j: (i, 0)),
        out_shape=jax.ShapeDtypeStruct((m, D_MODEL), F32),
        scratch_shapes=[pltpu.VMEM((D_MODEL // tn, tm, tn), F32)],
        compiler_params=_params(("parallel", "arbitrary")),
        name="down_proj",
    )(a, w_down, x1, norm_w)


def _alibi_slopes_log2(n):
    start = 2.0 ** (-8.0 / n)
    return jnp.asarray([start ** (i + 1) for i in range(n)], dtype=F32) * LOG2E


def _pad_lanes(v):
    return jnp.pad(v.astype(F32), (0, LANES - v.shape[0])).reshape(1, LANES)


def kernel(x, norm_mix_w, w_in, lambda_q1, lambda_k1, lambda_q2, lambda_k2, subln_w, conv_w, conv_b, dt_bias, a_log, d_skip, ssm_norm_w, w_out, norm_ffn_w, w_gate, w_up, w_down, norm_final_w):
    batch, seq, _ = x.shape
    assert w_in.shape[0] == 1, "single-layer block"
    layer = 0
    x2d = x.reshape(batch * seq, D_MODEL)
    col_scale = jnp.concatenate([
        jnp.full((ATT_WIDTH,), ATT_QK_DIM ** -0.5 * LOG2E, F32),
        jnp.ones((PROJ_MAIN - ATT_WIDTH,), F32)]).reshape(1, PROJ_MAIN)
    slopes2 = _alibi_slopes_log2(ATT_HEADS)
    lambda_init = 0.8 - 0.6 * math.exp(-0.3 * layer)
    w_in_t = w_in[layer].T
    w_dt_t = jnp.pad(w_in_t[PROJ_MAIN:], ((0, LANES - SSM_HEADS), (0, 0))).astype(BF16)
    proj, dt_raw = _in_proj(x2d, norm_mix_w[layer].reshape(1, D_MODEL), w_in_t, w_dt_t, col_scale)
    lamv = jnp.stack([lambda_q1[layer], lambda_k1[layer], lambda_q2[layer], lambda_k2[layer]]).astype(F32)
    subw_col = jnp.broadcast_to(subln_w[layer].astype(F32).reshape(ATT_V_DIM, 1), (ATT_V_DIM, LANES))
    att = _attention(proj, slopes2, lamv, subw_col, batch, seq, lambda_init)
    ssm = _ssd(proj, dt_raw, conv_w[layer], conv_b[layer].reshape(1, -1),
               _pad_lanes(dt_bias[layer]), _pad_lanes(a_log[layer]),
               jnp.repeat(d_skip[layer].astype(F32), SSM_HEAD_DIM).reshape(1, SSM_WIDTH),
               ssm_norm_w[layer].reshape(1, SSM_WIDTH), batch, seq)
    w_out2 = w_out[layer].reshape(2, ATT_WIDTH, D_MODEL)
    x1, w_down_b = _out_proj(att, ssm, w_out2, x2d, w_down[layer])
    a = _gate_up(x1, norm_ffn_w[layer].reshape(1, D_MODEL), w_gate[layer], w_up[layer])
    out = _down(a, w_down_b, x1, norm_final_w.reshape(1, D_MODEL))
    return out.reshape(batch, seq, D_MODEL)
```

```python
import functools
import math

import jax
import jax.numpy as jnp
from jax import lax
from jax.experimental import pallas as pl
from jax.experimental.pallas import tpu as pltpu

F32 = jnp.float32
BF16 = jnp.bfloat16

D_MODEL = 2048
ATT_HEADS = 8
ATT_QK_DIM = 128
ATT_V_DIM = 2 * ATT_QK_DIM
ATT_WIDTH = ATT_HEADS * ATT_V_DIM
SSM_HEADS = 32
SSM_HEAD_DIM = 64
SSM_WIDTH = SSM_HEADS * SSM_HEAD_DIM
SSM_GROUPS = 4
SSM_STATE = 128
SSM_CONV = 4
SSM_CHUNK = 128
SSM_BC_WIDTH = 2 * SSM_GROUPS * SSM_STATE
PROJ_MAIN = 2 * ATT_WIDTH + ATT_WIDTH + SSM_WIDTH + SSM_WIDTH + SSM_BC_WIDTH
RMS_EPS = 1e-6
SUB_EPS = 1e-5
LOG2E = 1.4426950408889634
LANES = 128
SUBLANES = 8
ONES_ROWS = 16
LOOKAHEAD = 4
HEADS_PER_GROUP = SSM_HEADS // SSM_GROUPS
PAIR_WIDTH = 2 * SSM_HEAD_DIM
NEG_INF = float("-inf")

COL_Q, COL_K, COL_V = 0, ATT_WIDTH, 2 * ATT_WIDTH
COL_Z = 3 * ATT_WIDTH
COL_XS = COL_Z + SSM_WIDTH
COL_BC = COL_XS + SSM_WIDTH

VMEM_LIMIT = 56 * 1024 * 1024


def _params(semantics):
    return pltpu.CompilerParams(dimension_semantics=semantics, vmem_limit_bytes=VMEM_LIMIT)


def _sigmoid(x):
    return 1.0 / (1.0 + jnp.exp(-x))


def _rmsnorm(x, w, eps):
    ms = jnp.mean(x * x, axis=-1, keepdims=True)
    return (x * lax.rsqrt(ms + eps)) * w


def _lane_tile(x, width):
    return x if width == LANES else jnp.concatenate([x] * (width // LANES), axis=1)


_CONTRACT_LAST = (((1,), (1,)), ((), ()))


def _in_proj_kernel(x_ref, nw_ref, wt_ref, wdt_ref, cs_ref, proj_ref, dt_ref, h_ref):
    def project(hb):
        acc = lax.dot_general(hb, wt_ref[...].astype(BF16), _CONTRACT_LAST,
                              preferred_element_type=F32)
        proj_ref[...] = (acc * cs_ref[...]).astype(BF16)

    @pl.when(pl.program_id(1) == 0)
    def _():
        hb = _rmsnorm(x_ref[...], nw_ref[...], RMS_EPS).astype(BF16)
        h_ref[...] = hb
        dt_ref[...] = lax.dot_general(hb, wdt_ref[...], _CONTRACT_LAST,
                                      preferred_element_type=F32)
        project(hb)

    @pl.when(pl.program_id(1) > 0)
    def _():
        project(h_ref[...])


def _in_proj(x2d, norm_w, w_in_t, w_dt_t, col_scale, tm=1024, tn=1024):
    m = x2d.shape[0]
    return pl.pallas_call(
        _in_proj_kernel,
        grid=(m // tm, PROJ_MAIN // tn),
        in_specs=[
            pl.BlockSpec((tm, D_MODEL), lambda i, j: (i, 0)),
            pl.BlockSpec((1, D_MODEL), lambda i, j: (0, 0)),
            pl.BlockSpec((tn, D_MODEL), lambda i, j: (j, 0)),
            pl.BlockSpec((LANES, D_MODEL), lambda i, j: (0, 0)),
            pl.BlockSpec((1, tn), lambda i, j: (0, j)),
        ],
        out_specs=[
            pl.BlockSpec((tm, tn), lambda i, j: (i, j)),
            pl.BlockSpec((tm, LANES), lambda i, j: (i, 0)),
        ],
        out_shape=[
            jax.ShapeDtypeStruct((m, PROJ_MAIN), BF16),
            jax.ShapeDtypeStruct((m, LANES), F32),
        ],
        scratch_shapes=[pltpu.VMEM((tm, D_MODEL), BF16)],
        compiler_params=_params(("parallel", "arbitrary")),
        name="in_proj",
    )(x2d, norm_w, w_in_t, w_dt_t, col_scale)


def _attn_kernel(slope_ref, lamv_ref, subw_ref, q_ref, k_ref, v_ref, o_ref,
                 vt_scr, qt_scr, kaug_scr, m_scr, acc_scr, *, tq, tk, qc, lambda_init):
    slope2 = slope_ref[pl.program_id(1)]
    aug_lane = lax.broadcasted_iota(jnp.int32, (tk, ATT_QK_DIM), 1)

    bias = slope2 * lax.broadcasted_iota(jnp.int32, (tk, ATT_QK_DIM), 0).astype(F32)
    hi = bias.astype(BF16).astype(F32)
    mid = (bias - hi).astype(BF16).astype(F32)
    lo = (bias - hi) - mid
    kaug_scr[...] = jnp.where(aug_lane == 0, hi, jnp.where(
        aug_lane == 1, mid, jnp.where(aug_lane == 2, lo, 0.0))).astype(BF16)

    row_minus_col = (lax.broadcasted_iota(jnp.int32, (tk, qc), 0)
                     - lax.broadcasted_iota(jnp.int32, (tk, qc), 1))
    pick = jnp.where(lax.broadcasted_iota(jnp.int32, (ATT_QK_DIM, tq), 0) < 3, 1.0, 0.0).astype(BF16)
    n_qchunks = tq // qc
    lv = lamv_ref[...]
    lam = (jnp.exp(jnp.sum(lv[0:1] * lv[1:2], axis=1, keepdims=True))
           - jnp.exp(jnp.sum(lv[2:3] * lv[3:4], axis=1, keepdims=True)) + lambda_init)

    def block_chains(d=None):
        out = []
        for u in range(n_qchunks):
            if d is not None and d * tk >= (u + 1) * qc:
                continue
            thr = None if d is None or (d + 1) * tk <= u * qc else u * qc - d * tk
            out += [(u, c, thr) for c in range(2)]
        return out

    def q_tile(t):
        rows = slice(t * tq, (t + 1) * tq)

        qt = q_ref[rows, :].T
        for c in range(2):
            qt_scr[t, c] = jnp.concatenate([qt[c * ATT_QK_DIM:(c + 1) * ATT_QK_DIM, :], pick], axis=0)
        m_scr[t] = jnp.full(m_scr.shape[1:], NEG_INF, F32)
        acc_scr[t] = jnp.zeros(acc_scr.shape[1:], F32)

        def scores(j, c, u):
            dims = slice(c * ATT_QK_DIM, (c + 1) * ATT_QK_DIM)
            k_aug = jnp.concatenate([k_ref[j * tk:(j + 1) * tk, dims], kaug_scr[...]], axis=1)
            return jnp.dot(k_aug, qt_scr[t, c, :, u * qc:(u + 1) * qc], preferred_element_type=F32)

        def chain(j, c, u, thr, st):
            cs = slice(u * qc, (u + 1) * qc)
            if thr is not None:
                st = jnp.where(row_minus_col <= thr, st, NEG_INF)
            off = slope2 * float(j * tk)
            m_old = m_scr[t, c, 0:1, cs]
            m_new = jnp.maximum(m_old, jnp.max(st, axis=0, keepdims=True) + off)
            alpha = jnp.exp2(m_old - m_new)
            pt = jnp.exp2(st - (m_new - off))
            m_scr[t, c, :, cs] = jnp.broadcast_to(m_new, (SUBLANES, qc))
            acc_scr[t, c, :, cs] = alpha * acc_scr[t, c, :, cs] + jnp.dot(
                vt_scr[j], pt.astype(BF16), preferred_element_type=F32)

        n_full = (t * tq) // tk
        for d in range(tq // tk):
            vt = v_ref[(n_full + d) * tk:(n_full + d + 1) * tk, :].T
            vt_scr[n_full + d] = jnp.concatenate([vt, jnp.ones((ONES_ROWS, tk), BF16)], axis=0)
        tasks = [(j, u, c, None) for j in range(n_full) for u, c, _ in block_chains()]
        tasks += [(n_full + d, u, c, thr) for d in range(tq // tk) for u, c, thr in block_chains(d)]
        sts = [scores(j, c, u) for j, u, c, _ in tasks[:LOOKAHEAD]]
        for i, (j, u, c, thr) in enumerate(tasks):
            if i + LOOKAHEAD < len(tasks):
                jn, un, cn, _ = tasks[i + LOOKAHEAD]
                sts.append(scores(jn, cn, un))
            chain(j, c, u, thr, sts[i])
            sts[i] = None

        inv_l0 = 1.0 / acc_scr[t, 0, ATT_V_DIM:ATT_V_DIM + 1, :]
        inv_l1 = lam / acc_scr[t, 1, ATT_V_DIM:ATT_V_DIM + 1, :]
        ot = (acc_scr[t, 0, :ATT_V_DIM, :] * inv_l0
              - acc_scr[t, 1, :ATT_V_DIM, :] * inv_l1)
        ms = jnp.mean(ot * ot, axis=0, keepdims=True)
        ot = (ot * lax.rsqrt(ms + SUB_EPS)) * _lane_tile(subw_ref[...], tq)
        o_ref[rows, :] = (ot * (1.0 - lambda_init)).astype(BF16).T

    for t in range(q_ref.shape[0] // tq):
        q_tile(t)


def _attention(proj, slopes2, lamv, subw_col, batch, seq, lambda_init, tq=2048, tk=256, qc=256):
    assert tq % tk == 0 and tq % qc == 0 and seq % tq == 0
    tiles = seq // tq
    kern = functools.partial(_attn_kernel, tq=tq, tk=tk, qc=qc, lambda_init=lambda_init)
    kb, vb = COL_K // ATT_V_DIM, COL_V // ATT_V_DIM
    return pl.pallas_call(
        kern,
        grid=(batch, ATT_HEADS),
        in_specs=[
            pl.BlockSpec(memory_space=pltpu.SMEM),
            pl.BlockSpec((4, ATT_QK_DIM), lambda b, h: (0, 0)),
            pl.BlockSpec((ATT_V_DIM, LANES), lambda b, h: (0, 0)),
            pl.BlockSpec((seq, ATT_V_DIM), lambda b, h: (b, h)),
            pl.BlockSpec((seq, ATT_V_DIM), lambda b, h: (b, kb + h)),
            pl.BlockSpec((seq, ATT_V_DIM), lambda b, h: (b, vb + h)),
        ],
        out_specs=pl.BlockSpec((seq, ATT_V_DIM), lambda b, h: (b, h)),
        out_shape=jax.ShapeDtypeStruct((batch * seq, ATT_WIDTH), BF16),
        scratch_shapes=[
            pltpu.VMEM((seq // tk, ATT_V_DIM + ONES_ROWS, tk), BF16),
            pltpu.VMEM((tiles, 2, 2 * ATT_QK_DIM, tq), BF16),
            pltpu.VMEM((tk, ATT_QK_DIM), BF16),
            pltpu.VMEM((tiles, 2, SUBLANES, tq), F32),
            pltpu.VMEM((tiles, 2, ATT_V_DIM + ONES_ROWS, tq), F32),
        ],
        compiler_params=_params(("parallel", "parallel")),
        name="diff_attention",
    )(slopes2, lamv, subw_col, proj, proj, proj)


def _ssd_kernel(z_ref, xs_ref, bc_ref, dt_ref, cwx_ref, cwbc_ref, cbx_ref, cbbc_ref,
                dtb_ref, alog_ref, dskip_ref, nw_ref, out_ref,
                tailx, tailbc, state, yscr):
    @pl.when(pl.program_id(1) == 0)
    def _():
        tailx[...] = jnp.zeros(tailx.shape, F32)
        tailbc[...] = jnp.zeros(tailbc.shape, F32)
        state[...] = jnp.zeros(state.shape, F32)

    for s in range(yscr.shape[0]):
        rows = pl.ds(s * SSM_CHUNK, SSM_CHUNK)
        _ssd_chunk(z_ref.at[rows], xs_ref.at[rows], bc_ref.at[rows], dt_ref.at[rows],
                   cwx_ref, cwbc_ref, cbx_ref, cbbc_ref, dtb_ref, alog_ref, dskip_ref, nw_ref,
                   out_ref.at[rows], tailx, tailbc, state, yscr.at[s])


def _ssd_chunk(z_ref, xs_ref, bc_ref, dt_ref, cwx_ref, cwbc_ref, cbx_ref, cbbc_ref,
               dtb_ref, alog_ref, dskip_ref, nw_ref, out_ref,
               tailx, tailbc, state, yscr):
    L = SSM_CHUNK

    def conv_silu(u_ref, tail, w_ref, b_ref):
        u = u_ref[...].astype(F32)
        full = jnp.concatenate([tail[...], u], axis=0)
        acc = b_ref[...] + w_ref[SSM_CONV - 1:SSM_CONV, :] * u
        for k in range(SSM_CONV - 1):
            lo = SUBLANES - (SSM_CONV - 1) + k
            acc = acc + w_ref[k:k + 1, :] * full[lo:lo + L]
        tail[...] = u[L - SUBLANES:L]
        return acc * _sigmoid(acc)

    xs = conv_silu(xs_ref, tailx, cwx_ref, cbx_ref)
    bc = conv_silu(bc_ref, tailbc, cwbc_ref, cbbc_ref)
    xs_b = xs.astype(BF16)
    bc_b = bc.astype(BF16)

    dtr = dt_ref[...] + dtb_ref[...]
    dt = jnp.maximum(dtr, 0.0) + jnp.log1p(jnp.exp(-jnp.abs(dtr)))
    acs = dt * (-jnp.exp(alog_ref[...]))
    rowi = lax.broadcasted_iota(jnp.int32, (L, LANES), 0)
    shift = 1
    while shift < L:
        acs = acs + jnp.where(rowi >= shift, pltpu.roll(acs, shift, axis=0), 0.0)
        shift *= 2
    acs = acs * LOG2E
    acs_t = acs.T
    dt_t = dt.T
    exp_a = jnp.exp2(acs)
    w_t = dt_t * jnp.exp2(acs_t[:, L - 1:L] - acs_t)
    cd_row = jnp.exp2(acs[L - 1:L, :])
    src_t = acs_t - jnp.log2(dt_t)

    li = lax.broadcasted_iota(jnp.int32, (L, L), 0)
    si = lax.broadcasted_iota(jnp.int32, (L, L), 1)
    causal = si <= li
    lane = lax.broadcasted_iota(jnp.int32, (L, PAIR_WIDTH), 1)
    lo_half = lane < SSM_HEAD_DIM
    lane_row = lax.broadcasted_iota(jnp.int32, (1, PAIR_WIDTH), 1) < SSM_HEAD_DIM
    zero_b = jnp.zeros((L, PAIR_WIDTH), BF16)

    gn = SSM_GROUPS * SSM_STATE
    for g in range(SSM_GROUPS):
        b_g = bc[:, g * SSM_STATE:(g + 1) * SSM_STATE]
        c_g = bc[:, gn + g * SSM_STATE:gn + (g + 1) * SSM_STATE]
        cb = lax.dot_general(bc_b[:, gn + g * SSM_STATE:gn + (g + 1) * SSM_STATE],
                             bc_b[:, g * SSM_STATE:(g + 1) * SSM_STATE],
                             (((1,), (1,)), ((), ())), preferred_element_type=F32)
        b_gt = b_g.T
        for pr in range(HEADS_PER_GROUP // 2):
            pair = g * (HEADS_PER_GROUP // 2) + pr
            lhs_y, lhs_s = [], []
            for hh in (2 * pair, 2 * pair + 1):
                seg = acs[:, hh:hh + 1] - src_t[hh:hh + 1, :]
                lhs_y.append((cb * jnp.exp2(jnp.where(causal, seg, NEG_INF))).astype(BF16))
                lhs_s.append((b_gt * w_t[hh:hh + 1, :]).astype(BF16))
            for hh in (2 * pair, 2 * pair + 1):
                lhs_y.append((c_g * exp_a[:, hh:hh + 1]).astype(BF16))
            x_p = xs_b[:, pair * PAIR_WIDTH:(pair + 1) * PAIR_WIDTH]
            x_lo = jnp.where(lo_half, x_p, zero_b)
            x_hi = jnp.where(lo_half, zero_b, x_p)
            st = state[pair]
            st_b = st.astype(BF16)
            st_lo = jnp.where(lo_half, st_b, zero_b)
            st_hi = jnp.where(lo_half, zero_b, st_b)
            y = jnp.dot(jnp.concatenate(lhs_y, axis=1),
                        jnp.concatenate([x_lo, x_hi, st_lo, st_hi], axis=0),
                        preferred_element_type=F32)
            yscr[:, pair * PAIR_WIDTH:(pair + 1) * PAIR_WIDTH] = y
            new = jnp.dot(jnp.concatenate(lhs_s, axis=1),
                          jnp.concatenate([x_lo, x_hi], axis=0),
                          preferred_element_type=F32)
            cd = jnp.where(lane_row, cd_row[:, 2 * pair:2 * pair + 1],
                           cd_row[:, 2 * pair + 1:2 * pair + 2])
            state[pair] = st * cd + new

    y = yscr[...] + dskip_ref[...] * xs
    z = z_ref[...].astype(F32)
    y = y * (z * _sigmoid(z))
    gw = SSM_WIDTH // SSM_GROUPS
    for g in range(SSM_GROUPS):
        cols = slice(g * gw, (g + 1) * gw)
        out_ref[:, cols] = _rmsnorm(y[:, cols], nw_ref[:, cols], SUB_EPS).astype(BF16)


def _ssd(proj, dt_raw, conv_w, conv_b, dt_bias, a_log, d_skip_vec, norm_w, batch, seq, chunks=4):
    L = chunks * SSM_CHUNK
    nc = seq // L
    assert nc * L == seq
    row = lambda b, c: b * nc + c
    full = lambda shape: pl.BlockSpec(shape, lambda b, c: (0, 0))
    cwx, cwbc = conv_w[:, :SSM_WIDTH], conv_w[:, SSM_WIDTH:]
    cbx, cbbc = conv_b[:, :SSM_WIDTH], conv_b[:, SSM_WIDTH:]
    return pl.pallas_call(
        _ssd_kernel,
        grid=(batch, nc),
        in_specs=[
            pl.BlockSpec((L, SSM_WIDTH), lambda b, c: (row(b, c), COL_Z // SSM_WIDTH)),
            pl.BlockSpec((L, SSM_WIDTH), lambda b, c: (row(b, c), COL_XS // SSM_WIDTH)),
            pl.BlockSpec((L, SSM_BC_WIDTH), lambda b, c: (row(b, c), COL_BC // SSM_BC_WIDTH)),
            pl.BlockSpec((L, LANES), lambda b, c: (row(b, c), 0)),
            full((SSM_CONV, SSM_WIDTH)),
            full((SSM_CONV, SSM_BC_WIDTH)),
            full((1, SSM_WIDTH)),
            full((1, SSM_BC_WIDTH)),
            full((1, LANES)),
            full((1, LANES)),
            full((1, SSM_WIDTH)),
            full((1, SSM_WIDTH)),
        ],
        out_specs=pl.BlockSpec((L, SSM_WIDTH), lambda b, c: (row(b, c), 0)),
        out_shape=jax.ShapeDtypeStruct((batch * seq, SSM_WIDTH), BF16),
        scratch_shapes=[
            pltpu.VMEM((SUBLANES, SSM_WIDTH), F32),
            pltpu.VMEM((SUBLANES, SSM_BC_WIDTH), F32),
            pltpu.VMEM((SSM_HEADS // 2, SSM_STATE, PAIR_WIDTH), F32),
            pltpu.VMEM((chunks, SSM_CHUNK, SSM_WIDTH), F32),
        ],
        compiler_params=_params(("parallel", "arbitrary")),
        name="ssd_mixer",
    )(proj, proj, proj, dt_raw, cwx, cwbc, cbx, cbbc, dt_bias, a_log, d_skip_vec, norm_w)


def _out_proj_kernel(att_ref, ssm_ref, w_ref, x_ref, wd_ref, x1_ref, wdb_ref, wb_ref):
    @pl.when(pl.program_id(1) == 0)
    def _():
        wb_ref[...] = w_ref[...].astype(BF16)

    acc = jnp.dot(att_ref[...], wb_ref[0], preferred_element_type=F32)
    acc = acc + jnp.dot(ssm_ref[...], wb_ref[1], preferred_element_type=F32)
    x1_ref[...] = x_ref[...] + acc
    wdb_ref[...] = wd_ref[...].astype(BF16)


def _out_proj(att, ssm, w_out2, x2d, w_down, tm=512, tn=1024):
    m = att.shape[0]
    ni = m // tm
    n_steps = (D_MODEL // tn) * ni
    d_ff = w_down.shape[0]
    wd_rows = d_ff // n_steps
    assert wd_rows * n_steps == d_ff and wd_rows % ONES_ROWS == 0
    return pl.pallas_call(
        _out_proj_kernel,
        grid=(D_MODEL // tn, ni),
        in_specs=[
            pl.BlockSpec((tm, ATT_WIDTH), lambda j, i: (i, 0)),
            pl.BlockSpec((tm, SSM_WIDTH), lambda j, i: (i, 0)),
            pl.BlockSpec((2, ATT_WIDTH, tn), lambda j, i: (0, 0, j), pipeline_mode=pl.Buffered(1)),
            pl.BlockSpec((tm, tn), lambda j, i: (i, j)),
            pl.BlockSpec((wd_rows, D_MODEL), lambda j, i: (j * ni + i, 0)),
        ],
        out_specs=[
            pl.BlockSpec((tm, tn), lambda j, i: (i, j)),
            pl.BlockSpec((wd_rows, D_MODEL), lambda j, i: (j * ni + i, 0)),
        ],
        out_shape=[
            jax.ShapeDtypeStruct((m, D_MODEL), F32),
            jax.ShapeDtypeStruct(w_down.shape, BF16),
        ],
        scratch_shapes=[pltpu.VMEM((2, ATT_WIDTH, tn), BF16)],
        compiler_params=_params(("parallel", "arbitrary")),
        name="out_proj",
    )(att, ssm, w_out2, x2d, w_down)


def _gate_up_kernel(x_ref, nw_ref, wg_ref, wu_ref, a_ref, h_ref):
    def gated(h):
        g = jnp.dot(h, wg_ref[...].astype(BF16), preferred_element_type=F32)
        u = jnp.dot(h, wu_ref[...].astype(BF16), preferred_element_type=F32)
        a_ref[...] = ((g * _sigmoid(g)) * u).astype(BF16)

    @pl.when(pl.program_id(1) == 0)
    def _():
        h = _rmsnorm(x_ref[...], nw_ref[...], RMS_EPS).astype(BF16)
        h_ref[...] = h
        gated(h)

    @pl.when(pl.program_id(1) > 0)
    def _():
        gated(h_ref[...])


def _gate_up(x1, norm_w, w_gate, w_up, tm=1024, tn=512):
    m = x1.shape[0]
    n = w_gate.shape[1]
    return pl.pallas_call(
        _gate_up_kernel,
        grid=(m // tm, n // tn),
        in_specs=[
            pl.BlockSpec((tm, D_MODEL), lambda i, j: (i, 0)),
            pl.BlockSpec((1, D_MODEL), lambda i, j: (0, 0)),
            pl.BlockSpec((D_MODEL, tn), lambda i, j: (0, j)),
            pl.BlockSpec((D_MODEL, tn), lambda i, j: (0, j)),
        ],
        out_specs=pl.BlockSpec((tm, tn), lambda i, j: (i, j)),
        out_shape=jax.ShapeDtypeStruct((m, n), BF16),
        scratch_shapes=[pltpu.VMEM((tm, D_MODEL), BF16)],
        compiler_params=_params(("parallel", "arbitrary")),
        name="gate_up",
    )(x1, norm_w, w_gate, w_up)


def _down_kernel(a_ref, w_ref, x_ref, nw_ref, o_ref, x2_ref):
    j = pl.program_id(1)
    x2_ref[j] = x_ref[...] + jnp.dot(a_ref[...], w_ref[...], preferred_element_type=F32)

    @pl.when(j == pl.num_programs(1) - 1)
    def _():
        x2 = jnp.concatenate([x2_ref[t] for t in range(x2_ref.shape[0])], axis=1)
        o_ref[...] = _rmsnorm(x2, nw_ref[...], RMS_EPS)


def _down(a, w_down, x1, norm_w, tm=512, tn=1024):
    m, kdim = a.shape
    return pl.pallas_call(
        _down_kernel,
        grid=(m // tm, D_MODEL // tn),
        in_specs=[
            pl.BlockSpec((tm, kdim), lambda i, j: (i, 0)),
            pl.BlockSpec((kdim, tn), lambda i, j: (0, j)),
            pl.BlockSpec((tm, tn), lambda i, j: (i, j)),
            pl.BlockSpec((1, D_MODEL), lambda i, j: (0, 0)),
        ],
        out_specs=pl.BlockSpec((tm, D_MODEL), lambda i, j: (i, 0)),
        out_shape=jax.ShapeDtypeStruct((m, D_MODEL), F32),
        scratch_shapes=[pltpu.VMEM((D_MODEL // tn, tm, tn), F32)],
        compiler_params=_params(("parallel", "arbitrary")),
        name="down_proj",
    )(a, w_down, x1, norm_w)


def _alibi_slopes_log2(n):
    start = 2.0 ** (-8.0 / n)
    return jnp.asarray([start ** (i + 1) for i in range(n)], dtype=F32) * LOG2E


def _pad_lanes(v):
    return jnp.pad(v.astype(F32), (0, LANES - v.shape[0])).reshape(1, LANES)


def kernel(x, norm_mix_w, w_in, lambda_q1, lambda_k1, lambda_q2, lambda_k2, subln_w, conv_w, conv_b, dt_bias, a_log, d_skip, ssm_norm_w, w_out, norm_ffn_w, w_gate, w_up, w_down, norm_final_w):
    batch, seq, _ = x.shape
    assert w_in.shape[0] == 1, "single-layer block"
    layer = 0
    x2d = x.reshape(batch * seq, D_MODEL)
    col_scale = jnp.concatenate([
        jnp.full((ATT_WIDTH,), ATT_QK_DIM ** -0.5 * LOG2E, F32),
        jnp.ones((PROJ_MAIN - ATT_WIDTH,), F32)]).reshape(1, PROJ_MAIN)
    slopes2 = _alibi_slopes_log2(ATT_HEADS)
    lambda_init = 0.8 - 0.6 * math.exp(-0.3 * layer)
    w_in_t = w_in[layer].T
    w_dt_t = jnp.pad(w_in_t[PROJ_MAIN:], ((0, LANES - SSM_HEADS), (0, 0))).astype(BF16)
    proj, dt_raw = _in_proj(x2d, norm_mix_w[layer].reshape(1, D_MODEL), w_in_t, w_dt_t, col_scale)
    lamv = jnp.stack([lambda_q1[layer], lambda_k1[layer], lambda_q2[layer], lambda_k2[layer]]).astype(F32)
    subw_col = jnp.broadcast_to(subln_w[layer].astype(F32).reshape(ATT_V_DIM, 1), (ATT_V_DIM, LANES))
    att = _attention(proj, slopes2, lamv, subw_col, batch, seq, lambda_init)
    ssm = _ssd(proj, dt_raw, conv_w[layer], conv_b[layer].reshape(1, -1),
               _pad_lanes(dt_bias[layer]), _pad_lanes(a_log[layer]),
               jnp.repeat(d_skip[layer].astype(F32), SSM_HEAD_DIM).reshape(1, SSM_WIDTH),
               ssm_norm_w[layer].reshape(1, SSM_WIDTH), batch, seq)
    w_out2 = w_out[layer].reshape(2, ATT_WIDTH, D_MODEL)
    x1, w_down_b = _out_proj(att, ssm, w_out2, x2d, w_down[layer])
    a = _gate_up(x1, norm_ffn_w[layer].reshape(1, D_MODEL), w_gate[layer], w_up[layer])
    out = _down(a, w_down_b, x1, norm_final_w.reshape(1, D_MODEL))
    return out.reshape(batch, seq, D_MODEL)
```

```python
import functools
import math

import jax
import jax.numpy as jnp
from jax import lax
from jax.experimental import pallas as pl
from jax.experimental.pallas import tpu as pltpu

F32 = jnp.float32
BF16 = jnp.bfloat16

D_MODEL = 2048
ATT_HEADS = 8
ATT_QK_DIM = 128
ATT_V_DIM = 2 * ATT_QK_DIM
ATT_WIDTH = ATT_HEADS * ATT_V_DIM
SSM_HEADS = 32
SSM_HEAD_DIM = 64
SSM_WIDTH = SSM_HEADS * SSM_HEAD_DIM
SSM_GROUPS = 4
SSM_STATE = 128
SSM_CONV = 4
SSM_CHUNK = 128
SSM_BC_WIDTH = 2 * SSM_GROUPS * SSM_STATE
PROJ_MAIN = 2 * ATT_WIDTH + ATT_WIDTH + SSM_WIDTH + SSM_WIDTH + SSM_BC_WIDTH
RMS_EPS = 1e-6
SUB_EPS = 1e-5
LOG2E = 1.4426950408889634
LANES = 128
SUBLANES = 8
ONES_ROWS = 16
LOOKAHEAD = 4
HEADS_PER_GROUP = SSM_HEADS // SSM_GROUPS
PAIR_WIDTH = 2 * SSM_HEAD_DIM
NEG_INF = float("-inf")

COL_Q, COL_K, COL_V = 0, ATT_WIDTH, 2 * ATT_WIDTH
COL_Z = 3 * ATT_WIDTH
COL_XS = COL_Z + SSM_WIDTH
COL_BC = COL_XS + SSM_WIDTH

VMEM_LIMIT = 56 * 1024 * 1024


def _params(semantics):
    return pltpu.CompilerParams(dimension_semantics=semantics, vmem_limit_bytes=VMEM_LIMIT)


def _sigmoid(x):
    return 1.0 / (1.0 + jnp.exp(-x))


def _rmsnorm(x, w, eps):
    ms = jnp.mean(x * x, axis=-1, keepdims=True)
    return (x * lax.rsqrt(ms + eps)) * w


def _lane_tile(x, width):
    return x if width == LANES else jnp.concatenate([x] * (width // LANES), axis=1)


_CONTRACT_LAST = (((1,), (1,)), ((), ()))


def _in_proj_kernel(x_ref, nw_ref, wt_ref, wdt_ref, cs_ref, proj_ref, dt_ref, h_ref):
    def project(hb):
        acc = lax.dot_general(hb, wt_ref[...].astype(BF16), _CONTRACT_LAST,
                              preferred_element_type=F32)
        proj_ref[...] = (acc * cs_ref[...]).astype(BF16)

    @pl.when(pl.program_id(1) == 0)
    def _():
        hb = _rmsnorm(x_ref[...], nw_ref[...], RMS_EPS).astype(BF16)
        h_ref[...] = hb
        dt_ref[...] = lax.dot_general(hb, wdt_ref[...], _CONTRACT_LAST,
                                      preferred_element_type=F32)
        project(hb)

    @pl.when(pl.program_id(1) > 0)
    def _():
        project(h_ref[...])


def _in_proj(x2d, norm_w, w_in_t, w_dt_t, col_scale, tm=1024, tn=1024):
    m = x2d.shape[0]
    return pl.pallas_call(
        _in_proj_kernel,
        grid=(m // tm, PROJ_MAIN // tn),
        in_specs=[
            pl.BlockSpec((tm, D_MODEL), lambda i, j: (i, 0)),
            pl.BlockSpec((1, D_MODEL), lambda i, j: (0, 0)),
            pl.BlockSpec((tn, D_MODEL), lambda i, j: (j, 0)),
            pl.BlockSpec((LANES, D_MODEL), lambda i, j: (0, 0)),
            pl.BlockSpec((1, tn), lambda i, j: (0, j)),
        ],
        out_specs=[
            pl.BlockSpec((tm, tn), lambda i, j: (i, j)),
            pl.BlockSpec((tm, LANES), lambda i, j: (i, 0)),
        ],
        out_shape=[
            jax.ShapeDtypeStruct((m, PROJ_MAIN), BF16),
            jax.ShapeDtypeStruct((m, LANES), F32),
        ],
        scratch_shapes=[pltpu.VMEM((tm, D_MODEL), BF16)],
        compiler_params=_params(("parallel", "arbitrary")),
        name="in_proj",
    )(x2d, norm_w, w_in_t, w_dt_t, col_scale)


def _attn_kernel(slope_ref, lamv_ref, subw_ref, q_ref, k_ref, v_ref, o_ref,
                 vt_scr, qt_scr, kaug_scr, m_scr, acc_scr, *, tq, tk, qc, lambda_init):
    slope2 = slope_ref[pl.program_id(1)]
    aug_lane = lax.broadcasted_iota(jnp.int32, (tk, ATT_QK_DIM), 1)

    bias = slope2 * lax.broadcasted_iota(jnp.int32, (tk, ATT_QK_DIM), 0).astype(F32)
    hi = bias.astype(BF16).astype(F32)
    mid = (bias - hi).astype(BF16).astype(F32)
    lo = (bias - hi) - mid
    kaug_scr[...] = jnp.where(aug_lane == 0, hi, jnp.where(
        aug_lane == 1, mid, jnp.where(aug_lane == 2, lo, 0.0))).astype(BF16)

    row_minus_col = (lax.broadcasted_iota(jnp.int32, (tk, qc), 0)
                     - lax.broadcasted_iota(jnp.int32, (tk, qc), 1))
    pick = jnp.where(lax.broadcasted_iota(jnp.int32, (ATT_QK_DIM, tq), 0) < 3, 1.0, 0.0).astype(BF16)
    n_qchunks = tq // qc
    lv = lamv_ref[...]
    lam = (jnp.exp(jnp.sum(lv[0:1] * lv[1:2], axis=1, keepdims=True))
           - jnp.exp(jnp.sum(lv[2:3] * lv[3:4], axis=1, keepdims=True)) + lambda_init)

    def block_chains(d=None):
        out = []
        for u in range(n_qchunks):
            if d is not None and d * tk >= (u + 1) * qc:
                continue
            thr = None if d is None or (d + 1) * tk <= u * qc else u * qc - d * tk
            out += [(u, c, thr) for c in range(2)]
        return out

    def q_tile(t):
        rows = slice(t * tq, (t + 1) * tq)

        qt = q_ref[rows, :].T
        for c in range(2):
            qt_scr[t, c] = jnp.concatenate([qt[c * ATT_QK_DIM:(c + 1) * ATT_QK_DIM, :], pick], axis=0)
        m_scr[t] = jnp.full(m_scr.shape[1:], NEG_INF, F32)
        acc_scr[t] = jnp.zeros(acc_scr.shape[1:], F32)

        def scores(j, c, u):
            dims = slice(c * ATT_QK_DIM, (c + 1) * ATT_QK_DIM)
            k_aug = jnp.concatenate([k_ref[j * tk:(j + 1) * tk, dims], kaug_scr[...]], axis=1)
            return jnp.dot(k_aug, qt_scr[t, c, :, u * qc:(u + 1) * qc], preferred_element_type=F32)

        def chain(j, c, u, thr, st):
            cs = slice(u * qc, (u + 1) * qc)
            if thr is not None:
                st = jnp.where(row_minus_col <= thr, st, NEG_INF)
            off = slope2 * float(j * tk)
            m_old = m_scr[t, c, 0:1, cs]
            m_new = jnp.maximum(m_old, jnp.max(st, axis=0, keepdims=True) + off)
            alpha = jnp.exp2(m_old - m_new)
            pt = jnp.exp2(st - (m_new - off))
            m_scr[t, c, :, cs] = jnp.broadcast_to(m_new, (SUBLANES, qc))
            acc_scr[t, c, :, cs] = alpha * acc_scr[t, c, :, cs] + jnp.dot(
                vt_scr[j], pt.astype(BF16), preferred_element_type=F32)

        n_full = (t * tq) // tk
        for d in range(tq // tk):
            vt = v_ref[(n_full + d) * tk:(n_full + d + 1) * tk, :].T
            vt_scr[n_full + d] = jnp.concatenate([vt, jnp.ones((ONES_ROWS, tk), BF16)], axis=0)
        tasks = [(j, u, c, None) for j in range(n_full) for u, c, _ in block_chains()]
        tasks += [(n_full + d, u, c, thr) for d in range(tq // tk) for u, c, thr in block_chains(d)]
        sts = [scores(j, c, u) for j, u, c, _ in tasks[:LOOKAHEAD]]
        for i, (j, u, c, thr) in enumerate(tasks):
            if i + LOOKAHEAD < len(tasks):
                jn, un, cn, _ = tasks[i + LOOKAHEAD]
                sts.append(scores(jn, cn, un))
            chain(j, c, u, thr, sts[i])
            sts[i] = None

        inv_l0 = 1.0 / acc_scr[t, 0, ATT_V_DIM:ATT_V_DIM + 1, :]
        inv_l1 = lam / acc_scr[t, 1, ATT_V_DIM:ATT_V_DIM + 1, :]
        ot = (acc_scr[t, 0, :ATT_V_DIM, :] * inv_l0
              - acc_scr[t, 1, :ATT_V_DIM, :] * inv_l1)
        ms = jnp.mean(ot * ot, axis=0, keepdims=True)
        ot = (ot * lax.rsqrt(ms + SUB_EPS)) * _lane_tile(subw_ref[...], tq)
        o_ref[rows, :] = (ot * (1.0 - lambda_init)).astype(BF16).T

    for t in range(q_ref.shape[0] // tq):
        q_tile(t)


def _attention(proj, slopes2, lamv, subw_col, batch, seq, lambda_init, tq=1024, tk=256, qc=256):
    assert tq % tk == 0 and tq % qc == 0 and seq % tq == 0
    tiles = seq // tq
    kern = functools.partial(_attn_kernel, tq=tq, tk=tk, qc=qc, lambda_init=lambda_init)
    kb, vb = COL_K // ATT_V_DIM, COL_V // ATT_V_DIM
    return pl.pallas_call(
        kern,
        grid=(batch, ATT_HEADS),
        in_specs=[
            pl.BlockSpec(memory_space=pltpu.SMEM),
            pl.BlockSpec((4, ATT_QK_DIM), lambda b, h: (0, 0)),
            pl.BlockSpec((ATT_V_DIM, LANES), lambda b, h: (0, 0)),
            pl.BlockSpec((seq, ATT_V_DIM), lambda b, h: (b, h)),
            pl.BlockSpec((seq, ATT_V_DIM), lambda b, h: (b, kb + h)),
            pl.BlockSpec((seq, ATT_V_DIM), lambda b, h: (b, vb + h)),
        ],
        out_specs=pl.BlockSpec((seq, ATT_V_DIM), lambda b, h: (b, h)),
        out_shape=jax.ShapeDtypeStruct((batch * seq, ATT_WIDTH), BF16),
        scratch_shapes=[
            pltpu.VMEM((seq // tk, ATT_V_DIM + ONES_ROWS, tk), BF16),
            pltpu.VMEM((tiles, 2, 2 * ATT_QK_DIM, tq), BF16),
            pltpu.VMEM((tk, ATT_QK_DIM), BF16),
            pltpu.VMEM((tiles, 2, SUBLANES, tq), F32),
            pltpu.VMEM((tiles, 2, ATT_V_DIM + ONES_ROWS, tq), F32),
        ],
        compiler_params=_params(("parallel", "parallel")),
        name="diff_attention",
    )(slopes2, lamv, subw_col, proj, proj, proj)


def _ssd_kernel(z_ref, xs_ref, bc_ref, dt_ref, cwx_ref, cwbc_ref, cbx_ref, cbbc_ref,
                dtb_ref, alog_ref, dskip_ref, nw_ref, out_ref,
                tailx, tailbc, state, yscr):
    @pl.when(pl.program_id(1) == 0)
    def _():
        tailx[...] = jnp.zeros(tailx.shape, F32)
        tailbc[...] = jnp.zeros(tailbc.shape, F32)
        state[...] = jnp.zeros(state.shape, F32)

    for s in range(yscr.shape[0]):
        rows = pl.ds(s * SSM_CHUNK, SSM_CHUNK)
        _ssd_chunk(z_ref.at[rows], xs_ref.at[rows], bc_ref.at[rows], dt_ref.at[rows],
                   cwx_ref, cwbc_ref, cbx_ref, cbbc_ref, dtb_ref, alog_ref, dskip_ref, nw_ref,
                   out_ref.at[rows], tailx, tailbc, state, yscr.at[s])


def _ssd_chunk(z_ref, xs_ref, bc_ref, dt_ref, cwx_ref, cwbc_ref, cbx_ref, cbbc_ref,
               dtb_ref, alog_ref, dskip_ref, nw_ref, out_ref,
               tailx, tailbc, state, yscr):
    L = SSM_CHUNK

    def conv_silu(u_ref, tail, w_ref, b_ref):
        u = u_ref[...].astype(F32)
        full = jnp.concatenate([tail[...], u], axis=0)
        acc = b_ref[...] + w_ref[SSM_CONV - 1:SSM_CONV, :] * u
        for k in range(SSM_CONV - 1):
            lo = SUBLANES - (SSM_CONV - 1) + k
            acc = acc + w_ref[k:k + 1, :] * full[lo:lo + L]
        tail[...] = u[L - SUBLANES:L]
        return acc * _sigmoid(acc)

    xs = conv_silu(xs_ref, tailx, cwx_ref, cbx_ref)
    bc = conv_silu(bc_ref, tailbc, cwbc_ref, cbbc_ref)
    xs_b = xs.astype(BF16)
    bc_b = bc.astype(BF16)

    dtr = dt_ref[...] + dtb_ref[...]
    dt = jnp.maximum(dtr, 0.0) + jnp.log1p(jnp.exp(-jnp.abs(dtr)))
    acs = dt * (-jnp.exp(alog_ref[...]))
    rowi = lax.broadcasted_iota(jnp.int32, (L, LANES), 0)
    shift = 1
    while shift < L:
        acs = acs + jnp.where(rowi >= shift, pltpu.roll(acs, shift, axis=0), 0.0)
        shift *= 2
    acs = acs * LOG2E
    acs_t = acs.T
    dt_t = dt.T
    exp_a = jnp.exp2(acs)
    w_t = dt_t * jnp.exp2(acs_t[:, L - 1:L] - acs_t)
    cd_row = jnp.exp2(acs[L - 1:L, :])
    src_t = acs_t - jnp.log2(dt_t)

    li = lax.broadcasted_iota(jnp.int32, (L, L), 0)
    si = lax.broadcasted_iota(jnp.int32, (L, L), 1)
    causal = si <= li
    lane = lax.broadcasted_iota(jnp.int32, (L, PAIR_WIDTH), 1)
    lo_half = lane < SSM_HEAD_DIM
    lane_row = lax.broadcasted_iota(jnp.int32, (1, PAIR_WIDTH), 1) < SSM_HEAD_DIM
    zero_b = jnp.zeros((L, PAIR_WIDTH), BF16)

    gn = SSM_GROUPS * SSM_STATE
    for g in range(SSM_GROUPS):
        b_g = bc[:, g * SSM_STATE:(g + 1) * SSM_STATE]
        c_g = bc[:, gn + g * SSM_STATE:gn + (g + 1) * SSM_STATE]
        cb = lax.dot_general(bc_b[:, gn + g * SSM_STATE:gn + (g + 1) * SSM_STATE],
                             bc_b[:, g * SSM_STATE:(g + 1) * SSM_STATE],
                             (((1,), (1,)), ((), ())), preferred_element_type=F32)
        b_gt = b_g.T
        for pr in range(HEADS_PER_GROUP // 2):
            pair = g * (HEADS_PER_GROUP // 2) + pr
            lhs_y, lhs_s = [], []
            for hh in (2 * pair, 2 * pair + 1):
                seg = acs[:, hh:hh + 1] - src_t[hh:hh + 1, :]
                lhs_y.append((cb * jnp.exp2(jnp.where(causal, seg, NEG_INF))).astype(BF16))
                lhs_s.append((b_gt * w_t[hh:hh + 1, :]).astype(BF16))
            for hh in (2 * pair, 2 * pair + 1):
                lhs_y.append((c_g * exp_a[:, hh:hh + 1]).astype(BF16))
            x_p = xs_b[:, pair * PAIR_WIDTH:(pair + 1) * PAIR_WIDTH]
            x_lo = jnp.where(lo_half, x_p, zero_b)
            x_hi = jnp.where(lo_half, zero_b, x_p)
            st = state[pair]
            st_b = st.astype(BF16)
            st_lo = jnp.where(lo_half, st_b, zero_b)
            st_hi = jnp.where(lo_half, zero_b, st_b)
            y = jnp.dot(jnp.concatenate(lhs_y, axis=1),
                        jnp.concatenate([x_lo, x_hi, st_lo, st_hi], axis=0),
                        preferred_element_type=F32)
            yscr[:, pair * PAIR_WIDTH:(pair + 1) * PAIR_WIDTH] = y
            new = jnp.dot(jnp.concatenate(lhs_s, axis=1),
                          jnp.concatenate([x_lo, x_hi], axis=0),
                          preferred_element_type=F32)
            cd = jnp.where(lane_row, cd_row[:, 2 * pair:2 * pair + 1],
                           cd_row[:, 2 * pair + 1:2 * pair + 2])
            state[pair] = st * cd + new

    y = yscr[...] + dskip_ref[...] * xs
    z = z_ref[...].astype(F32)
    y = y * (z * _sigmoid(z))
    gw = SSM_WIDTH // SSM_GROUPS
    for g in range(SSM_GROUPS):
        cols = slice(g * gw, (g + 1) * gw)
        out_ref[:, cols] = _rmsnorm(y[:, cols], nw_ref[:, cols], SUB_EPS).astype(BF16)


def _ssd(proj, dt_raw, conv_w, conv_b, dt_bias, a_log, d_skip_vec, norm_w, batch, seq, chunks=4):
    L = chunks * SSM_CHUNK
    nc = seq // L
    assert nc * L == seq
    row = lambda b, c: b * nc + c
    full = lambda shape: pl.BlockSpec(shape, lambda b, c: (0, 0))
    cwx, cwbc = conv_w[:, :SSM_WIDTH], conv_w[:, SSM_WIDTH:]
    cbx, cbbc = conv_b[:, :SSM_WIDTH], conv_b[:, SSM_WIDTH:]
    return pl.pallas_call(
        _ssd_kernel,
        grid=(batch, nc),
        in_specs=[
            pl.BlockSpec((L, SSM_WIDTH), lambda b, c: (row(b, c), COL_Z // SSM_WIDTH)),
            pl.BlockSpec((L, SSM_WIDTH), lambda b, c: (row(b, c), COL_XS // SSM_WIDTH)),
            pl.BlockSpec((L, SSM_BC_WIDTH), lambda b, c: (row(b, c), COL_BC // SSM_BC_WIDTH)),
            pl.BlockSpec((L, LANES), lambda b, c: (row(b, c), 0)),
            full((SSM_CONV, SSM_WIDTH)),
            full((SSM_CONV, SSM_BC_WIDTH)),
            full((1, SSM_WIDTH)),
            full((1, SSM_BC_WIDTH)),
            full((1, LANES)),
            full((1, LANES)),
            full((1, SSM_WIDTH)),
            full((1, SSM_WIDTH)),
        ],
        out_specs=pl.BlockSpec((L, SSM_WIDTH), lambda b, c: (row(b, c), 0)),
        out_shape=jax.ShapeDtypeStruct((batch * seq, SSM_WIDTH), BF16),
        scratch_shapes=[
            pltpu.VMEM((SUBLANES, SSM_WIDTH), F32),
            pltpu.VMEM((SUBLANES, SSM_BC_WIDTH), F32),
            pltpu.VMEM((SSM_HEADS // 2, SSM_STATE, PAIR_WIDTH), F32),
            pltpu.VMEM((chunks, SSM_CHUNK, SSM_WIDTH), F32),
        ],
        compiler_params=_params(("parallel", "arbitrary")),
        name="ssd_mixer",
    )(proj, proj, proj, dt_raw, cwx, cwbc, cbx, cbbc, dt_bias, a_log, d_skip_vec, norm_w)


def _out_proj_kernel(att_ref, ssm_ref, w_ref, x_ref, wd_ref, x1_ref, wdb_ref, wb_ref):
    @pl.when(pl.program_id(1) == 0)
    def _():
        wb_ref[...] = w_ref[...].astype(BF16)

    acc = jnp.dot(att_ref[...], wb_ref[0], preferred_element_type=F32)
    acc = acc + jnp.dot(ssm_ref[...], wb_ref[1], preferred_element_type=F32)
    x1_ref[...] = x_ref[...] + acc
    wdb_ref[...] = wd_ref[...].astype(BF16)


def _out_proj(att, ssm, w_out2, x2d, w_down, tm=512, tn=1024):
    m = att.shape[0]
    ni = m // tm
    n_steps = (D_MODEL // tn) * ni
    d_ff = w_down.shape[0]
    wd_rows = d_ff // n_steps
    assert wd_rows * n_steps == d_ff and wd_rows % ONES_ROWS == 0
    return pl.pallas_call(
        _out_proj_kernel,
        grid=(D_MODEL // tn, ni),
        in_specs=[
            pl.BlockSpec((tm, ATT_WIDTH), lambda j, i: (i, 0)),
            pl.BlockSpec((tm, SSM_WIDTH), lambda j, i: (i, 0)),
            pl.BlockSpec((2, ATT_WIDTH, tn), lambda j, i: (0, 0, j), pipeline_mode=pl.Buffered(1)),
            pl.BlockSpec((tm, tn), lambda j, i: (i, j)),
            pl.BlockSpec((wd_rows, D_MODEL), lambda j, i: (j * ni + i, 0)),
        ],
        out_specs=[
            pl.BlockSpec((tm, tn), lambda j, i: (i, j)),
            pl.BlockSpec((wd_rows, D_MODEL), lambda j, i: (j * ni + i, 0)),
        ],
        out_shape=[
            jax.ShapeDtypeStruct((m, D_MODEL), F32),
            jax.ShapeDtypeStruct(w_down.shape, BF16),
        ],
        scratch_shapes=[pltpu.VMEM((2, ATT_WIDTH, tn), BF16)],
        compiler_params=_params(("parallel", "arbitrary")),
        name="out_proj",
    )(att, ssm, w_out2, x2d, w_down)


def _gate_up_kernel(x_ref, nw_ref, wg_ref, wu_ref, a_ref, h_ref):
    def gated(h):
        g = jnp.dot(h, wg_ref[...].astype(BF16), preferred_element_type=F32)
        u = jnp.dot(h, wu_ref[...].astype(BF16), preferred_element_type=F32)
        a_ref[...] = ((g * _sigmoid(g)) * u).astype(BF16)

    @pl.when(pl.program_id(1) == 0)
    def _():
        h = _rmsnorm(x_ref[...], nw_ref[...], RMS_EPS).astype(BF16)
        h_ref[...] = h
        gated(h)

    @pl.when(pl.program_id(1) > 0)
    def _():
        gated(h_ref[...])


def _gate_up(x1, norm_w, w_gate, w_up, tm=1024, tn=512):
    m = x1.shape[0]
    n = w_gate.shape[1]
    return pl.pallas_call(
        _gate_up_kernel,
        grid=(m // tm, n // tn),
        in_specs=[
            pl.BlockSpec((tm, D_MODEL), lambda i, j: (i, 0)),
            pl.BlockSpec((1, D_MODEL), lambda i, j: (0, 0)),
            pl.BlockSpec((D_MODEL, tn), lambda i, j: (0, j)),
            pl.BlockSpec((D_MODEL, tn), lambda i, j: (0, j)),
        ],
        out_specs=pl.BlockSpec((tm, tn), lambda i, j: (i, j)),
        out_shape=jax.ShapeDtypeStruct((m, n), BF16),
        scratch_shapes=[pltpu.VMEM((tm, D_MODEL), BF16)],
        compiler_params=_params(("parallel", "arbitrary")),
        name="gate_up",
    )(x1, norm_w, w_gate, w_up)


def _down_kernel(a_ref, w_ref, x_ref, nw_ref, o_ref, x2_ref):
    j = pl.program_id(1)
    x2_ref[j] = x_ref[...] + jnp.dot(a_ref[...], w_ref[...], preferred_element_type=F32)

    @pl.when(j == pl.num_programs(1) - 1)
    def _():
        x2 = jnp.concatenate([x2_ref[t] for t in range(x2_ref.shape[0])], axis=1)
        o_ref[...] = _rmsnorm(x2, nw_ref[...], RMS_EPS)


def _down(a, w_down, x1, norm_w, tm=512, tn=1024):
    m, kdim = a.shape
    return pl.pallas_call(
        _down_kernel,
        grid=(m // tm, D_MODEL // tn),
        in_specs=[
            pl.BlockSpec((tm, kdim), lambda i, j: (i, 0)),
            pl.BlockSpec((kdim, tn), lambda i, j: (0, j)),
            pl.BlockSpec((tm, tn), lambda i, j: (i, j)),
            pl.BlockSpec((1, D_MODEL), lambda i, j: (0, 0)),
        ],
        out_specs=pl.BlockSpec((tm, D_MODEL), lambda i, j: (i, 0)),
        out_shape=jax.ShapeDtypeStruct((m, D_MODEL), F32),
        scratch_shapes=[pltpu.VMEM((D_MODEL // tn, tm, tn), F32)],
        compiler_params=_params(("parallel", "arbitrary")),
        name="down_proj",
    )(a, w_down, x1, norm_w)


def _alibi_slopes_log2(n):
    start = 2.0 ** (-8.0 / n)
    return jnp.asarray([start ** (i + 1) for i in range(n)], dtype=F32) * LOG2E


def _pad_lanes(v):
    return jnp.pad(v.astype(F32), (0, LANES - v.shape[0])).reshape(1, LANES)


def kernel(x, norm_mix_w, w_in, lambda_q1, lambda_k1, lambda_q2, lambda_k2, subln_w, conv_w, conv_b, dt_bias, a_log, d_skip, ssm_norm_w, w_out, norm_ffn_w, w_gate, w_up, w_down, norm_final_w):
    batch, seq, _ = x.shape
    assert w_in.shape[0] == 1, "single-layer block"
    layer = 0
    x2d = x.reshape(batch * seq, D_MODEL)
    col_scale = jnp.concatenate([
        jnp.full((ATT_WIDTH,), ATT_QK_DIM ** -0.5 * LOG2E, F32),
        jnp.ones((PROJ_MAIN - ATT_WIDTH,), F32)]).reshape(1, PROJ_MAIN)
    slopes2 = _alibi_slopes_log2(ATT_HEADS)
    lambda_init = 0.8 - 0.6 * math.exp(-0.3 * layer)
    w_in_t = w_in[layer].T
    w_dt_t = jnp.pad(w_in_t[PROJ_MAIN:], ((0, LANES - SSM_HEADS), (0, 0))).astype(BF16)
    proj, dt_raw = _in_proj(x2d, norm_mix_w[layer].reshape(1, D_MODEL), w_in_t, w_dt_t, col_scale)
    lamv = jnp.stack([lambda_q1[layer], lambda_k1[layer], lambda_q2[layer], lambda_k2[layer]]).astype(F32)
    subw_col = jnp.broadcast_to(subln_w[layer].astype(F32).reshape(ATT_V_DIM, 1), (ATT_V_DIM, LANES))
    att = _attention(proj, slopes2, lamv, subw_col, batch, seq, lambda_init)
    ssm = _ssd(proj, dt_raw, conv_w[layer], conv_b[layer].reshape(1, -1),
               _pad_lanes(dt_bias[layer]), _pad_lanes(a_log[layer]),
               jnp.repeat(d_skip[layer].astype(F32), SSM_HEAD_DIM).reshape(1, SSM_WIDTH),
               ssm_norm_w[layer].reshape(1, SSM_WIDTH), batch, seq)
    w_out2 = w_out[layer].reshape(2, ATT_WIDTH, D_MODEL)
    x1, w_down_b = _out_proj(att, ssm, w_out2, x2d, w_down[layer])
    a = _gate_up(x1, norm_ffn_w[layer].reshape(1, D_MODEL), w_gate[layer], w_up[layer])
    out = _down(a, w_down_b, x1, norm_final_w.reshape(1, D_MODEL))
    return out.reshape(batch, seq, D_MODEL)
```

```python
import functools
import math

import jax
import jax.numpy as jnp
from jax import lax
from jax.experimental import pallas as pl
from jax.experimental.pallas import tpu as pltpu

F32 = jnp.float32
BF16 = jnp.bfloat16

D_MODEL = 2048
ATT_HEADS = 8
ATT_QK_DIM = 128
ATT_V_DIM = 2 * ATT_QK_DIM
ATT_WIDTH = ATT_HEADS * ATT_V_DIM
SSM_HEADS = 32
SSM_HEAD_DIM = 64
SSM_WIDTH = SSM_HEADS * SSM_HEAD_DIM
SSM_GROUPS = 4
SSM_STATE = 128
SSM_CONV = 4
SSM_CHUNK = 128
SSM_BC_WIDTH = 2 * SSM_GROUPS * SSM_STATE
PROJ_MAIN = 2 * ATT_WIDTH + ATT_WIDTH + SSM_WIDTH + SSM_WIDTH + SSM_BC_WIDTH
RMS_EPS = 1e-6
SUB_EPS = 1e-5
LOG2E = 1.4426950408889634
LANES = 128
SUBLANES = 8
ONES_ROWS = 16
LOOKAHEAD = 4
HEADS_PER_GROUP = SSM_HEADS // SSM_GROUPS
PAIR_WIDTH = 2 * SSM_HEAD_DIM
NEG_INF = float("-inf")

COL_Q, COL_K, COL_V = 0, ATT_WIDTH, 2 * ATT_WIDTH
COL_Z = 3 * ATT_WIDTH
COL_XS = COL_Z + SSM_WIDTH
COL_BC = COL_XS + SSM_WIDTH

VMEM_LIMIT = 56 * 1024 * 1024


def _params(semantics):
    return pltpu.CompilerParams(dimension_semantics=semantics, vmem_limit_bytes=VMEM_LIMIT)


def _sigmoid(x):
    return 1.0 / (1.0 + jnp.exp(-x))


def _rmsnorm(x, w, eps):
    ms = jnp.mean(x * x, axis=-1, keepdims=True)
    return (x * lax.rsqrt(ms + eps)) * w


def _lane_tile(x, width):
    return x if width == LANES else jnp.concatenate([x] * (width // LANES), axis=1)


_CONTRACT_LAST = (((1,), (1,)), ((), ()))


def _in_proj_kernel(x_ref, nw_ref, wt_ref, wdt_ref, cs_ref, proj_ref, dt_ref, h_ref):
    def project(hb):
        acc = lax.dot_general(hb, wt_ref[...].astype(BF16), _CONTRACT_LAST,
                              preferred_element_type=F32)
        proj_ref[...] = (acc * cs_ref[...]).astype(BF16)

    @pl.when(pl.program_id(1) == 0)
    def _():
        hb = _rmsnorm(x_ref[...], nw_ref[...], RMS_EPS).astype(BF16)
        h_ref[...] = hb
        dt_ref[...] = lax.dot_general(hb, wdt_ref[...], _CONTRACT_LAST,
                                      preferred_element_type=F32)
        project(hb)

    @pl.when(pl.program_id(1) > 0)
    def _():
        project(h_ref[...])


def _in_proj(x2d, norm_w, w_in_t, w_dt_t, col_scale, tm=1024, tn=1024):
    m = x2d.shape[0]
    return pl.pallas_call(
        _in_proj_kernel,
        grid=(m // tm, PROJ_MAIN // tn),
        in_specs=[
            pl.BlockSpec((tm, D_MODEL), lambda i, j: (i, 0)),
            pl.BlockSpec((1, D_MODEL), lambda i, j: (0, 0)),
            pl.BlockSpec((tn, D_MODEL), lambda i, j: (j, 0)),
            pl.BlockSpec((LANES, D_MODEL), lambda i, j: (0, 0)),
            pl.BlockSpec((1, tn), lambda i, j: (0, j)),
        ],
        out_specs=[
            pl.BlockSpec((tm, tn), lambda i, j: (i, j)),
            pl.BlockSpec((tm, LANES), lambda i, j: (i, 0)),
        ],
        out_shape=[
            jax.ShapeDtypeStruct((m, PROJ_MAIN), BF16),
            jax.ShapeDtypeStruct((m, LANES), F32),
        ],
        scratch_shapes=[pltpu.VMEM((tm, D_MODEL), BF16)],
        compiler_params=_params(("parallel", "arbitrary")),
        name="in_proj",
    )(x2d, norm_w, w_in_t, w_dt_t, col_scale)


def _attn_kernel(slope_ref, lamv_ref, subw_ref, q_ref, k_ref, v_ref, o_ref,
                 vt_scr, qt_scr, kaug_scr, m_scr, acc_scr, *, tq, tk, qc, lambda_init):
    slope2 = slope_ref[pl.program_id(1)]
    aug_lane = lax.broadcasted_iota(jnp.int32, (tk, ATT_QK_DIM), 1)

    bias = slope2 * lax.broadcasted_iota(jnp.int32, (tk, ATT_QK_DIM), 0).astype(F32)
    hi = bias.astype(BF16).astype(F32)
    mid = (bias - hi).astype(BF16).astype(F32)
    lo = (bias - hi) - mid
    kaug_scr[...] = jnp.where(aug_lane == 0, hi, jnp.where(
        aug_lane == 1, mid, jnp.where(aug_lane == 2, lo, 0.0))).astype(BF16)

    row_minus_col = (lax.broadcasted_iota(jnp.int32, (tk, qc), 0)
                     - lax.broadcasted_iota(jnp.int32, (tk, qc), 1))
    pick = jnp.where(lax.broadcasted_iota(jnp.int32, (ATT_QK_DIM, tq), 0) < 3, 1.0, 0.0).astype(BF16)
    n_qchunks = tq // qc
    lv = lamv_ref[...]
    lam = (jnp.exp(jnp.sum(lv[0:1] * lv[1:2], axis=1, keepdims=True))
           - jnp.exp(jnp.sum(lv[2:3] * lv[3:4], axis=1, keepdims=True)) + lambda_init)

    def block_chains(d=None):
        out = []
        for u in range(n_qchunks):
            if d is not None and d * tk >= (u + 1) * qc:
                continue
            thr = None if d is None or (d + 1) * tk <= u * qc else u * qc - d * tk
            out += [(u, c, thr) for c in range(2)]
        return out

    def q_tile(t):
        rows = slice(t * tq, (t + 1) * tq)

        qt = q_ref[rows, :].T
        for c in range(2):
            qt_scr[t, c] = jnp.concatenate([qt[c * ATT_QK_DIM:(c + 1) * ATT_QK_DIM, :], pick], axis=0)
        m_scr[t] = jnp.full(m_scr.shape[1:], NEG_INF, F32)
        acc_scr[t] = jnp.zeros(acc_scr.shape[1:], F32)

        def scores(j, c, u):
            dims = slice(c * ATT_QK_DIM, (c + 1) * ATT_QK_DIM)
            k_aug = jnp.concatenate([k_ref[j * tk:(j + 1) * tk, dims], kaug_scr[...]], axis=1)
            return jnp.dot(k_aug, qt_scr[t, c, :, u * qc:(u + 1) * qc], preferred_element_type=F32)

        def chain(j, c, u, thr, st):
            cs = slice(u * qc, (u + 1) * qc)
            if thr is not None:
                st = jnp.where(row_minus_col <= thr, st, NEG_INF)
            off = slope2 * float(j * tk)
            m_old = m_scr[t, c, 0:1, cs]
            m_new = jnp.maximum(m_old, jnp.max(st, axis=0, keepdims=True) + off)
            alpha = jnp.exp2(m_old - m_new)
            pt = jnp.exp2(st - (m_new - off))
            m_scr[t, c, :, cs] = jnp.broadcast_to(m_new, (SUBLANES, qc))
            acc_scr[t, c, :, cs] = alpha * acc_scr[t, c, :, cs] + jnp.dot(
                vt_scr[j], pt.astype(BF16), preferred_element_type=F32)

        n_full = (t * tq) // tk
        for d in range(tq // tk):
            vt = v_ref[(n_full + d) * tk:(n_full + d + 1) * tk, :].T
            vt_scr[n_full + d] = jnp.concatenate([vt, jnp.ones((ONES_ROWS, tk), BF16)], axis=0)
        tasks = [(j, u, c, None) for j in range(n_full) for u, c, _ in block_chains()]
        tasks += [(n_full + d, u, c, thr) for d in range(tq // tk) for u, c, thr in block_chains(d)]
        sts = [scores(j, c, u) for j, u, c, _ in tasks[:LOOKAHEAD]]
        for i, (j, u, c, thr) in enumerate(tasks):
            if i + LOOKAHEAD < len(tasks):
                jn, un, cn, _ = tasks[i + LOOKAHEAD]
                sts.append(scores(jn, cn, un))
            chain(j, c, u, thr, sts[i])
            sts[i] = None

        inv_l0 = 1.0 / acc_scr[t, 0, ATT_V_DIM:ATT_V_DIM + 1, :]
        inv_l1 = lam / acc_scr[t, 1, ATT_V_DIM:ATT_V_DIM + 1, :]
        ot = (acc_scr[t, 0, :ATT_V_DIM, :] * inv_l0
              - acc_scr[t, 1, :ATT_V_DIM, :] * inv_l1)
        ms = jnp.mean(ot * ot, axis=0, keepdims=True)
        ot = (ot * lax.rsqrt(ms + SUB_EPS)) * _lane_tile(subw_ref[...], tq)
        o_ref[rows, :] = (ot * (1.0 - lambda_init)).astype(BF16).T

    for t in range(q_ref.shape[0] // tq):
        q_tile(t)


def _attention(proj, slopes2, lamv, subw_col, batch, seq, lambda_init, tq=1024, tk=256, qc=256):
    assert tq % tk == 0 and tq % qc == 0 and seq % tq == 0
    tiles = seq // tq
    kern = functools.partial(_attn_kernel, tq=tq, tk=tk, qc=qc, lambda_init=lambda_init)
    kb, vb = COL_K // ATT_V_DIM, COL_V // ATT_V_DIM
    return pl.pallas_call(
        kern,
        grid=(batch, ATT_HEADS),
        in_specs=[
            pl.BlockSpec(memory_space=pltpu.SMEM),
            pl.BlockSpec((4, ATT_QK_DIM), lambda b, h: (0, 0)),
            pl.BlockSpec((ATT_V_DIM, LANES), lambda b, h: (0, 0)),
            pl.BlockSpec((seq, ATT_V_DIM), lambda b, h: (b, h)),
            pl.BlockSpec((seq, ATT_V_DIM), lambda b, h: (b, kb + h)),
            pl.BlockSpec((seq, ATT_V_DIM), lambda b, h: (b, vb + h)),
        ],
        out_specs=pl.BlockSpec((seq, ATT_V_DIM), lambda b, h: (b, h)),
        out_shape=jax.ShapeDtypeStruct((batch * seq, ATT_WIDTH), BF16),
        scratch_shapes=[
            pltpu.VMEM((seq // tk, ATT_V_DIM + ONES_ROWS, tk), BF16),
            pltpu.VMEM((tiles, 2, 2 * ATT_QK_DIM, tq), BF16),
            pltpu.VMEM((tk, ATT_QK_DIM), BF16),
            pltpu.VMEM((tiles, 2, SUBLANES, tq), F32),
            pltpu.VMEM((tiles, 2, ATT_V_DIM + ONES_ROWS, tq), F32),
        ],
        compiler_params=_params(("parallel", "parallel")),
        name="diff_attention",
    )(slopes2, lamv, subw_col, proj, proj, proj)


def _ssd_kernel(z_ref, xs_ref, bc_ref, dt_ref, cwx_ref, cwbc_ref, cbx_ref, cbbc_ref,
                dtb_ref, alog_ref, dskip_ref, nw_ref, out_ref,
                tailx, tailbc, state, yscr):
    @pl.when(pl.program_id(1) == 0)
    def _():
        tailx[...] = jnp.zeros(tailx.shape, F32)
        tailbc[...] = jnp.zeros(tailbc.shape, F32)
        state[...] = jnp.zeros(state.shape, F32)

    for s in range(yscr.shape[0]):
        rows = pl.ds(s * SSM_CHUNK, SSM_CHUNK)
        _ssd_chunk(z_ref.at[rows], xs_ref.at[rows], bc_ref.at[rows], dt_ref.at[rows],
                   cwx_ref, cwbc_ref, cbx_ref, cbbc_ref, dtb_ref, alog_ref, dskip_ref, nw_ref,
                   out_ref.at[rows], tailx, tailbc, state, yscr.at[s])


def _ssd_chunk(z_ref, xs_ref, bc_ref, dt_ref, cwx_ref, cwbc_ref, cbx_ref, cbbc_ref,
               dtb_ref, alog_ref, dskip_ref, nw_ref, out_ref,
               tailx, tailbc, state, yscr):
    L = SSM_CHUNK

    def conv_silu(u_ref, tail, w_ref, b_ref):
        u = u_ref[...].astype(F32)
        full = jnp.concatenate([tail[...], u], axis=0)
        acc = b_ref[...] + w_ref[SSM_CONV - 1:SSM_CONV, :] * u
        for k in range(SSM_CONV - 1):
            lo = SUBLANES - (SSM_CONV - 1) + k
            acc = acc + w_ref[k:k + 1, :] * full[lo:lo + L]
        tail[...] = u[L - SUBLANES:L]
        return acc * _sigmoid(acc)

    xs = conv_silu(xs_ref, tailx, cwx_ref, cbx_ref)
    bc = conv_silu(bc_ref, tailbc, cwbc_ref, cbbc_ref)
    xs_b = xs.astype(BF16)
    bc_b = bc.astype(BF16)

    dtr = dt_ref[...] + dtb_ref[...]
    dt = jnp.maximum(dtr, 0.0) + jnp.log1p(jnp.exp(-jnp.abs(dtr)))
    acs = dt * (-jnp.exp(alog_ref[...]))
    rowi = lax.broadcasted_iota(jnp.int32, (L, LANES), 0)
    shift = 1
    while shift < L:
        acs = acs + jnp.where(rowi >= shift, pltpu.roll(acs, shift, axis=0), 0.0)
        shift *= 2
    acs = acs * LOG2E
    acs_t = acs.T
    dt_t = dt.T
    exp_a = jnp.exp2(acs)
    w_t = dt_t * jnp.exp2(acs_t[:, L - 1:L] - acs_t)
    cd_row = jnp.exp2(acs[L - 1:L, :])
    src_t = acs_t - jnp.log2(dt_t)

    li = lax.broadcasted_iota(jnp.int32, (L, L), 0)
    si = lax.broadcasted_iota(jnp.int32, (L, L), 1)
    causal = si <= li
    lane = lax.broadcasted_iota(jnp.int32, (L, PAIR_WIDTH), 1)
    lo_half = lane < SSM_HEAD_DIM
    lane_row = lax.broadcasted_iota(jnp.int32, (1, PAIR_WIDTH), 1) < SSM_HEAD_DIM
    zero_b = jnp.zeros((L, PAIR_WIDTH), BF16)

    gn = SSM_GROUPS * SSM_STATE
    for g in range(SSM_GROUPS):
        b_g = bc[:, g * SSM_STATE:(g + 1) * SSM_STATE]
        c_g = bc[:, gn + g * SSM_STATE:gn + (g + 1) * SSM_STATE]
        cb = lax.dot_general(bc_b[:, gn + g * SSM_STATE:gn + (g + 1) * SSM_STATE],
                             bc_b[:, g * SSM_STATE:(g + 1) * SSM_STATE],
                             (((1,), (1,)), ((), ())), preferred_element_type=F32)
        b_gt = b_g.T
        for pr in range(HEADS_PER_GROUP // 2):
            pair = g * (HEADS_PER_GROUP // 2) + pr
            lhs_y, lhs_s = [], []
            for hh in (2 * pair, 2 * pair + 1):
                seg = acs[:, hh:hh + 1] - src_t[hh:hh + 1, :]
                lhs_y.append((cb * jnp.exp2(jnp.where(causal, seg, NEG_INF))).astype(BF16))
                lhs_s.append((b_gt * w_t[hh:hh + 1, :]).astype(BF16))
            for hh in (2 * pair, 2 * pair + 1):
                lhs_y.append((c_g * exp_a[:, hh:hh + 1]).astype(BF16))
            x_p = xs_b[:, pair * PAIR_WIDTH:(pair + 1) * PAIR_WIDTH]
            x_lo = jnp.where(lo_half, x_p, zero_b)
            x_hi = jnp.where(lo_half, zero_b, x_p)
            st = state[pair]
            st_b = st.astype(BF16)
            st_lo = jnp.where(lo_half, st_b, zero_b)
            st_hi = jnp.where(lo_half, zero_b, st_b)
            y = jnp.dot(jnp.concatenate(lhs_y, axis=1),
                        jnp.concatenate([x_lo, x_hi, st_lo, st_hi], axis=0),
                        preferred_element_type=F32)
            yscr[:, pair * PAIR_WIDTH:(pair + 1) * PAIR_WIDTH] = y
            new = jnp.dot(jnp.concatenate(lhs_s, axis=1),
                          jnp.concatenate([x_lo, x_hi], axis=0),
                          preferred_element_type=F32)
            cd = jnp.where(lane_row, cd_row[:, 2 * pair:2 * pair + 1],
                           cd_row[:, 2 * pair + 1:2 * pair + 2])
            state[pair] = st * cd + new

    y = yscr[...] + dskip_ref[...] * xs
    z = z_ref[...].astype(F32)
    y = y * (z * _sigmoid(z))
    gw = SSM_WIDTH // SSM_GROUPS
    for g in range(SSM_GROUPS):
        cols = slice(g * gw, (g + 1) * gw)
        out_ref[:, cols] = _rmsnorm(y[:, cols], nw_ref[:, cols], SUB_EPS).astype(BF16)


def _ssd(proj, dt_raw, conv_w, conv_b, dt_bias, a_log, d_skip_vec, norm_w, batch, seq, chunks=8):
    L = chunks * SSM_CHUNK
    nc = seq // L
    assert nc * L == seq
    row = lambda b, c: b * nc + c
    full = lambda shape: pl.BlockSpec(shape, lambda b, c: (0, 0))
    cwx, cwbc = conv_w[:, :SSM_WIDTH], conv_w[:, SSM_WIDTH:]
    cbx, cbbc = conv_b[:, :SSM_WIDTH], conv_b[:, SSM_WIDTH:]
    return pl.pallas_call(
        _ssd_kernel,
        grid=(batch, nc),
        in_specs=[
            pl.BlockSpec((L, SSM_WIDTH), lambda b, c: (row(b, c), COL_Z // SSM_WIDTH)),
            pl.BlockSpec((L, SSM_WIDTH), lambda b, c: (row(b, c), COL_XS // SSM_WIDTH)),
            pl.BlockSpec((L, SSM_BC_WIDTH), lambda b, c: (row(b, c), COL_BC // SSM_BC_WIDTH)),
            pl.BlockSpec((L, LANES), lambda b, c: (row(b, c), 0)),
            full((SSM_CONV, SSM_WIDTH)),
            full((SSM_CONV, SSM_BC_WIDTH)),
            full((1, SSM_WIDTH)),
            full((1, SSM_BC_WIDTH)),
            full((1, LANES)),
            full((1, LANES)),
            full((1, SSM_WIDTH)),
            full((1, SSM_WIDTH)),
        ],
        out_specs=pl.BlockSpec((L, SSM_WIDTH), lambda b, c: (row(b, c), 0)),
        out_shape=jax.ShapeDtypeStruct((batch * seq, SSM_WIDTH), BF16),
        scratch_shapes=[
            pltpu.VMEM((SUBLANES, SSM_WIDTH), F32),
            pltpu.VMEM((SUBLANES, SSM_BC_WIDTH), F32),
            pltpu.VMEM((SSM_HEADS // 2, SSM_STATE, PAIR_WIDTH), F32),
            pltpu.VMEM((chunks, SSM_CHUNK, SSM_WIDTH), F32),
        ],
        compiler_params=_params(("parallel", "arbitrary")),
        name="ssd_mixer",
    )(proj, proj, proj, dt_raw, cwx, cwbc, cbx, cbbc, dt_bias, a_log, d_skip_vec, norm_w)


def _out_proj_kernel(att_ref, ssm_ref, w_ref, x_ref, wd_ref, x1_ref, wdb_ref, wb_ref):
    @pl.when(pl.program_id(1) == 0)
    def _():
        wb_ref[...] = w_ref[...].astype(BF16)

    acc = jnp.dot(att_ref[...], wb_ref[0], preferred_element_type=F32)
    acc = acc + jnp.dot(ssm_ref[...], wb_ref[1], preferred_element_type=F32)
    x1_ref[...] = x_ref[...] + acc
    wdb_ref[...] = wd_ref[...].astype(BF16)


def _out_proj(att, ssm, w_out2, x2d, w_down, tm=512, tn=1024):
    m = att.shape[0]
    ni = m // tm
    n_steps = (D_MODEL // tn) * ni
    d_ff = w_down.shape[0]
    wd_rows = d_ff // n_steps
    assert wd_rows * n_steps == d_ff and wd_rows % ONES_ROWS == 0
    return pl.pallas_call(
        _out_proj_kernel,
        grid=(D_MODEL // tn, ni),
        in_specs=[
            pl.BlockSpec((tm, ATT_WIDTH), lambda j, i: (i, 0)),
            pl.BlockSpec((tm, SSM_WIDTH), lambda j, i: (i, 0)),
            pl.BlockSpec((2, ATT_WIDTH, tn), lambda j, i: (0, 0, j), pipeline_mode=pl.Buffered(1)),
            pl.BlockSpec((tm, tn), lambda j, i: (i, j)),
            pl.BlockSpec((wd_rows, D_MODEL), lambda j, i: (j * ni + i, 0)),
        ],
        out_specs=[
            pl.BlockSpec((tm, tn), lambda j, i: (i, j)),
            pl.BlockSpec((wd_rows, D_MODEL), lambda j, i: (j * ni + i, 0)),
        ],
        out_shape=[
            jax.ShapeDtypeStruct((m, D_MODEL), F32),
            jax.ShapeDtypeStruct(w_down.shape, BF16),
        ],
        scratch_shapes=[pltpu.VMEM((2, ATT_WIDTH, tn), BF16)],
        compiler_params=_params(("parallel", "arbitrary")),
        name="out_proj",
    )(att, ssm, w_out2, x2d, w_down)


def _gate_up_kernel(x_ref, nw_ref, wg_ref, wu_ref, a_ref, h_ref):
    def gated(h):
        g = jnp.dot(h, wg_ref[...].astype(BF16), preferred_element_type=F32)
        u = jnp.dot(h, wu_ref[...].astype(BF16), preferred_element_type=F32)
        a_ref[...] = ((g * _sigmoid(g)) * u).astype(BF16)

    @pl.when(pl.program_id(1) == 0)
    def _():
        h = _rmsnorm(x_ref[...], nw_ref[...], RMS_EPS).astype(BF16)
        h_ref[...] = h
        gated(h)

    @pl.when(pl.program_id(1) > 0)
    def _():
        gated(h_ref[...])


def _gate_up(x1, norm_w, w_gate, w_up, tm=1024, tn=512):
    m = x1.shape[0]
    n = w_gate.shape[1]
    return pl.pallas_call(
        _gate_up_kernel,
        grid=(m // tm, n // tn),
        in_specs=[
            pl.BlockSpec((tm, D_MODEL), lambda i, j: (i, 0)),
            pl.BlockSpec((1, D_MODEL), lambda i, j: (0, 0)),
            pl.BlockSpec((D_MODEL, tn), lambda i, j: (0, j)),
            pl.BlockSpec((D_MODEL, tn), lambda i, j: (0, j)),
        ],
        out_specs=pl.BlockSpec((tm, tn), lambda i, j: (i, j)),
        out_shape=jax.ShapeDtypeStruct((m, n), BF16),
        scratch_shapes=[pltpu.VMEM((tm, D_MODEL), BF16)],
        compiler_params=_params(("parallel", "arbitrary")),
        name="gate_up",
    )(x1, norm_w, w_gate, w_up)


def _down_kernel(a_ref, w_ref, x_ref, nw_ref, o_ref, x2_ref):
    j = pl.program_id(1)
    x2_ref[j] = x_ref[...] + jnp.dot(a_ref[...], w_ref[...], preferred_element_type=F32)

    @pl.when(j == pl.num_programs(1) - 1)
    def _():
        x2 = jnp.concatenate([x2_ref[t] for t in range(x2_ref.shape[0])], axis=1)
        o_ref[...] = _rmsnorm(x2, nw_ref[...], RMS_EPS)


def _down(a, w_down, x1, norm_w, tm=512, tn=1024):
    m, kdim = a.shape
    return pl.pallas_call(
        _down_kernel,
        grid=(m // tm, D_MODEL // tn),
        in_specs=[
            pl.BlockSpec((tm, kdim), lambda i, j: (i, 0)),
            pl.BlockSpec((kdim, tn), lambda i, j: (0, j)),
            pl.BlockSpec((tm, tn), lambda i, j: (i, j)),
            pl.BlockSpec((1, D_MODEL), lambda i, j: (0, 0)),
        ],
        out_specs=pl.BlockSpec((tm, D_MODEL), lambda i, j: (i, 0)),
        out_shape=jax.ShapeDtypeStruct((m, D_MODEL), F32),
        scratch_shapes=[pltpu.VMEM((D_MODEL // tn, tm, tn), F32)],
        compiler_params=_params(("parallel", "arbitrary")),
        name="down_proj",
    )(a, w_down, x1, norm_w)


def _alibi_slopes_log2(n):
    start = 2.0 ** (-8.0 / n)
    return jnp.asarray([start ** (i + 1) for i in range(n)], dtype=F32) * LOG2E


def _pad_lanes(v):
    return jnp.pad(v.astype(F32), (0, LANES - v.shape[0])).reshape(1, LANES)


def kernel(x, norm_mix_w, w_in, lambda_q1, lambda_k1, lambda_q2, lambda_k2, subln_w, conv_w, conv_b, dt_bias, a_log, d_skip, ssm_norm_w, w_out, norm_ffn_w, w_gate, w_up, w_down, norm_final_w):
    batch, seq, _ = x.shape
    assert w_in.shape[0] == 1, "single-layer block"
    layer = 0
    x2d = x.reshape(batch * seq, D_MODEL)
    col_scale = jnp.concatenate([
        jnp.full((ATT_WIDTH,), ATT_QK_DIM ** -0.5 * LOG2E, F32),
        jnp.ones((PROJ_MAIN - ATT_WIDTH,), F32)]).reshape(1, PROJ_MAIN)
    slopes2 = _alibi_slopes_log2(ATT_HEADS)
    lambda_init = 0.8 - 0.6 * math.exp(-0.3 * layer)
    w_in_t = w_in[layer].T
    w_dt_t = jnp.pad(w_in_t[PROJ_MAIN:], ((0, LANES - SSM_HEADS), (0, 0))).astype(BF16)
    proj, dt_raw = _in_proj(x2d, norm_mix_w[layer].reshape(1, D_MODEL), w_in_t, w_dt_t, col_scale)
    lamv = jnp.stack([lambda_q1[layer], lambda_k1[layer], lambda_q2[layer], lambda_k2[layer]]).astype(F32)
    subw_col = jnp.broadcast_to(subln_w[layer].astype(F32).reshape(ATT_V_DIM, 1), (ATT_V_DIM, LANES))
    att = _attention(proj, slopes2, lamv, subw_col, batch, seq, lambda_init)
    ssm = _ssd(proj, dt_raw, conv_w[layer], conv_b[layer].reshape(1, -1),
               _pad_lanes(dt_bias[layer]), _pad_lanes(a_log[layer]),
               jnp.repeat(d_skip[layer].astype(F32), SSM_HEAD_DIM).reshape(1, SSM_WIDTH),
               ssm_norm_w[layer].reshape(1, SSM_WIDTH), batch, seq)
    w_out2 = w_out[layer].reshape(2, ATT_WIDTH, D_MODEL)
    x1, w_down_b = _out_proj(att, ssm, w_out2, x2d, w_down[layer])
    a = _gate_up(x1, norm_ffn_w[layer].reshape(1, D_MODEL), w_gate[layer], w_up[layer])
    out = _down(a, w_down_b, x1, norm_final_w.reshape(1, D_MODEL))
    return out.reshape(batch, seq, D_MODEL)
```
